```python
import math
import jax, jax.numpy as jnp
from jax import lax
import numpy as np

D_MODEL = 1024
BATCH = 2
SEQ = 8192
DEPTH = 1

N_MEM = 256
D_MIX = D_MODEL
D_RNN = D_MIX // 2
RNN_HEADS = 8
RNN_HEAD_DIM = D_RNN // RNN_HEADS
CONV_WIDTH = 4
LRU_C = 8.0
D_POOL = D_MIX // 4
POOL_WINDOWS = (2, 4, 8, 16)
POOL_GROUPS = len(POOL_WINDOWS)
POOL_GROUP_DIM = D_POOL // POOL_GROUPS
MAX_WINDOW = max(POOL_WINDOWS)
D_XATTN = D_MIX // 4
XATTN_HEADS = 4
XATTN_HEAD_DIM = D_XATTN // XATTN_HEADS
D_IN = 2 * D_RNN + D_POOL + D_XATTN
N_EXPERTS = 32
TOP_K = 4
D_FF = D_MODEL
SWIGLU_ALPHA = 1.702
SWIGLU_LIMIT = 7.0
MOE_BLOCK = 128
RMS_EPS = 1e-6

kernel_name = "hymba_rglru_pool_memxattn_moe"


def rmsnorm(x, g):
    xf = x.astype(jnp.float32)
    y = xf * lax.rsqrt(jnp.mean(xf * xf, axis=-1, keepdims=True) + RMS_EPS)
    return (y * g.astype(jnp.float32)).astype(x.dtype)


def causal_depthwise_conv(u, w, b):
    y = lax.conv_general_dilated(
        u, w[:, None, :].astype(u.dtype), window_strides=(1,),
        padding=[(CONV_WIDTH - 1, 0)],
        dimension_numbers=("NWC", "WIO", "NWC"),
        feature_group_count=u.shape[-1])
    return y + b.astype(u.dtype)


def rg_lru(u, w_a, b_a, w_i, b_i, lam):
    B, S, C = u.shape
    uh = u.reshape(B, S, RNN_HEADS, RNN_HEAD_DIM)
    r = jax.nn.sigmoid(jnp.einsum("bshi,hij->bshj", uh, w_a).reshape(B, S, C) + b_a)
    i = jax.nn.sigmoid(jnp.einsum("bshi,hij->bshj", uh, w_i).reshape(B, S, C) + b_i)
    log_a = -LRU_C * r.astype(jnp.float32) * jax.nn.softplus(-lam.astype(jnp.float32))
    a = jnp.exp(log_a)
    mult = jnp.sqrt(jnp.maximum(-jnp.expm1(2.0 * log_a), 0.0))
    is_start = (jnp.arange(S) == 0)[None, :, None]
    mult = jnp.where(is_start, 1.0, mult)
    bx = mult * (i * u).astype(jnp.float32)

    def combine(left, right):
        a1, b1 = left
        a2, b2 = right
        return a1 * a2, a2 * b1 + b2

    _, h = lax.associative_scan(combine, (a, bx), axis=1)
    return h.astype(u.dtype)


def multiscale_pool(u, w_pool, scale):
    B, S, _ = u.shape
    uf = u.astype(jnp.float32).reshape(B, S, POOL_GROUPS, POOL_GROUP_DIM)
    csum = jnp.cumsum(uf, axis=1)
    cpad = jnp.pad(csum, ((0, 0), (MAX_WINDOW, 0), (0, 0), (0, 0)))
    pos = jnp.arange(1, S + 1, dtype=jnp.float32)[None, :, None]
    outs = []
    for g, w in enumerate(POOL_WINDOWS):
        prev = cpad[:, MAX_WINDOW - w: MAX_WINDOW - w + S, g]
        mean = (csum[:, :, g] - prev) / jnp.minimum(pos, float(w))
        outs.append(mean - uf[:, :, g])
    d = jnp.stack(outs, axis=2).astype(u.dtype)
    y = jnp.einsum("bsgi,gij->bsgj", d, w_pool).reshape(B, S, D_POOL)
    return y * scale


def memory_cross_attention(q, mem_n, w_mem_kv):
    B, S, _ = q.shape
    kv = mem_n @ w_mem_kv
    k, v = jnp.split(kv, 2, axis=-1)
    qh = q.reshape(B, S, XATTN_HEADS, XATTN_HEAD_DIM)
    kh = k.reshape(B, -1, XATTN_HEADS, XATTN_HEAD_DIM)
    vh = v.reshape(B, -1, XATTN_HEADS, XATTN_HEAD_DIM)
    s = jnp.einsum("bshd,bmhd->bhsm", qh, kh).astype(jnp.float32) * (XATTN_HEAD_DIM ** -0.5)
    p = jax.nn.softmax(s, axis=-1).astype(vh.dtype)
    o = jnp.einsum("bhsm,bmhd->bshd", p, vh)
    return o.reshape(B, S, D_XATTN)


def moe_ffn(x, w_router, b_router, w_gu, b_gu, w_down, b_down):
    B, S, D = x.shape
    T = B * S
    TK = T * TOP_K
    xt = x.reshape(T, D)
    logits = (xt @ w_router + b_router).astype(jnp.float32)
    top_vals, top_idx = lax.top_k(logits, TOP_K)
    gates = jax.nn.softmax(top_vals, axis=-1)
    flat_e = top_idx.reshape(-1).astype(jnp.int32)
    flat_tok = jnp.arange(TK, dtype=jnp.int32) // TOP_K
    order = jnp.argsort(flat_e, stable=True)
    sorted_e = flat_e[order]
    sorted_tok = flat_tok[order]
    sorted_gate = gates.reshape(-1)[order]
    counts = jnp.bincount(flat_e, length=N_EXPERTS)
    padded = ((counts + MOE_BLOCK - 1) // MOE_BLOCK) * MOE_BLOCK
    pad_end = jnp.cumsum(padded)
    pad_start = pad_end - padded
    start = jnp.cumsum(counts) - counts
    dest = pad_start[sorted_e] + (jnp.arange(TK, dtype=jnp.int32) - start[sorted_e])
    n_blocks = (TK + N_EXPERTS * (MOE_BLOCK - 1) + MOE_BLOCK - 1) // MOE_BLOCK
    n_pad = n_blocks * MOE_BLOCK
    buf_tok = jnp.full((n_pad,), T, jnp.int32).at[dest].set(sorted_tok)
    buf_gate = jnp.zeros((n_pad,), jnp.float32).at[dest].set(sorted_gate)
    block_start = jnp.arange(n_blocks, dtype=jnp.int32) * MOE_BLOCK
    block_e = jnp.minimum(jnp.searchsorted(pad_end, block_start, side="right"), N_EXPERTS - 1)
    xpad = jnp.concatenate([xt, jnp.zeros((1, D), xt.dtype)], axis=0)

    def expert_block(args):
        e, tok = args
        xb = xpad[tok]
        gu = xb @ w_gu[e] + b_gu[e]
        g, up = jnp.split(gu, 2, axis=-1)
        g = jnp.minimum(g, SWIGLU_LIMIT)
        up = jnp.clip(up, -SWIGLU_LIMIT, SWIGLU_LIMIT)
        glu = g * jax.nn.sigmoid(g * SWIGLU_ALPHA)
        return ((up + 1.0) * glu) @ w_down[e] + b_down[e]

    yb = lax.map(expert_block, (block_e, buf_tok.reshape(n_blocks, MOE_BLOCK)))
    y = yb.reshape(n_pad, D) * buf_gate[:, None].astype(yb.dtype)
    out = jnp.zeros((T + 1, D), yb.dtype).at[buf_tok].add(y)[:T]
    return out.reshape(B, S, D)


def setup_inputs(seed: int = 0) -> dict:
    key = jax.random.key(seed)
    ks = jax.random.split(key, 24)
    f32 = jnp.float32
    L = DEPTH

    def nrm(k, shape, scale):
        return jax.random.normal(k, shape, f32) * scale

    a_c = jax.random.uniform(ks[9], (L, D_RNN), f32, 0.9, 0.999)
    a0 = a_c ** (1.0 / LRU_C)
    lru_lambda = jnp.log(a0) - jnp.log1p(-a0)
    return {
        "x": nrm(ks[0], (BATCH, SEQ, D_MODEL), 1.0),
        "mem": nrm(ks[1], (BATCH, N_MEM, D_MODEL), 1.0),
        "norm_mix_g": 1.0 + nrm(ks[2], (L, D_MODEL), 0.02),
        "w_in": nrm(ks[3], (L, D_MODEL, D_IN), D_MODEL ** -0.5),
        "conv_w": nrm(ks[4], (L, CONV_WIDTH, D_RNN), CONV_WIDTH ** -0.5),
        "conv_b": nrm(ks[5], (L, D_RNN), 0.01),
        "w_rg_a": nrm(ks[6], (L, RNN_HEADS, RNN_HEAD_DIM, RNN_HEAD_DIM), RNN_HEAD_DIM ** -0.5),
        "b_rg_a": nrm(ks[7], (L, D_RNN), 0.01),
        "w_rg_i": nrm(ks[8], (L, RNN_HEADS, RNN_HEAD_DIM, RNN_HEAD_DIM), RNN_HEAD_DIM ** -0.5),
        "b_rg_i": nrm(ks[10], (L, D_RNN), 0.01),
        "lru_lambda": lru_lambda,
        "w_pool": nrm(ks[11], (L, POOL_GROUPS, POOL_GROUP_DIM, POOL_GROUP_DIM), POOL_GROUP_DIM ** -0.5),
        "pool_scale": 1.0 + nrm(ks[12], (L, D_POOL), 0.1),
        "mem_norm_g": 1.0 + nrm(ks[13], (L, D_MODEL), 0.02),
        "w_mem_kv": nrm(ks[14], (L, D_MODEL, 2 * D_XATTN), D_MODEL ** -0.5),
        "w_out": nrm(ks[15], (L, D_MIX, D_MODEL), D_MIX ** -0.5),
        "norm_ffn_g": 1.0 + nrm(ks[16], (L, D_MODEL), 0.02),
        "w_router": nrm(ks[17], (L, D_MODEL, N_EXPERTS), D_MODEL ** -0.5),
        "b_router": nrm(ks[18], (L, N_EXPERTS), 0.01),
        "w_gu": nrm(ks[19], (L, N_EXPERTS, D_MODEL, 2 * D_FF), D_MODEL ** -0.5),
        "b_gu": nrm(ks[20], (L, N_EXPERTS, 2 * D_FF), 0.01),
        "w_down": nrm(ks[21], (L, N_EXPERTS, D_FF, D_MODEL), D_FF ** -0.5),
        "b_down": nrm(ks[22], (L, N_EXPERTS, D_MODEL), 0.01),
        "final_norm_g": 1.0 + nrm(ks[23], (D_MODEL,), 0.02),
    }


def reference(x, mem, norm_mix_g, w_in, conv_w, conv_b, w_rg_a, b_rg_a, w_rg_i, b_rg_i,
              lru_lambda, w_pool, pool_scale, mem_norm_g, w_mem_kv, w_out, norm_ffn_g,
              w_router, b_router, w_gu, b_gu, w_down, b_down, final_norm_g):
    for l in range(DEPTH):
        h = rmsnorm(x, norm_mix_g[l])
        proj = h @ w_in[l]
        u_rnn = proj[..., :D_RNN]
        g_rnn = proj[..., D_RNN:2 * D_RNN]
        u_pool = proj[..., 2 * D_RNN:2 * D_RNN + D_POOL]
        q_mem = proj[..., 2 * D_RNN + D_POOL:]
        y_rnn = rg_lru(causal_depthwise_conv(u_rnn, conv_w[l], conv_b[l]),
                       w_rg_a[l], b_rg_a[l], w_rg_i[l], b_rg_i[l], lru_lambda[l])
        y_rnn = y_rnn * jax.nn.gelu(g_rnn)
        y_pool = multiscale_pool(u_pool, w_pool[l], pool_scale[l])
        y_mem = memory_cross_attention(q_mem, rmsnorm(mem, mem_norm_g[l]), w_mem_kv[l])
        mixed = jnp.concatenate([y_rnn, y_pool, y_mem], axis=-1)
        x = x + mixed @ w_out[l]
        x = x + moe_ffn(rmsnorm(x, norm_ffn_g[l]), w_router[l], b_router[l],
                        w_gu[l], b_gu[l], w_down[l], b_down[l])
    return rmsnorm(x, final_norm_g)
```

```python
import functools
import math

import jax
import jax.numpy as jnp
from jax import lax
from jax.experimental import pallas as pl
from jax.experimental.pallas import tpu as pltpu

D_MODEL = 1024
N_MEM = 256
D_RNN = 512
RNN_HEADS = 8
RNN_HEAD_DIM = D_RNN // RNN_HEADS
CONV_WIDTH = 4
LRU_C = 8.0
D_POOL = 256
POOL_WINDOWS = (2, 4, 8, 16)
POOL_GROUP_DIM = D_POOL // len(POOL_WINDOWS)
MAX_WINDOW = max(POOL_WINDOWS)
D_XATTN = 256
XATTN_HEADS = 4
XATTN_HEAD_DIM = D_XATTN // XATTN_HEADS
D_IN = 2 * D_RNN + D_POOL + D_XATTN
N_EXPERTS = 32
TOP_K = 4
D_FF = D_MODEL
SWIGLU_ALPHA = 1.702
SWIGLU_LIMIT = 7.0
RMS_EPS = 1e-6

SUBLANES = 8
MXU_DIM = 256
VMEM_LIMIT_BYTES = 56 * 1024 * 1024

MIX_ROWS = 512
HIST = 16
EXPERT_ROWS = 512
ROUTE_ROWS = 256

BF16 = jnp.bfloat16
F32 = jnp.float32


def _rms(xf, g):
    return xf * lax.rsqrt(jnp.mean(xf * xf, axis=-1, keepdims=True) + RMS_EPS) * g


def _dot(a, b):
    return jnp.dot(a, b, preferred_element_type=F32)


def _kv_kernel(mem_ref, g_ref, w_ref, kv_ref):
    mn = _rms(mem_ref[0], g_ref[...])
    kv_ref[0] = _dot(mn.astype(BF16), w_ref[...])


def _kv_call(mem, g, w_kv_bf):
    b = mem.shape[0]
    return pl.pallas_call(
        _kv_kernel,
        grid=(b,),
        in_specs=[
            pl.BlockSpec((1, N_MEM, D_MODEL), lambda i: (i, 0, 0)),
            pl.BlockSpec((1, D_MODEL), lambda i: (0, 0)),
            pl.BlockSpec((D_MODEL, 2 * D_XATTN), lambda i: (0, 0)),
        ],
        out_specs=pl.BlockSpec((1, N_MEM, 2 * D_XATTN), lambda i: (i, 0, 0)),
        out_shape=jax.ShapeDtypeStruct((b, N_MEM, 2 * D_XATTN), F32),
        name="kv_proj",
    )(mem, g, w_kv_bf)


def _mix_kernel(x_ref, g_mix_ref, w_in_ref, conv_w_ref, conv_b_ref, wg_ref, b_a_ref, b_i_ref, lam_ref,
                w_pool_ref, pool_scale_ref, kbd_ref, vbd_ref, w_out_ref, g_ffn_ref, w_rt_ref, b_r_ref,
                x1_ref, hn_ref, idx_ref, gate_ref,
                urnn_buf, upool_buf, hcar, a_s, b_s, mixed_s):
    tm = MIX_ROWS
    s = pl.program_id(1)

    @pl.when(s == 0)
    def _():
        urnn_buf[0:HIST, :] = jnp.zeros((HIST, D_RNN), F32)
        upool_buf[0:HIST, :] = jnp.zeros((HIST, D_POOL), F32)
        hcar[...] = jnp.zeros((1, D_RNN), F32)

    x = x_ref[0]
    h = _rms(x, g_mix_ref[...])
    proj = _dot(h.astype(BF16), w_in_ref[...])
    g_rnn = proj[:, D_RNN:2 * D_RNN]
    q_mem = proj[:, 2 * D_RNN + D_POOL:]
    urnn_buf[HIST:, :] = proj[:, :D_RNN]
    upool_buf[HIST:, :] = proj[:, 2 * D_RNN:2 * D_RNN + D_POOL]

    row = lax.broadcasted_iota(jnp.int32, (tm, 1), 0)
    grow = row + s * tm

    uc = conv_b_ref[...] + conv_w_ref[CONV_WIDTH - 1:CONV_WIDTH, :] * urnn_buf[pl.ds(HIST, tm), :]
    for k in range(CONV_WIDTH - 1):
        off = HIST - (CONV_WIDTH - 1) + k
        uc = uc + conv_w_ref[k:k + 1, :] * urnn_buf[pl.ds(off, tm), :]

    ucb = uc.astype(BF16)
    pre = [_dot(ucb[:, c * MXU_DIM:(c + 1) * MXU_DIM], wg_ref[c]) for c in range(D_RNN // MXU_DIM)]
    pre_a = jnp.concatenate([p[:, :MXU_DIM] for p in pre], axis=1)
    pre_i = jnp.concatenate([p[:, MXU_DIM:] for p in pre], axis=1)
    r = jax.nn.sigmoid(pre_a + b_a_ref[...])
    ig = jax.nn.sigmoid(pre_i + b_i_ref[...])
    lam = lam_ref[...]
    softplus_neg = jnp.maximum(-lam, 0.0) + jnp.log(1.0 + jnp.exp(-jnp.abs(lam)))
    log_a = (-LRU_C) * r * softplus_neg
    a = jnp.exp(log_a)
    mult = jnp.sqrt(jnp.maximum(1.0 - a * a, 0.0))
    mult = jnp.where(grow == 0, 1.0, mult)
    bx = mult * (ig * uc)

    row8 = lax.broadcasted_iota(jnp.int32, (tm, D_RNN), 0) & (SUBLANES - 1)
    d = 1
    while d < SUBLANES:
        a_sh = pltpu.roll(a, d, axis=0)
        b_sh = pltpu.roll(bx, d, axis=0)
        m = row8 >= d
        bx = jnp.where(m, a * b_sh + bx, bx)
        a = jnp.where(m, a * a_sh, a)
        d *= 2
    a_s[...] = a
    b_s[...] = bx

    def carry_body(g, carry):
        i0 = pl.multiple_of(g * SUBLANES, SUBLANES)
        hv = a_s[pl.ds(i0, SUBLANES), :] * carry + b_s[pl.ds(i0, SUBLANES), :]
        b_s[pl.ds(i0, SUBLANES), :] = hv
        return jnp.broadcast_to(hv[SUBLANES - 1:SUBLANES, :], (SUBLANES, D_RNN))

    carry = lax.fori_loop(0, tm // SUBLANES, carry_body,
                          jnp.broadcast_to(hcar[...], (SUBLANES, D_RNN)), unroll=4)
    hcar[...] = carry[0:1, :]
    y_rnn = b_s[...] * jax.nn.gelu(g_rnn)
    mixed_s[:, :D_RNN] = y_rnn.astype(BF16)

    e = upool_buf[...]
    s2 = e + pltpu.roll(e, 1, axis=0)
    s4 = s2 + pltpu.roll(s2, 2, axis=0)
    s8 = s4 + pltpu.roll(s4, 4, axis=0)
    s16 = s8 + pltpu.roll(s8, 8, axis=0)
    grp = lax.broadcasted_iota(jnp.int32, (tm, D_POOL), 1) // POOL_GROUP_DIM
    wsum = jnp.where(grp == 0, s2[HIST:], jnp.where(grp == 1, s4[HIST:], jnp.where(grp == 2, s8[HIST:], s16[HIST:])))
    win = jnp.where(grp == 0, 2.0, jnp.where(grp == 1, 4.0, jnp.where(grp == 2, 8.0, 16.0)))
    pos = (grow + 1).astype(F32)
    dpool = wsum / jnp.minimum(pos, win) - e[HIST:]
    y_pool = _dot(dpool.astype(BF16), w_pool_ref[...]) * pool_scale_ref[...]
    mixed_s[:, D_RNN:D_RNN + D_POOL] = y_pool.astype(BF16)

    sc = _dot(q_mem.astype(BF16), kbd_ref[0]) * (XATTN_HEAD_DIM ** -0.5)
    ps = []
    for hh in range(XATTN_HEADS):
        sh = sc[:, hh * N_MEM:(hh + 1) * N_MEM]
        ph = jnp.exp(sh - jnp.max(sh, axis=-1, keepdims=True))
        ps.append((ph * (1.0 / jnp.sum(ph, axis=-1, keepdims=True))).astype(BF16))
    y_mem = _dot(jnp.concatenate(ps, axis=1), vbd_ref[0])
    mixed_s[:, D_RNN + D_POOL:] = y_mem.astype(BF16)

    x1 = x + _dot(mixed_s[...], w_out_ref[...])
    x1_ref[0] = x1

    urnn_buf[0:HIST, :] = urnn_buf[tm:tm + HIST, :]
    upool_buf[0:HIST, :] = upool_buf[tm:tm + HIST, :]

    hn = _rms(x1, g_ffn_ref[...])
    hn_ref[...] = hn
    logits = lax.dot_general(w_rt_ref[...], hn.astype(BF16), (((1,), (1,)), ((), ())),
                             preferred_element_type=F32) + b_r_ref[...]
    eid = lax.broadcasted_iota(jnp.int32, (N_EXPERTS, tm), 0)
    vals, idxs = [], []
    for _ in range(TOP_K):
        mx = jnp.max(logits, axis=0, keepdims=True)
        ix = jnp.min(jnp.where(logits == mx, eid, N_EXPERTS), axis=0, keepdims=True)
        vals.append(mx)
        idxs.append(ix)
        logits = jnp.where(eid == ix, -jnp.inf, logits)
    ex = [jnp.exp(v - vals[0]) for v in vals]
    den = ex[0] + ex[1] + ex[2] + ex[3]
    idx_ref[0] = jnp.concatenate(idxs, axis=0)
    gate_ref[0] = jnp.concatenate([e_ / den for e_ in ex], axis=0)


def _mix_call(x, g_mix, w_in, conv_w, conv_b, wg, b_a, b_i, lam, w_pool, pool_scale, kbd, vbd, w_out,
              g_ffn, w_rt, b_r):
    b, s, d = x.shape
    tm = MIX_ROWS
    ns = s // tm
    const2 = lambda shape: pl.BlockSpec(shape, lambda bi, si: (0, 0))
    const3 = lambda shape: pl.BlockSpec(shape, lambda bi, si: (0, 0, 0))
    return pl.pallas_call(
        _mix_kernel,
        grid=(b, ns),
        in_specs=[
            pl.BlockSpec((1, tm, d), lambda bi, si: (bi, si, 0)),
            const2((1, d)),
            const2((d, D_IN)),
            const2((CONV_WIDTH, D_RNN)),
            const2((1, D_RNN)),
            const3((D_RNN // MXU_DIM, MXU_DIM, 2 * MXU_DIM)),
            const2((1, D_RNN)),
            const2((1, D_RNN)),
            const2((1, D_RNN)),
            const2((D_POOL, D_POOL)),
            const2((1, D_POOL)),
            pl.BlockSpec((1, D_XATTN, XATTN_HEADS * N_MEM), lambda bi, si: (bi, 0, 0)),
            pl.BlockSpec((1, XATTN_HEADS * N_MEM, D_XATTN), lambda bi, si: (bi, 0, 0)),
            const2((d, d)),
            const2((1, d)),
            const2((N_EXPERTS, d)),
            const2((N_EXPERTS, 1)),
        ],
        out_specs=[
            pl.BlockSpec((1, tm, d), lambda bi, si: (bi, si, 0)),
            pl.BlockSpec((tm, d), lambda bi, si: (bi * ns + si, 0)),
            pl.BlockSpec((1, TOP_K, tm), lambda bi, si: (bi, 0, si)),
            pl.BlockSpec((1, TOP_K, tm), lambda bi, si: (bi, 0, si)),
        ],
        out_shape=[
            jax.ShapeDtypeStruct((b, s, d), F32),
            jax.ShapeDtypeStruct((b * s, d), F32),
            jax.ShapeDtypeStruct((b, TOP_K, s), jnp.int32),
            jax.ShapeDtypeStruct((b, TOP_K, s), F32),
        ],
        scratch_shapes=[
            pltpu.VMEM((HIST + tm, D_RNN), F32),
            pltpu.VMEM((HIST + tm, D_POOL), F32),
            pltpu.VMEM((1, D_RNN), F32),
            pltpu.VMEM((tm, D_RNN), F32),
            pltpu.VMEM((tm, D_RNN), F32),
            pltpu.VMEM((tm, d), BF16),
        ],
        compiler_params=pltpu.CompilerParams(
            dimension_semantics=("arbitrary", "arbitrary"), vmem_limit_bytes=VMEM_LIMIT_BYTES),
        name="mixer_router",
    )(x, g_mix, w_in, conv_w, conv_b, wg, b_a, b_i, lam, w_pool, pool_scale, kbd, vbd, w_out, g_ffn, w_rt, b_r)


def _dispatch_kernel(dest_ref, hn_ref, xs_in_ref, xs_ref, sem):
    del xs_in_ref
    tt = ROUTE_ROWS

    def row_copy(k, r):
        return pltpu.make_async_copy(hn_ref.at[pl.ds(r, 1), :], xs_ref.at[pl.ds(dest_ref[k, r], 1), :], sem)

    def issue(r, c):
        for k in range(TOP_K):
            row_copy(k, r).start()
        return c

    lax.fori_loop(0, tt, issue, 0)

    def drain(r, c):
        for k in range(TOP_K):
            row_copy(k, r).wait()
        return c

    lax.fori_loop(0, tt, drain, 0)


def _dispatch_call(dest, hn, xs_init):
    t, d = hn.shape
    tt = ROUTE_ROWS
    return pl.pallas_call(
        _dispatch_kernel,
        grid=(t // tt,),
        in_specs=[
            pl.BlockSpec((TOP_K, tt), lambda i: (0, i), memory_space=pltpu.SMEM),
            pl.BlockSpec((tt, d), lambda i: (i, 0)),
            pl.BlockSpec(memory_space=pl.ANY),
        ],
        out_specs=pl.BlockSpec(memory_space=pl.ANY),
        out_shape=jax.ShapeDtypeStruct(xs_init.shape, xs_init.dtype),
        scratch_shapes=[pltpu.SemaphoreType.DMA],
        input_output_aliases={2: 0},
        compiler_params=pltpu.CompilerParams(dimension_semantics=("arbitrary",)),
        name="dispatch",
    )(dest, hn, xs_init)


def _expert_kernel(be_ref, bv_ref, x_ref, wgu_ref, bgu_ref, wd_ref, bd_ref, y_ref, wgu_bf, wd_bf, h_s):
    i = pl.program_id(0)
    e = be_ref[i]
    e_prev = be_ref[jnp.maximum(i - 1, 0)]
    valid = bv_ref[i]

    @pl.when(jnp.logical_and(valid > 0, jnp.logical_or(i == 0, e != e_prev)))
    def _():
        wgu_bf[...] = wgu_ref[0].astype(BF16)
        wd_bf[...] = wd_ref[0].astype(BF16)

    @pl.when(valid > 0)
    def _():
        xb = x_ref[...].astype(BF16)
        for c in range(D_FF // (2 * MXU_DIM)):
            lo = c * 2 * MXU_DIM
            hi = lo + 2 * MXU_DIM
            g = _dot(xb, wgu_bf[:, lo:hi]) + bgu_ref[0, :, lo:hi]
            up = _dot(xb, wgu_bf[:, D_FF + lo:D_FF + hi]) + bgu_ref[0, :, D_FF + lo:D_FF + hi]
            g = jnp.minimum(g, SWIGLU_LIMIT)
            up = jnp.clip(up, -SWIGLU_LIMIT, SWIGLU_LIMIT)
            glu = g * jax.nn.sigmoid(g * SWIGLU_ALPHA)
            h_s[:, lo:hi] = ((up + 1.0) * glu).astype(BF16)
        y_ref[...] = _dot(h_s[...], wd_bf[...]) + bd_ref[0]

    @pl.when(valid == 0)
    def _():
        y_ref[...] = jnp.zeros(y_ref.shape, F32)


def _expert_call(block_e, block_valid, n_used, xs, w_gu, b_gu, w_down, b_down):
    n_pad, d = xs.shape
    bm = EXPERT_ROWS
    nb = n_pad // bm
    del n_used
    grid_spec = pltpu.PrefetchScalarGridSpec(
        num_scalar_prefetch=2,
        grid=(nb,),
        in_specs=[
            pl.BlockSpec((bm, d), lambda i, be, bv: (i, 0)),
            pl.BlockSpec((1, d, 2 * D_FF), lambda i, be, bv: (be[i], 0, 0)),
            pl.BlockSpec((1, 1, 2 * D_FF), lambda i, be, bv: (be[i], 0, 0)),
            pl.BlockSpec((1, D_FF, d), lambda i, be, bv: (be[i], 0, 0)),
            pl.BlockSpec((1, 1, d), lambda i, be, bv: (be[i], 0, 0)),
        ],
        out_specs=pl.BlockSpec((bm, d), lambda i, be, bv: (i, 0)),
        scratch_shapes=[
            pltpu.VMEM((d, 2 * D_FF), BF16),
            pltpu.VMEM((D_FF, d), BF16),
            pltpu.VMEM((bm, D_FF), BF16),
        ],
    )
    return pl.pallas_call(
        _expert_kernel,
        grid_spec=grid_spec,
        out_shape=jax.ShapeDtypeStruct((n_pad, d), F32),
        compiler_params=pltpu.CompilerParams(
            dimension_semantics=("arbitrary",), vmem_limit_bytes=VMEM_LIMIT_BYTES),
        name="expert_ffn",
    )(block_e, block_valid, xs, w_gu, b_gu, w_down, b_down)


def _combine_kernel(dest_ref, x1_ref, gate_ref, ys_ref, g_ref, out_ref, ybuf, sem):
    tt = ROUTE_ROWS

    def row_copy(k, r):
        return pltpu.make_async_copy(ys_ref.at[pl.ds(dest_ref[k, r], 1), :], ybuf.at[k, pl.ds(r, 1), :], sem)

    def issue(r, c):
        for k in range(TOP_K):
            row_copy(k, r).start()
        return c

    lax.fori_loop(0, tt, issue, 0)

    def drain(r, c):
        for k in range(TOP_K):
            row_copy(k, r).wait()
        return c

    lax.fori_loop(0, tt, drain, 0)

    acc = x1_ref[...]
    gate = gate_ref[...]
    for k in range(TOP_K):
        acc = acc + gate[:, k:k + 1] * ybuf[k]
    out_ref[...] = _rms(acc, g_ref[...])


def _combine_call(dest, x1, gate_t, ys, g_final):
    t, d = x1.shape
    tt = ROUTE_ROWS
    return pl.pallas_call(
        _combine_kernel,
        grid=(t // tt,),
        in_specs=[
            pl.BlockSpec((TOP_K, tt), lambda i: (0, i), memory_space=pltpu.SMEM),
            pl.BlockSpec((tt, d), lambda i: (i, 0)),
            pl.BlockSpec((tt, TOP_K), lambda i: (i, 0)),
            pl.BlockSpec(memory_space=pl.ANY),
            pl.BlockSpec((1, d), lambda i: (0, 0)),
        ],
        out_specs=pl.BlockSpec((tt, d), lambda i: (i, 0)),
        out_shape=jax.ShapeDtypeStruct((t, d), F32),
        scratch_shapes=[pltpu.VMEM((TOP_K, tt, d), F32), pltpu.SemaphoreType.DMA],
        compiler_params=pltpu.CompilerParams(dimension_semantics=("arbitrary",)),
        name="combine_norm",
    )(dest, x1, gate_t, ys, g_final)


def _block_diag(w):
    hh, n, _ = w.shape
    eye = jnp.eye(hh, dtype=w.dtype)
    return jnp.einsum("hij,hg->higj", w, eye).reshape(hh * n, hh * n)


def _route(idx, t):
    bm = EXPERT_ROWS
    tk = t * TOP_K
    n_blocks = (tk + N_EXPERTS * (bm - 1) + bm - 1) // bm
    eids = jnp.arange(N_EXPERTS, dtype=jnp.int32)
    sel = (idx[:, :, None] == eids[None, None, :])
    sel_t = jnp.sum(sel.astype(jnp.int32), axis=0)
    rank = jnp.cumsum(sel_t, axis=0) - sel_t
    counts = jnp.sum(sel_t, axis=0)
    padded = ((counts + bm - 1) // bm) * bm
    pad_end = jnp.cumsum(padded)
    pad_start = pad_end - padded
    dest = jnp.sum(jnp.where(sel, (pad_start[None, :] + rank)[None, :, :], 0), axis=-1).astype(jnp.int32)
    n_used = pad_end[-1] // bm
    bstart = jnp.arange(n_blocks, dtype=jnp.int32) * bm
    bi = jnp.minimum(jnp.arange(n_blocks, dtype=jnp.int32), n_used - 1)
    be = jnp.minimum(jnp.searchsorted(pad_end, bi * bm, side="right"), N_EXPERTS - 1).astype(jnp.int32)
    bv = jnp.clip(counts[be] - (bi * bm - pad_start[be]), 0, bm)
    bv = jnp.where(bstart < pad_end[-1], bv, 0).astype(jnp.int32)
    return dest, be, bv, n_used.astype(jnp.int32), n_blocks


def kernel(x, mem, norm_mix_g, w_in, conv_w, conv_b, w_rg_a, b_rg_a, w_rg_i, b_rg_i, lru_lambda, w_pool,
           pool_scale, mem_norm_g, w_mem_kv, w_out, norm_ffn_g, w_router, b_router, w_gu, b_gu, w_down,
           b_down, final_norm_g):
    b, s, d = x.shape
    t = b * s
    l = 0
    row = lambda v: v.reshape(1, -1)

    kv = _kv_call(mem, row(mem_norm_g[l]), w_mem_kv[l].astype(BF16))
    kh = kv[..., :D_XATTN].reshape(b, N_MEM, XATTN_HEADS, XATTN_HEAD_DIM)
    vh = kv[..., D_XATTN:].reshape(b, N_MEM, XATTN_HEADS, XATTN_HEAD_DIM)
    eye_h = jnp.eye(XATTN_HEADS, dtype=F32)
    kbd = jnp.einsum("bmhd,hg->bhdgm", kh, eye_h).reshape(b, D_XATTN, XATTN_HEADS * N_MEM).astype(BF16)
    vbd = jnp.einsum("bmhd,hg->bgmhd", vh, eye_h).reshape(b, XATTN_HEADS * N_MEM, D_XATTN).astype(BF16)

    heads_per = MXU_DIM // RNN_HEAD_DIM
    wg = jnp.stack([
        jnp.concatenate([_block_diag(w_rg_a[l, c * heads_per:(c + 1) * heads_per]),
                         _block_diag(w_rg_i[l, c * heads_per:(c + 1) * heads_per])], axis=1)
        for c in range(D_RNN // MXU_DIM)]).astype(BF16)

    x1, hn, idx, gate = _mix_call(
        x, row(norm_mix_g[l]), w_in[l].astype(BF16), conv_w[l], row(conv_b[l]), wg, row(b_rg_a[l]),
        row(b_rg_i[l]), row(lru_lambda[l]), _block_diag(w_pool[l]).astype(BF16), row(pool_scale[l]), kbd, vbd,
        w_out[l].astype(BF16), row(norm_ffn_g[l]), w_router[l].T.astype(BF16), b_router[l].reshape(-1, 1))

    idx = jnp.transpose(idx, (1, 0, 2)).reshape(TOP_K, t)
    gate_t = jnp.transpose(gate, (0, 2, 1)).reshape(t, TOP_K)
    dest, be, bv, n_used, n_blocks = _route(idx, t)

    xs = _dispatch_call(dest, hn, jnp.zeros((n_blocks * EXPERT_ROWS, d), F32))
    ys = _expert_call(be, bv, n_used, xs, w_gu[l], b_gu[l].reshape(N_EXPERTS, 1, -1), w_down[l],
                      b_down[l].reshape(N_EXPERTS, 1, -1))
    out = _combine_call(dest, x1.reshape(t, d), gate_t, ys, row(final_norm_g))
    return out.reshape(b, s, d)
```

```python
import functools
import math

import jax
import jax.numpy as jnp
from jax import lax
from jax.experimental import pallas as pl
from jax.experimental.pallas import tpu as pltpu

D_MODEL = 1024
N_MEM = 256
D_RNN = 512
RNN_HEADS = 8
RNN_HEAD_DIM = D_RNN // RNN_HEADS
CONV_WIDTH = 4
LRU_C = 8.0
D_POOL = 256
POOL_WINDOWS = (2, 4, 8, 16)
POOL_GROUP_DIM = D_POOL // len(POOL_WINDOWS)
MAX_WINDOW = max(POOL_WINDOWS)
D_XATTN = 256
XATTN_HEADS = 4
XATTN_HEAD_DIM = D_XATTN // XATTN_HEADS
D_IN = 2 * D_RNN + D_POOL + D_XATTN
N_EXPERTS = 32
TOP_K = 4
D_FF = D_MODEL
SWIGLU_ALPHA = 1.702
SWIGLU_LIMIT = 7.0
RMS_EPS = 1e-6

SUBLANES = 8
MXU_DIM = 256
VMEM_LIMIT_BYTES = 56 * 1024 * 1024

MIX_ROWS = 512
HIST = 16
EXPERT_ROWS = 512
ROUTE_ROWS = 256
ROUTE_CHUNK = 1024

BF16 = jnp.bfloat16
F32 = jnp.float32


def _rms(xf, g):
    return xf * lax.rsqrt(jnp.mean(xf * xf, axis=-1, keepdims=True) + RMS_EPS) * g


def _dot(a, b):
    return jnp.dot(a, b, preferred_element_type=F32)


def _kv_kernel(mem_ref, g_ref, w_ref, kv_ref):
    mn = _rms(mem_ref[0], g_ref[...])
    kv_ref[0] = _dot(mn.astype(BF16), w_ref[...])


def _kv_call(mem, g, w_kv_bf):
    b = mem.shape[0]
    return pl.pallas_call(
        _kv_kernel,
        grid=(b,),
        in_specs=[
            pl.BlockSpec((1, N_MEM, D_MODEL), lambda i: (i, 0, 0)),
            pl.BlockSpec((1, D_MODEL), lambda i: (0, 0)),
            pl.BlockSpec((D_MODEL, 2 * D_XATTN), lambda i: (0, 0)),
        ],
        out_specs=pl.BlockSpec((1, N_MEM, 2 * D_XATTN), lambda i: (i, 0, 0)),
        out_shape=jax.ShapeDtypeStruct((b, N_MEM, 2 * D_XATTN), F32),
        name="kv_proj",
    )(mem, g, w_kv_bf)


def _mix_kernel(x_ref, g_mix_ref, w_in_ref, conv_w_ref, conv_b_ref, wg_ref, b_a_ref, b_i_ref, lam_ref,
                w_pool_ref, pool_scale_ref, kbd_ref, vbd_ref, w_out_ref, g_ffn_ref, w_rt_ref, b_r_ref,
                x1_ref, hn_ref, idx_ref, gate_ref,
                urnn_buf, upool_buf, hcar, a_s, b_s, mixed_s):
    tm = MIX_ROWS
    s = pl.program_id(1)

    @pl.when(s == 0)
    def _():
        urnn_buf[0:HIST, :] = jnp.zeros((HIST, D_RNN), F32)
        upool_buf[0:HIST, :] = jnp.zeros((HIST, D_POOL), F32)
        hcar[...] = jnp.zeros((1, D_RNN), F32)

    x = x_ref[0]
    h = _rms(x, g_mix_ref[...])
    proj = _dot(h.astype(BF16), w_in_ref[...])
    g_rnn = proj[:, D_RNN:2 * D_RNN]
    q_mem = proj[:, 2 * D_RNN + D_POOL:]
    urnn_buf[HIST:, :] = proj[:, :D_RNN]
    upool_buf[HIST:, :] = proj[:, 2 * D_RNN:2 * D_RNN + D_POOL]

    row = lax.broadcasted_iota(jnp.int32, (tm, 1), 0)
    grow = row + s * tm

    uc = conv_b_ref[...] + conv_w_ref[CONV_WIDTH - 1:CONV_WIDTH, :] * urnn_buf[pl.ds(HIST, tm), :]
    for k in range(CONV_WIDTH - 1):
        off = HIST - (CONV_WIDTH - 1) + k
        uc = uc + conv_w_ref[k:k + 1, :] * urnn_buf[pl.ds(off, tm), :]

    ucb = uc.astype(BF16)
    pre = [_dot(ucb[:, c * MXU_DIM:(c + 1) * MXU_DIM], wg_ref[c]) for c in range(D_RNN // MXU_DIM)]
    pre_a = jnp.concatenate([p[:, :MXU_DIM] for p in pre], axis=1)
    pre_i = jnp.concatenate([p[:, MXU_DIM:] for p in pre], axis=1)
    r = jax.nn.sigmoid(pre_a + b_a_ref[...])
    ig = jax.nn.sigmoid(pre_i + b_i_ref[...])
    lam = lam_ref[...]
    softplus_neg = jnp.maximum(-lam, 0.0) + jnp.log(1.0 + jnp.exp(-jnp.abs(lam)))
    log_a = (-LRU_C) * r * softplus_neg
    a = jnp.exp(log_a)
    mult = jnp.sqrt(jnp.maximum(1.0 - a * a, 0.0))
    mult = jnp.where(grow == 0, 1.0, mult)
    bx = mult * (ig * uc)

    row8 = lax.broadcasted_iota(jnp.int32, (tm, D_RNN), 0) & (SUBLANES - 1)
    d = 1
    while d < SUBLANES:
        a_sh = pltpu.roll(a, d, axis=0)
        b_sh = pltpu.roll(bx, d, axis=0)
        m = row8 >= d
        bx = jnp.where(m, a * b_sh + bx, bx)
        a = jnp.where(m, a * a_sh, a)
        d *= 2
    a_s[...] = a
    b_s[...] = bx

    def carry_body(g, carry):
        i0 = pl.multiple_of(g * SUBLANES, SUBLANES)
        hv = a_s[pl.ds(i0, SUBLANES), :] * carry + b_s[pl.ds(i0, SUBLANES), :]
        b_s[pl.ds(i0, SUBLANES), :] = hv
        return jnp.broadcast_to(hv[SUBLANES - 1:SUBLANES, :], (SUBLANES, D_RNN))

    carry = lax.fori_loop(0, tm // SUBLANES, carry_body,
                          jnp.broadcast_to(hcar[...], (SUBLANES, D_RNN)), unroll=4)
    hcar[...] = carry[0:1, :]
    y_rnn = b_s[...] * jax.nn.gelu(g_rnn)
    mixed_s[:, :D_RNN] = y_rnn.astype(BF16)

    e = upool_buf[...]
    s2 = e + pltpu.roll(e, 1, axis=0)
    s4 = s2 + pltpu.roll(s2, 2, axis=0)
    s8 = s4 + pltpu.roll(s4, 4, axis=0)
    s16 = s8 + pltpu.roll(s8, 8, axis=0)
    grp = lax.broadcasted_iota(jnp.int32, (tm, D_POOL), 1) // POOL_GROUP_DIM
    wsum = jnp.where(grp == 0, s2[HIST:], jnp.where(grp == 1, s4[HIST:], jnp.where(grp == 2, s8[HIST:], s16[HIST:])))
    win = jnp.where(grp == 0, 2.0, jnp.where(grp == 1, 4.0, jnp.where(grp == 2, 8.0, 16.0)))
    pos = (grow + 1).astype(F32)
    dpool = wsum / jnp.minimum(pos, win) - e[HIST:]
    y_pool = _dot(dpool.astype(BF16), w_pool_ref[...]) * pool_scale_ref[...]
    mixed_s[:, D_RNN:D_RNN + D_POOL] = y_pool.astype(BF16)

    sc = _dot(q_mem.astype(BF16), kbd_ref[0]) * (XATTN_HEAD_DIM ** -0.5)
    ps = []
    for hh in range(XATTN_HEADS):
        sh = sc[:, hh * N_MEM:(hh + 1) * N_MEM]
        ph = jnp.exp(sh - jnp.max(sh, axis=-1, keepdims=True))
        ps.append((ph * (1.0 / jnp.sum(ph, axis=-1, keepdims=True))).astype(BF16))
    y_mem = _dot(jnp.concatenate(ps, axis=1), vbd_ref[0])
    mixed_s[:, D_RNN + D_POOL:] = y_mem.astype(BF16)

    x1 = x + _dot(mixed_s[...], w_out_ref[...])
    x1_ref[0] = x1

    urnn_buf[0:HIST, :] = urnn_buf[tm:tm + HIST, :]
    upool_buf[0:HIST, :] = upool_buf[tm:tm + HIST, :]

    hn = _rms(x1, g_ffn_ref[...])
    hn_ref[...] = hn
    logits = lax.dot_general(w_rt_ref[...], hn.astype(BF16), (((1,), (1,)), ((), ())),
                             preferred_element_type=F32) + b_r_ref[...]
    eid = lax.broadcasted_iota(jnp.int32, (N_EXPERTS, tm), 0)
    vals, idxs = [], []
    for _ in range(TOP_K):
        mx = jnp.max(logits, axis=0, keepdims=True)
        ix = jnp.min(jnp.where(logits == mx, eid, N_EXPERTS), axis=0, keepdims=True)
        vals.append(mx)
        idxs.append(ix)
        logits = jnp.where(eid == ix, -jnp.inf, logits)
    ex = [jnp.exp(v - vals[0]) for v in vals]
    den = ex[0] + ex[1] + ex[2] + ex[3]
    idx_ref[...] = jnp.concatenate(idxs, axis=0)
    gate_ref[...] = jnp.concatenate([e_ / den for e_ in ex], axis=0)


def _mix_call(x, g_mix, w_in, conv_w, conv_b, wg, b_a, b_i, lam, w_pool, pool_scale, kbd, vbd, w_out,
              g_ffn, w_rt, b_r):
    b, s, d = x.shape
    tm = MIX_ROWS
    ns = s // tm
    const2 = lambda shape: pl.BlockSpec(shape, lambda bi, si: (0, 0))
    const3 = lambda shape: pl.BlockSpec(shape, lambda bi, si: (0, 0, 0))
    return pl.pallas_call(
        _mix_kernel,
        grid=(b, ns),
        in_specs=[
            pl.BlockSpec((1, tm, d), lambda bi, si: (bi, si, 0)),
            const2((1, d)),
            const2((d, D_IN)),
            const2((CONV_WIDTH, D_RNN)),
            const2((1, D_RNN)),
            const3((D_RNN // MXU_DIM, MXU_DIM, 2 * MXU_DIM)),
            const2((1, D_RNN)),
            const2((1, D_RNN)),
            const2((1, D_RNN)),
            const2((D_POOL, D_POOL)),
            const2((1, D_POOL)),
            pl.BlockSpec((1, D_XATTN, XATTN_HEADS * N_MEM), lambda bi, si: (bi, 0, 0)),
            pl.BlockSpec((1, XATTN_HEADS * N_MEM, D_XATTN), lambda bi, si: (bi, 0, 0)),
            const2((d, d)),
            const2((1, d)),
            const2((N_EXPERTS, d)),
            const2((N_EXPERTS, 1)),
        ],
        out_specs=[
            pl.BlockSpec((1, tm, d), lambda bi, si: (bi, si, 0)),
            pl.BlockSpec((tm, d), lambda bi, si: (bi * ns + si, 0)),
            pl.BlockSpec((TOP_K, tm), lambda bi, si: (0, bi * ns + si)),
            pl.BlockSpec((TOP_K, tm), lambda bi, si: (0, bi * ns + si)),
        ],
        out_shape=[
            jax.ShapeDtypeStruct((b, s, d), F32),
            jax.ShapeDtypeStruct((b * s, d), F32),
            jax.ShapeDtypeStruct((TOP_K, b * s), jnp.int32),
            jax.ShapeDtypeStruct((TOP_K, b * s), F32),
        ],
        scratch_shapes=[
            pltpu.VMEM((HIST + tm, D_RNN), F32),
            pltpu.VMEM((HIST + tm, D_POOL), F32),
            pltpu.VMEM((1, D_RNN), F32),
            pltpu.VMEM((tm, D_RNN), F32),
            pltpu.VMEM((tm, D_RNN), F32),
            pltpu.VMEM((tm, d), BF16),
        ],
        compiler_params=pltpu.CompilerParams(
            dimension_semantics=("arbitrary", "arbitrary"), vmem_limit_bytes=VMEM_LIMIT_BYTES),
        name="mixer_router",
    )(x, g_mix, w_in, conv_w, conv_b, wg, b_a, b_i, lam, w_pool, pool_scale, kbd, vbd, w_out, g_ffn, w_rt, b_r)


def _num_blocks(t):
    bm = EXPERT_ROWS
    return (t * TOP_K + N_EXPERTS * (bm - 1) + bm - 1) // bm


def _route_kernel(idx_ref, dest_ref, meta_ref, rank_s, carry_s, start_s, tri_s):
    p = pl.program_id(0)
    c = pl.program_id(1)
    nc = pl.num_programs(1)
    ch = ROUTE_CHUNK
    bm = float(EXPERT_ROWS)
    nbl = meta_ref.shape[1]
    eid = lax.broadcasted_iota(jnp.int32, (N_EXPERTS, ch), 0)
    idxc = idx_ref[...]

    @pl.when(jnp.logical_and(p == 0, c == 0))
    def _():
        carry_s[...] = jnp.zeros(carry_s.shape, F32)
        tri_s[...] = (lax.broadcasted_iota(jnp.int32, (ch, ch), 0)
                      < lax.broadcasted_iota(jnp.int32, (ch, ch), 1)).astype(BF16)

    @pl.when(p == 0)
    def _():
        sel = jnp.zeros((N_EXPERTS, ch), F32)
        for k in range(TOP_K):
            sel = sel + (idxc[k:k + 1, :] == eid).astype(F32)
        rank_s[c] = _dot(sel.astype(BF16), tri_s[...]) + carry_s[:, 0:1]
        carry_s[...] = carry_s[...] + jnp.sum(sel, axis=1, keepdims=True)

    @pl.when(jnp.logical_and(p == 0, c == nc - 1))
    def _():
        counts = carry_s[...]
        padded = jnp.floor((counts + (bm - 1.0)) * (1.0 / bm)) * bm
        e128 = lax.broadcasted_iota(jnp.int32, counts.shape, 0)
        pad_end = padded
        sh = 1
        while sh < N_EXPERTS:
            pad_end = pad_end + jnp.where(e128 >= sh, pltpu.roll(pad_end, sh, axis=0), 0.0)
            sh *= 2
        pad_start = pad_end - padded
        start_s[...] = pad_start
        total = pad_end[N_EXPERTS - 1:N_EXPERTS, 0:1]
        lane = lax.broadcasted_iota(jnp.int32, (1, nbl), 1)
        lane_f = lane.astype(F32)
        bclamp = jnp.minimum(lane_f, total * (1.0 / bm) - 1.0)
        bstart = bclamp * bm
        pe, ps, cn, pd = pad_end[:, 0:1], pad_start[:, 0:1], counts[:, 0:1], padded[:, 0:1]
        be = jnp.minimum(jnp.sum((pe <= bstart).astype(F32), axis=0, keepdims=True), N_EXPERTS - 1.0)
        esub = lax.broadcasted_iota(jnp.int32, (N_EXPERTS, nbl), 0)
        onehot = esub.astype(F32) == be
        bv = jnp.sum(jnp.where(onehot, cn - (bstart - ps), 0.0), axis=0, keepdims=True)
        bv = jnp.where(lane_f * bm < total, jnp.clip(bv, 0.0, bm), 0.0)
        own = esub == lane
        fill_start = jnp.sum(jnp.where(own, ps + cn, 0.0), axis=0, keepdims=True)
        fill_n = jnp.sum(jnp.where(own, pd - cn, 0.0), axis=0, keepdims=True)
        n_used = jnp.broadcast_to(total * (1.0 / bm), (1, nbl))
        zero = jnp.zeros((1, nbl), F32)
        meta_ref[...] = jnp.concatenate([be, bv, bclamp, fill_start, fill_n, n_used, zero, zero],
                                        axis=0).astype(jnp.int32)

    @pl.when(p == 1)
    def _():
        base = start_s[:, 0:1] + rank_s[c]
        rows = [jnp.sum(jnp.where(idxc[k:k + 1, :] == eid, base, 0.0), axis=0, keepdims=True)
                for k in range(TOP_K)]
        dest_ref[...] = jnp.concatenate(rows, axis=0).astype(jnp.int32)


def _route_call(idx):
    k, t = idx.shape
    ch = ROUTE_CHUNK
    nc = t // ch
    nbl = -(-_num_blocks(t) // 128) * 128
    return pl.pallas_call(
        _route_kernel,
        grid=(2, nc),
        in_specs=[pl.BlockSpec((k, ch), lambda p, c: (0, c))],
        out_specs=[
            pl.BlockSpec((k, ch), lambda p, c: (0, c * p)),
            pl.BlockSpec((8, nbl), lambda p, c: (0, 0)),
        ],
        out_shape=[
            jax.ShapeDtypeStruct((k, t), jnp.int32),
            jax.ShapeDtypeStruct((8, nbl), jnp.int32),
        ],
        scratch_shapes=[
            pltpu.VMEM((nc, N_EXPERTS, ch), F32),
            pltpu.VMEM((N_EXPERTS, 128), F32),
            pltpu.VMEM((N_EXPERTS, 128), F32),
            pltpu.VMEM((ch, ch), BF16),
        ],
        compiler_params=pltpu.CompilerParams(dimension_semantics=("arbitrary", "arbitrary")),
        name="route_tables",
    )(idx)


def _dispatch_kernel(fill_start_ref, fill_n_ref, n_used_ref, dest_ref, hn_ref, xs_ref, zero_s, sem, zsem):
    tt = ROUTE_ROWS
    bm = EXPERT_ROWS
    n_blocks = xs_ref.shape[0] // bm

    @pl.when(pl.program_id(0) == 0)
    def _():
        zero_s[...] = jnp.zeros(zero_s.shape, F32)

        def go(cp, wait):
            if wait:
                cp.wait()
            else:
                cp.start()

        def fill(e, wait):
            n = fill_n_ref[e]
            st = fill_start_ref[e]
            head = n & (SUBLANES - 1)
            for j in range(SUBLANES - 1):
                cp = pltpu.make_async_copy(zero_s.at[pl.ds(0, 1), :], xs_ref.at[pl.ds(st + j, 1), :], zsem)
                pl.when(j < head)(functools.partial(go, cp, wait))
            st = pl.multiple_of(st + head, SUBLANES)
            bit = bm // 2
            while bit >= SUBLANES:
                cp = pltpu.make_async_copy(zero_s.at[pl.ds(0, bit), :], xs_ref.at[pl.ds(st, bit), :], zsem)
                pl.when((n & bit) != 0)(functools.partial(go, cp, wait))
                st = pl.multiple_of(st + (n & bit), SUBLANES)
                bit //= 2

        def tail(blk, wait):
            cp = pltpu.make_async_copy(zero_s, xs_ref.at[pl.ds(pl.multiple_of(blk * bm, bm), bm), :], zsem)
            go(cp, wait)

        for wait in (False, True):
            lax.fori_loop(0, N_EXPERTS, lambda e, c, wait=wait: (fill(e, wait), c)[1], 0)
            lax.fori_loop(n_used_ref[0], n_blocks, lambda blk, c, wait=wait: (tail(blk, wait), c)[1], 0)

    def row_copy(k, r):
        return pltpu.make_async_copy(hn_ref.at[pl.ds(r, 1), :], xs_ref.at[pl.ds(dest_ref[k, r], 1), :], sem)

    def issue(r, c):
        for k in range(TOP_K):
            row_copy(k, r).start(priority=k % 2)
        return c

    lax.fori_loop(0, tt, issue, 0)

    def drain(r, c):
        for k in range(TOP_K):
            row_copy(k, r).wait()
        return c

    lax.fori_loop(0, tt, drain, 0)


def _dispatch_call(fill_start, fill_n, n_used, dest, hn, n_rows):
    t, d = hn.shape
    tt = ROUTE_ROWS
    grid_spec = pltpu.PrefetchScalarGridSpec(
        num_scalar_prefetch=3,
        grid=(t // tt,),
        in_specs=[
            pl.BlockSpec((TOP_K, tt), lambda i, fs, fn, nu: (0, i), memory_space=pltpu.SMEM),
            pl.BlockSpec((tt, d), lambda i, fs, fn, nu: (i, 0)),
        ],
        out_specs=pl.BlockSpec(memory_space=pl.ANY),
        scratch_shapes=[pltpu.VMEM((EXPERT_ROWS, d), F32), pltpu.SemaphoreType.DMA,
                        pltpu.SemaphoreType.DMA],
    )
    return pl.pallas_call(
        _dispatch_kernel,
        grid_spec=grid_spec,
        out_shape=jax.ShapeDtypeStruct((n_rows, d), F32),
        compiler_params=pltpu.CompilerParams(dimension_semantics=("arbitrary",)),
        name="dispatch",
    )(fill_start, fill_n, n_used, dest, hn)


def _expert_kernel(be_ref, bv_ref, bi_ref, x_ref, wgu_ref, bgu_ref, wd_ref, bd_ref, y_ref, wgu_bf, wd_bf, h_s):
    del bi_ref
    i = pl.program_id(0)
    e = be_ref[i]
    e_prev = be_ref[jnp.maximum(i - 1, 0)]
    valid = bv_ref[i]

    @pl.when(jnp.logical_and(valid > 0, jnp.logical_or(i == 0, e != e_prev)))
    def _():
        wgu_bf[...] = wgu_ref[0].astype(BF16)
        wd_bf[...] = wd_ref[0].astype(BF16)

    @pl.when(valid > 0)
    def _():
        xb = x_ref[...].astype(BF16)
        for c in range(D_FF // (2 * MXU_DIM)):
            lo = c * 2 * MXU_DIM
            hi = lo + 2 * MXU_DIM
            g = _dot(xb, wgu_bf[:, lo:hi]) + bgu_ref[0, :, lo:hi]
            up = _dot(xb, wgu_bf[:, D_FF + lo:D_FF + hi]) + bgu_ref[0, :, D_FF + lo:D_FF + hi]
            g = jnp.minimum(g, SWIGLU_LIMIT)
            up = jnp.clip(up, -SWIGLU_LIMIT, SWIGLU_LIMIT)
            glu = g * jax.nn.sigmoid(g * SWIGLU_ALPHA)
            h_s[:, lo:hi] = ((up + 1.0) * glu).astype(BF16)
        y_ref[...] = _dot(h_s[...], wd_bf[...]) + bd_ref[0]

    @pl.when(valid == 0)
    def _():
        y_ref[...] = jnp.zeros(y_ref.shape, F32)


def _expert_call(block_e, block_valid, block_idx, xs, w_gu, b_gu, w_down, b_down):
    n_pad, d = xs.shape
    bm = EXPERT_ROWS
    nb = n_pad // bm
    grid_spec = pltpu.PrefetchScalarGridSpec(
        num_scalar_prefetch=3,
        grid=(nb,),
        in_specs=[
            pl.BlockSpec((bm, d), lambda i, be, bv, bi: (bi[i], 0)),
            pl.BlockSpec((1, d, 2 * D_FF), lambda i, be, bv, bi: (be[i], 0, 0)),
            pl.BlockSpec((1, 1, 2 * D_FF), lambda i, be, bv, bi: (be[i], 0, 0)),
            pl.BlockSpec((1, D_FF, d), lambda i, be, bv, bi: (be[i], 0, 0)),
            pl.BlockSpec((1, 1, d), lambda i, be, bv, bi: (be[i], 0, 0)),
        ],
        out_specs=pl.BlockSpec((bm, d), lambda i, be, bv, bi: (i, 0)),
        scratch_shapes=[
            pltpu.VMEM((d, 2 * D_FF), BF16),
            pltpu.VMEM((D_FF, d), BF16),
            pltpu.VMEM((bm, D_FF), BF16),
        ],
    )
    return pl.pallas_call(
        _expert_kernel,
        grid_spec=grid_spec,
        out_shape=jax.ShapeDtypeStruct((n_pad, d), F32),
        compiler_params=pltpu.CompilerParams(
            dimension_semantics=("arbitrary",), vmem_limit_bytes=VMEM_LIMIT_BYTES),
        name="expert_ffn",
    )(block_e, block_valid, block_idx, xs, w_gu, b_gu, w_down, b_down)


def _combine_kernel(dest_ref, x1_ref, gate_ref, ys_ref, g_ref, out_ref, ybuf, sem):
    tt = ROUTE_ROWS

    def row_copy(k, r):
        return pltpu.make_async_copy(ys_ref.at[pl.ds(dest_ref[k, r], 1), :], ybuf.at[k, pl.ds(r, 1), :], sem)

    def issue(r, c):
        for k in range(TOP_K):
            row_copy(k, r).start(priority=k % 2)
        return c

    lax.fori_loop(0, tt, issue, 0)

    def drain(r, c):
        for k in range(TOP_K):
            row_copy(k, r).wait()
        return c

    lax.fori_loop(0, tt, drain, 0)

    acc = x1_ref[...]
    gate = gate_ref[...]
    for k in range(TOP_K):
        acc = acc + gate[:, k:k + 1] * ybuf[k]
    out_ref[...] = _rms(acc, g_ref[...])


def _combine_call(dest, x1, gate_t, ys, g_final):
    t, d = x1.shape
    tt = ROUTE_ROWS
    return pl.pallas_call(
        _combine_kernel,
        grid=(t // tt,),
        in_specs=[
            pl.BlockSpec((TOP_K, tt), lambda i: (0, i), memory_space=pltpu.SMEM),
            pl.BlockSpec((tt, d), lambda i: (i, 0)),
            pl.BlockSpec((tt, TOP_K), lambda i: (i, 0)),
            pl.BlockSpec(memory_space=pl.ANY),
            pl.BlockSpec((1, d), lambda i: (0, 0)),
        ],
        out_specs=pl.BlockSpec((tt, d), lambda i: (i, 0)),
        out_shape=jax.ShapeDtypeStruct((t, d), F32),
        scratch_shapes=[pltpu.VMEM((TOP_K, tt, d), F32), pltpu.SemaphoreType.DMA],
        compiler_params=pltpu.CompilerParams(dimension_semantics=("arbitrary",)),
        name="combine_norm",
    )(dest, x1, gate_t, ys, g_final)


def _block_diag(w):
    hh, n, _ = w.shape
    eye = jnp.eye(hh, dtype=w.dtype)
    return jnp.einsum("hij,hg->higj", w, eye).reshape(hh * n, hh * n)


def kernel(x, mem, norm_mix_g, w_in, conv_w, conv_b, w_rg_a, b_rg_a, w_rg_i, b_rg_i, lru_lambda, w_pool,
           pool_scale, mem_norm_g, w_mem_kv, w_out, norm_ffn_g, w_router, b_router, w_gu, b_gu, w_down,
           b_down, final_norm_g):
    b, s, d = x.shape
    t = b * s
    l = 0
    row = lambda v: v.reshape(1, -1)

    kv = _kv_call(mem, row(mem_norm_g[l]), w_mem_kv[l].astype(BF16))
    kh = kv[..., :D_XATTN].reshape(b, N_MEM, XATTN_HEADS, XATTN_HEAD_DIM)
    vh = kv[..., D_XATTN:].reshape(b, N_MEM, XATTN_HEADS, XATTN_HEAD_DIM)
    eye_h = jnp.eye(XATTN_HEADS, dtype=F32)
    kbd = jnp.einsum("bmhd,hg->bhdgm", kh, eye_h).reshape(b, D_XATTN, XATTN_HEADS * N_MEM).astype(BF16)
    vbd = jnp.einsum("bmhd,hg->bgmhd", vh, eye_h).reshape(b, XATTN_HEADS * N_MEM, D_XATTN).astype(BF16)

    heads_per = MXU_DIM // RNN_HEAD_DIM
    wg = jnp.stack([
        jnp.concatenate([_block_diag(w_rg_a[l, c * heads_per:(c + 1) * heads_per]),
                         _block_diag(w_rg_i[l, c * heads_per:(c + 1) * heads_per])], axis=1)
        for c in range(D_RNN // MXU_DIM)]).astype(BF16)

    x1, hn, idx, gate = _mix_call(
        x, row(norm_mix_g[l]), w_in[l].astype(BF16), conv_w[l], row(conv_b[l]), wg, row(b_rg_a[l]),
        row(b_rg_i[l]), row(lru_lambda[l]), _block_diag(w_pool[l]).astype(BF16), row(pool_scale[l]), kbd, vbd,
        w_out[l].astype(BF16), row(norm_ffn_g[l]), w_router[l].T.astype(BF16), b_router[l].reshape(-1, 1))

    n_blocks = _num_blocks(t)
    dest, meta = _route_call(idx)
    be, bv, bi = meta[0, :n_blocks], meta[1, :n_blocks], meta[2, :n_blocks]
    fill_start, fill_n, n_used = meta[3, :N_EXPERTS], meta[4, :N_EXPERTS], meta[5, :1]

    xs = _dispatch_call(fill_start, fill_n, n_used, dest, hn, n_blocks * EXPERT_ROWS)
    ys = _expert_call(be, bv, bi, xs, w_gu[l], b_gu[l].reshape(N_EXPERTS, 1, -1), w_down[l],
                      b_down[l].reshape(N_EXPERTS, 1, -1))
    gate_t = gate.T
    out = _combine_call(dest, x1.reshape(t, d), gate_t, ys, row(final_norm_g))
    return out.reshape(b, s, d)
```

```python
import functools
import math

import jax
import jax.numpy as jnp
from jax import lax
from jax.experimental import pallas as pl
from jax.experimental.pallas import tpu as pltpu
from jax.experimental.pallas import tpu_sc as plsc

D_MODEL = 1024
N_MEM = 256
D_RNN = 512
RNN_HEADS = 8
RNN_HEAD_DIM = D_RNN // RNN_HEADS
CONV_WIDTH = 4
LRU_C = 8.0
D_POOL = 256
POOL_WINDOWS = (2, 4, 8, 16)
POOL_GROUP_DIM = D_POOL // len(POOL_WINDOWS)
MAX_WINDOW = max(POOL_WINDOWS)
D_XATTN = 256
XATTN_HEADS = 4
XATTN_HEAD_DIM = D_XATTN // XATTN_HEADS
D_IN = 2 * D_RNN + D_POOL + D_XATTN
N_EXPERTS = 32
TOP_K = 4
D_FF = D_MODEL
SWIGLU_ALPHA = 1.702
SWIGLU_LIMIT = 7.0
RMS_EPS = 1e-6

SUBLANES = 8
MXU_DIM = 256
VMEM_LIMIT_BYTES = 56 * 1024 * 1024

MIX_ROWS = 512
HIST = 16
EXPERT_ROWS = 512
COMBINE_ROWS = 512
ROUTE_CHUNK = 1024
SC_ROWS = 128

BF16 = jnp.bfloat16
F32 = jnp.float32


def _rms(xf, g):
    return xf * lax.rsqrt(jnp.mean(xf * xf, axis=-1, keepdims=True) + RMS_EPS) * g


def _dot(a, b):
    return jnp.dot(a, b, preferred_element_type=F32)


def _pack_bf16_pairs(v):
    n = v.shape[1] // 2
    bits = lax.bitcast_convert_type(v.astype(BF16).astype(F32), jnp.uint32)
    return (bits[:, :n] >> 16) | (bits[:, n:] & jnp.uint32(0xFFFF0000))


def _unpack_bf16_pairs(w):
    lo = lax.bitcast_convert_type(w << 16, F32)
    hi = lax.bitcast_convert_type(w & jnp.uint32(0xFFFF0000), F32)
    return lo, hi


def _kv_kernel(mem_ref, g_ref, w_ref, kv_ref):
    mn = _rms(mem_ref[0], g_ref[...])
    kv_ref[0] = _dot(mn.astype(BF16), w_ref[...])


def _kv_call(mem, g, w_kv_bf):
    b = mem.shape[0]
    return pl.pallas_call(
        _kv_kernel,
        grid=(b,),
        in_specs=[
            pl.BlockSpec((1, N_MEM, D_MODEL), lambda i: (i, 0, 0)),
            pl.BlockSpec((1, D_MODEL), lambda i: (0, 0)),
            pl.BlockSpec((D_MODEL, 2 * D_XATTN), lambda i: (0, 0)),
        ],
        out_specs=pl.BlockSpec((1, N_MEM, 2 * D_XATTN), lambda i: (i, 0, 0)),
        out_shape=jax.ShapeDtypeStruct((b, N_MEM, 2 * D_XATTN), F32),
        name="kv_proj",
    )(mem, g, w_kv_bf)


def _mix_kernel(x_ref, g_mix_ref, w_in_ref, conv_w_ref, conv_b_ref, wg_ref, b_a_ref, b_i_ref, lam_ref,
                w_pool_ref, pool_scale_ref, kbd_ref, vbd_ref, w_out_ref, g_ffn_ref, w_rt_ref, b_r_ref,
                x1_ref, hn_ref, idx_ref, gate_ref,
                urnn_buf, upool_buf, hcar, a_s, b_s, mixed_s):
    tm = MIX_ROWS
    s = pl.program_id(1)

    @pl.when(s == 0)
    def _():
        urnn_buf[0:HIST, :] = jnp.zeros((HIST, D_RNN), F32)
        upool_buf[0:HIST, :] = jnp.zeros((HIST, D_POOL), F32)
        hcar[...] = jnp.zeros((1, D_RNN), F32)

    x = x_ref[0]
    h = _rms(x, g_mix_ref[...])
    proj = _dot(h.astype(BF16), w_in_ref[...])
    g_rnn = proj[:, D_RNN:2 * D_RNN]
    q_mem = proj[:, 2 * D_RNN + D_POOL:]
    urnn_buf[HIST:, :] = proj[:, :D_RNN]
    upool_buf[HIST:, :] = proj[:, 2 * D_RNN:2 * D_RNN + D_POOL]

    row = lax.broadcasted_iota(jnp.int32, (tm, 1), 0)
    grow = row + s * tm

    uc = conv_b_ref[...] + conv_w_ref[CONV_WIDTH - 1:CONV_WIDTH, :] * urnn_buf[pl.ds(HIST, tm), :]
    for k in range(CONV_WIDTH - 1):
        off = HIST - (CONV_WIDTH - 1) + k
        uc = uc + conv_w_ref[k:k + 1, :] * urnn_buf[pl.ds(off, tm), :]

    ucb = uc.astype(BF16)
    pre = [_dot(ucb[:, c * MXU_DIM:(c + 1) * MXU_DIM], wg_ref[c]) for c in range(D_RNN // MXU_DIM)]
    pre_a = jnp.concatenate([p[:, :MXU_DIM] for p in pre], axis=1)
    pre_i = jnp.concatenate([p[:, MXU_DIM:] for p in pre], axis=1)
    r = jax.nn.sigmoid(pre_a + b_a_ref[...])
    ig = jax.nn.sigmoid(pre_i + b_i_ref[...])
    lam = lam_ref[...]
    softplus_neg = jnp.maximum(-lam, 0.0) + jnp.log(1.0 + jnp.exp(-jnp.abs(lam)))
    log_a = (-LRU_C) * r * softplus_neg
    a = jnp.exp(log_a)
    mult = jnp.sqrt(jnp.maximum(1.0 - a * a, 0.0))
    mult = jnp.where(grow == 0, 1.0, mult)
    bx = mult * (ig * uc)

    row8 = lax.broadcasted_iota(jnp.int32, (tm, D_RNN), 0) & (SUBLANES - 1)
    d = 1
    while d < SUBLANES:
        a_sh = pltpu.roll(a, d, axis=0)
        b_sh = pltpu.roll(bx, d, axis=0)
        m = row8 >= d
        bx = jnp.where(m, a * b_sh + bx, bx)
        a = jnp.where(m, a * a_sh, a)
        d *= 2
    a_s[...] = a
    b_s[...] = bx

    def carry_body(g, carry):
        i0 = pl.multiple_of(g * SUBLANES, SUBLANES)
        hv = a_s[pl.ds(i0, SUBLANES), :] * carry + b_s[pl.ds(i0, SUBLANES), :]
        b_s[pl.ds(i0, SUBLANES), :] = hv
        return jnp.broadcast_to(hv[SUBLANES - 1:SUBLANES, :], (SUBLANES, D_RNN))

    carry = lax.fori_loop(0, tm // SUBLANES, carry_body,
                          jnp.broadcast_to(hcar[...], (SUBLANES, D_RNN)), unroll=4)
    hcar[...] = carry[0:1, :]
    y_rnn = b_s[...] * jax.nn.gelu(g_rnn)
    mixed_s[:, :D_RNN] = y_rnn.astype(BF16)

    e = upool_buf[...]
    s2 = e + pltpu.roll(e, 1, axis=0)
    s4 = s2 + pltpu.roll(s2, 2, axis=0)
    s8 = s4 + pltpu.roll(s4, 4, axis=0)
    s16 = s8 + pltpu.roll(s8, 8, axis=0)
    grp = lax.broadcasted_iota(jnp.int32, (tm, D_POOL), 1) // POOL_GROUP_DIM
    wsum = jnp.where(grp == 0, s2[HIST:], jnp.where(grp == 1, s4[HIST:], jnp.where(grp == 2, s8[HIST:], s16[HIST:])))
    win = jnp.where(grp == 0, 2.0, jnp.where(grp == 1, 4.0, jnp.where(grp == 2, 8.0, 16.0)))
    pos = (grow + 1).astype(F32)
    dpool = wsum / jnp.minimum(pos, win) - e[HIST:]
    y_pool = _dot(dpool.astype(BF16), w_pool_ref[...]) * pool_scale_ref[...]
    mixed_s[:, D_RNN:D_RNN + D_POOL] = y_pool.astype(BF16)

    sc = _dot(q_mem.astype(BF16), kbd_ref[0]) * (XATTN_HEAD_DIM ** -0.5)
    ps = []
    for hh in range(XATTN_HEADS):
        sh = sc[:, hh * N_MEM:(hh + 1) * N_MEM]
        ph = jnp.exp(sh - jnp.max(sh, axis=-1, keepdims=True))
        ps.append((ph * (1.0 / jnp.sum(ph, axis=-1, keepdims=True))).astype(BF16))
    y_mem = _dot(jnp.concatenate(ps, axis=1), vbd_ref[0])
    mixed_s[:, D_RNN + D_POOL:] = y_mem.astype(BF16)

    x1 = x + _dot(mixed_s[...], w_out_ref[...])
    x1_ref[0] = x1

    urnn_buf[0:HIST, :] = urnn_buf[tm:tm + HIST, :]
    upool_buf[0:HIST, :] = upool_buf[tm:tm + HIST, :]

    hn = _rms(x1, g_ffn_ref[...]).astype(BF16)
    hn_ref[...] = _pack_bf16_pairs(hn)
    logits = lax.dot_general(w_rt_ref[...], hn, (((1,), (1,)), ((), ())),
                             preferred_element_type=F32) + b_r_ref[...]
    eid = lax.broadcasted_iota(jnp.int32, (N_EXPERTS, tm), 0)
    vals, idxs = [], []
    for _ in range(TOP_K):
        mx = jnp.max(logits, axis=0, keepdims=True)
        ix = jnp.min(jnp.where(logits == mx, eid, N_EXPERTS), axis=0, keepdims=True)
        vals.append(mx)
        idxs.append(ix)
        logits = jnp.where(eid == ix, -jnp.inf, logits)
    ex = [jnp.exp(v - vals[0]) for v in vals]
    den = ex[0] + ex[1] + ex[2] + ex[3]
    idx_ref[...] = jnp.concatenate(idxs, axis=0)
    gate_ref[...] = jnp.concatenate([e_ / den for e_ in ex], axis=0)


def _mix_call(x, g_mix, w_in, conv_w, conv_b, wg, b_a, b_i, lam, w_pool, pool_scale, kbd, vbd, w_out,
              g_ffn, w_rt, b_r):
    b, s, d = x.shape
    tm = MIX_ROWS
    ns = s // tm
    const2 = lambda shape: pl.BlockSpec(shape, lambda bi, si: (0, 0))
    const3 = lambda shape: pl.BlockSpec(shape, lambda bi, si: (0, 0, 0))
    return pl.pallas_call(
        _mix_kernel,
        grid=(b, ns),
        in_specs=[
            pl.BlockSpec((1, tm, d), lambda bi, si: (bi, si, 0)),
            const2((1, d)),
            const2((d, D_IN)),
            const2((CONV_WIDTH, D_RNN)),
            const2((1, D_RNN)),
            const3((D_RNN // MXU_DIM, MXU_DIM, 2 * MXU_DIM)),
            const2((1, D_RNN)),
            const2((1, D_RNN)),
            const2((1, D_RNN)),
            const2((D_POOL, D_POOL)),
            const2((1, D_POOL)),
            pl.BlockSpec((1, D_XATTN, XATTN_HEADS * N_MEM), lambda bi, si: (bi, 0, 0)),
            pl.BlockSpec((1, XATTN_HEADS * N_MEM, D_XATTN), lambda bi, si: (bi, 0, 0)),
            const2((d, d)),
            const2((1, d)),
            const2((N_EXPERTS, d)),
            const2((N_EXPERTS, 1)),
        ],
        out_specs=[
            pl.BlockSpec((1, tm, d), lambda bi, si: (bi, si, 0)),
            pl.BlockSpec((tm, d // 2), lambda bi, si: (bi * ns + si, 0)),
            pl.BlockSpec((TOP_K, tm), lambda bi, si: (0, bi * ns + si)),
            pl.BlockSpec((TOP_K, tm), lambda bi, si: (0, bi * ns + si)),
        ],
        out_shape=[
            jax.ShapeDtypeStruct((b, s, d), F32),
            jax.ShapeDtypeStruct((b * s, d // 2), jnp.uint32),
            jax.ShapeDtypeStruct((TOP_K, b * s), jnp.int32),
            jax.ShapeDtypeStruct((TOP_K, b * s), F32),
        ],
        scratch_shapes=[
            pltpu.VMEM((HIST + tm, D_RNN), F32),
            pltpu.VMEM((HIST + tm, D_POOL), F32),
            pltpu.VMEM((1, D_RNN), F32),
            pltpu.VMEM((tm, D_RNN), F32),
            pltpu.VMEM((tm, D_RNN), F32),
            pltpu.VMEM((tm, d), BF16),
        ],
        compiler_params=pltpu.CompilerParams(
            dimension_semantics=("arbitrary", "arbitrary"), vmem_limit_bytes=VMEM_LIMIT_BYTES),
        name="mixer_router",
    )(x, g_mix, w_in, conv_w, conv_b, wg, b_a, b_i, lam, w_pool, pool_scale, kbd, vbd, w_out, g_ffn, w_rt, b_r)


def _num_blocks(t):
    bm = EXPERT_ROWS
    return (t * TOP_K + N_EXPERTS * (bm - 1) + bm - 1) // bm


def _route_kernel(idx_ref, dest_ref, meta_ref, rank_s, carry_s, start_s, tri_s):
    p = pl.program_id(0)
    c = pl.program_id(1)
    nc = pl.num_programs(1)
    ch = ROUTE_CHUNK
    bm = float(EXPERT_ROWS)
    nbl = meta_ref.shape[1]
    eid = lax.broadcasted_iota(jnp.int32, (N_EXPERTS, ch), 0)
    idxc = idx_ref[...]

    @pl.when(jnp.logical_and(p == 0, c == 0))
    def _():
        carry_s[...] = jnp.zeros(carry_s.shape, F32)
        tri_s[...] = (lax.broadcasted_iota(jnp.int32, (ch, ch), 0)
                      < lax.broadcasted_iota(jnp.int32, (ch, ch), 1)).astype(BF16)

    @pl.when(p == 0)
    def _():
        sel = jnp.zeros((N_EXPERTS, ch), F32)
        for k in range(TOP_K):
            sel = sel + (idxc[k:k + 1, :] == eid).astype(F32)
        rank_s[c] = _dot(sel.astype(BF16), tri_s[...]) + carry_s[:, 0:1]
        carry_s[...] = carry_s[...] + jnp.sum(sel, axis=1, keepdims=True)

    @pl.when(jnp.logical_and(p == 0, c == nc - 1))
    def _():
        counts = carry_s[...]
        padded = jnp.floor((counts + (bm - 1.0)) * (1.0 / bm)) * bm
        e128 = lax.broadcasted_iota(jnp.int32, counts.shape, 0)
        pad_end = padded
        sh = 1
        while sh < N_EXPERTS:
            pad_end = pad_end + jnp.where(e128 >= sh, pltpu.roll(pad_end, sh, axis=0), 0.0)
            sh *= 2
        pad_start = pad_end - padded
        start_s[...] = pad_start
        total = pad_end[N_EXPERTS - 1:N_EXPERTS, 0:1]
        lane = lax.broadcasted_iota(jnp.int32, (1, nbl), 1)
        lane_f = lane.astype(F32)
        bclamp = jnp.minimum(lane_f, total * (1.0 / bm) - 1.0)
        bstart = bclamp * bm
        pe, ps, cn, pd = pad_end[:, 0:1], pad_start[:, 0:1], counts[:, 0:1], padded[:, 0:1]
        be = jnp.minimum(jnp.sum((pe <= bstart).astype(F32), axis=0, keepdims=True), N_EXPERTS - 1.0)
        esub = lax.broadcasted_iota(jnp.int32, (N_EXPERTS, nbl), 0)
        onehot = esub.astype(F32) == be
        bv = jnp.sum(jnp.where(onehot, cn - (bstart - ps), 0.0), axis=0, keepdims=True)
        bv = jnp.where(lane_f * bm < total, jnp.clip(bv, 0.0, bm), 0.0)
        own = esub == lane
        fill_start = jnp.sum(jnp.where(own, ps + cn, 0.0), axis=0, keepdims=True)
        fill_n = jnp.sum(jnp.where(own, pd - cn, 0.0), axis=0, keepdims=True)
        n_used = jnp.broadcast_to(total * (1.0 / bm), (1, nbl))
        zero = jnp.zeros((1, nbl), F32)
        meta_ref[...] = jnp.concatenate([be, bv, bclamp, fill_start, fill_n, n_used, zero, zero],
                                        axis=0).astype(jnp.int32)

    @pl.when(p == 1)
    def _():
        base = start_s[:, 0:1] + rank_s[c]
        rows = [jnp.sum(jnp.where(idxc[k:k + 1, :] == eid, base, 0.0), axis=0, keepdims=True)
                for k in range(TOP_K)]
        dest_ref[...] = jnp.concatenate(rows, axis=0).astype(jnp.int32)


def _route_call(idx):
    k, t = idx.shape
    ch = ROUTE_CHUNK
    nc = t // ch
    nbl = -(-_num_blocks(t) // 128) * 128
    return pl.pallas_call(
        _route_kernel,
        grid=(2, nc),
        in_specs=[pl.BlockSpec((k, ch), lambda p, c: (0, c))],
        out_specs=[
            pl.BlockSpec((k, ch), lambda p, c: (0, c * p)),
            pl.BlockSpec((8, nbl), lambda p, c: (0, 0)),
        ],
        out_shape=[
            jax.ShapeDtypeStruct((k, t), jnp.int32),
            jax.ShapeDtypeStruct((8, nbl), jnp.int32),
        ],
        scratch_shapes=[
            pltpu.VMEM((nc, N_EXPERTS, ch), F32),
            pltpu.VMEM((N_EXPERTS, 128), F32),
            pltpu.VMEM((N_EXPERTS, 128), F32),
            pltpu.VMEM((ch, ch), BF16),
        ],
        compiler_params=pltpu.CompilerParams(dimension_semantics=("arbitrary", "arbitrary")),
        name="route_tables",
    )(idx)


def _sc_workers():
    info = plsc.get_sparse_core_info()
    return info.num_cores, info.num_cores * info.num_subcores


def _sc_dispatch_call(hn, dest, n_rows):
    t, w = hn.shape
    nc, nw = _sc_workers()
    per_w = t // nw
    ch = SC_ROWS
    mesh = plsc.VectorSubcoreMesh(core_axis_name="c", subcore_axis_name="s")

    @functools.partial(
        pl.kernel, mesh=mesh,
        out_type=jax.ShapeDtypeStruct((n_rows, w), hn.dtype),
        scratch_types=[pltpu.VMEM((ch,), jnp.int32), pltpu.VMEM((ch, w), hn.dtype)],
    )
    def body(hn_hbm, dest_hbm, xs_hbm, idx_v, rows_v):
        wid = lax.axis_index("s") * nc + lax.axis_index("c")

        @pl.loop(0, per_w // ch)
        def _(ci):
            base = pl.multiple_of(wid * per_w + ci * ch, ch)
            pltpu.sync_copy(hn_hbm.at[pl.ds(base, ch)], rows_v)
            for k in range(TOP_K):
                pltpu.sync_copy(dest_hbm.at[k, pl.ds(base, ch)], idx_v)
                pltpu.sync_copy(rows_v, xs_hbm.at[idx_v])

    return body(hn, dest)


def _sc_gather_call(ys, dest):
    k_, t = dest.shape
    w = ys.shape[1]
    nc, nw = _sc_workers()
    per_w = t // nw
    ch = SC_ROWS
    mesh = plsc.VectorSubcoreMesh(core_axis_name="c", subcore_axis_name="s")

    @functools.partial(
        pl.kernel, mesh=mesh,
        out_type=jax.ShapeDtypeStruct((k_, t, w), ys.dtype),
        scratch_types=[pltpu.VMEM((ch,), jnp.int32), pltpu.VMEM((ch, w), ys.dtype)],
    )
    def body(ys_hbm, dest_hbm, y4_hbm, idx_v, rows_v):
        wid = lax.axis_index("s") * nc + lax.axis_index("c")

        @pl.loop(0, per_w // ch)
        def _(ci):
            base = pl.multiple_of(wid * per_w + ci * ch, ch)
            for k in range(k_):
                pltpu.sync_copy(dest_hbm.at[k, pl.ds(base, ch)], idx_v)
                pltpu.sync_copy(ys_hbm.at[idx_v], rows_v)
                pltpu.sync_copy(rows_v, y4_hbm.at[k, pl.ds(base, ch)])

    return body(ys, dest)


def _pad_fill_kernel(fill_start_ref, fill_n_ref, n_used_ref, xs_in_ref, xs_ref, zero_s, zsem):
    del xs_in_ref
    bm = EXPERT_ROWS
    n_blocks = xs_ref.shape[0] // bm
    zero_s[...] = jnp.zeros(zero_s.shape, zero_s.dtype)

    def go(cp, wait):
        if wait:
            cp.wait()
        else:
            cp.start()

    def fill(e, wait):
        n = fill_n_ref[e]
        st = fill_start_ref[e]
        head = n & (SUBLANES - 1)
        for j in range(SUBLANES - 1):
            cp = pltpu.make_async_copy(zero_s.at[pl.ds(0, 1), :], xs_ref.at[pl.ds(st + j, 1), :], zsem)
            pl.when(j < head)(functools.partial(go, cp, wait))
        st = pl.multiple_of(st + head, SUBLANES)
        bit = bm // 2
        while bit >= SUBLANES:
            cp = pltpu.make_async_copy(zero_s.at[pl.ds(0, bit), :], xs_ref.at[pl.ds(st, bit), :], zsem)
            pl.when((n & bit) != 0)(functools.partial(go, cp, wait))
            st = pl.multiple_of(st + (n & bit), SUBLANES)
            bit //= 2

    def tail(blk, wait):
        cp = pltpu.make_async_copy(zero_s, xs_ref.at[pl.ds(pl.multiple_of(blk * bm, bm), bm), :], zsem)
        go(cp, wait)

    for wait in (False, True):
        lax.fori_loop(0, N_EXPERTS, lambda e, c, wait=wait: (fill(e, wait), c)[1], 0)
        lax.fori_loop(n_used_ref[0], n_blocks, lambda blk, c, wait=wait: (tail(blk, wait), c)[1], 0)


def _pad_fill_call(fill_start, fill_n, n_used, xs):
    grid_spec = pltpu.PrefetchScalarGridSpec(
        num_scalar_prefetch=3,
        grid=(1,),
        in_specs=[pl.BlockSpec(memory_space=pl.ANY)],
        out_specs=pl.BlockSpec(memory_space=pl.ANY),
        scratch_shapes=[pltpu.VMEM((EXPERT_ROWS, xs.shape[1]), xs.dtype), pltpu.SemaphoreType.DMA],
    )
    return pl.pallas_call(
        _pad_fill_kernel,
        grid_spec=grid_spec,
        out_shape=jax.ShapeDtypeStruct(xs.shape, xs.dtype),
        input_output_aliases={3: 0},
        compiler_params=pltpu.CompilerParams(dimension_semantics=("arbitrary",)),
        name="pad_fill",
    )(fill_start, fill_n, n_used, xs)


def _expert_kernel(be_ref, bv_ref, bi_ref, x_ref, wgu_ref, bgu_ref, wd_ref, bd_ref, y_ref, wgu_bf, wd_bf, h_s):
    del bi_ref
    i = pl.program_id(0)
    e = be_ref[i]
    e_prev = be_ref[jnp.maximum(i - 1, 0)]
    valid = bv_ref[i]

    @pl.when(jnp.logical_and(valid > 0, jnp.logical_or(i == 0, e != e_prev)))
    def _():
        wgu_bf[...] = wgu_ref[0].astype(BF16)
        wd_bf[...] = wd_ref[0].astype(BF16)

    @pl.when(valid > 0)
    def _():
        x_lo, x_hi = _unpack_bf16_pairs(x_ref[...])
        xb = jnp.concatenate([x_lo.astype(BF16), x_hi.astype(BF16)], axis=1)
        for c in range(D_FF // (2 * MXU_DIM)):
            lo = c * 2 * MXU_DIM
            hi = lo + 2 * MXU_DIM
            g = _dot(xb, wgu_bf[:, lo:hi]) + bgu_ref[0, :, lo:hi]
            up = _dot(xb, wgu_bf[:, D_FF + lo:D_FF + hi]) + bgu_ref[0, :, D_FF + lo:D_FF + hi]
            g = jnp.minimum(g, SWIGLU_LIMIT)
            up = jnp.clip(up, -SWIGLU_LIMIT, SWIGLU_LIMIT)
            glu = g * jax.nn.sigmoid(g * SWIGLU_ALPHA)
            h_s[:, lo:hi] = ((up + 1.0) * glu).astype(BF16)
        y_ref[...] = _pack_bf16_pairs(_dot(h_s[...], wd_bf[...]) + bd_ref[0])

    @pl.when(valid == 0)
    def _():
        y_ref[...] = jnp.zeros(y_ref.shape, y_ref.dtype)


def _expert_call(block_e, block_valid, block_idx, xs, w_gu, b_gu, w_down, b_down):
    n_pad, dw = xs.shape
    d = 2 * dw
    bm = EXPERT_ROWS
    nb = n_pad // bm
    grid_spec = pltpu.PrefetchScalarGridSpec(
        num_scalar_prefetch=3,
        grid=(nb,),
        in_specs=[
            pl.BlockSpec((bm, dw), lambda i, be, bv, bi: (bi[i], 0)),
            pl.BlockSpec((1, d, 2 * D_FF), lambda i, be, bv, bi: (be[i], 0, 0)),
            pl.BlockSpec((1, 1, 2 * D_FF), lambda i, be, bv, bi: (be[i], 0, 0)),
            pl.BlockSpec((1, D_FF, d), lambda i, be, bv, bi: (be[i], 0, 0)),
            pl.BlockSpec((1, 1, d), lambda i, be, bv, bi: (be[i], 0, 0)),
        ],
        out_specs=pl.BlockSpec((bm, dw), lambda i, be, bv, bi: (i, 0)),
        scratch_shapes=[
            pltpu.VMEM((d, 2 * D_FF), BF16),
            pltpu.VMEM((D_FF, d), BF16),
            pltpu.VMEM((bm, D_FF), BF16),
        ],
    )
    return pl.pallas_call(
        _expert_kernel,
        grid_spec=grid_spec,
        out_shape=jax.ShapeDtypeStruct((n_pad, dw), jnp.uint32),
        compiler_params=pltpu.CompilerParams(
            dimension_semantics=("arbitrary",), vmem_limit_bytes=VMEM_LIMIT_BYTES),
        name="expert_ffn",
    )(block_e, block_valid, block_idx, xs, w_gu, b_gu, w_down, b_down)


def _combine_kernel(x1_ref, gate_ref, y4_ref, g_ref, out_ref):
    x1 = x1_ref[...]
    n = x1.shape[1] // 2
    gate = gate_ref[...]
    acc_lo, acc_hi = x1[:, :n], x1[:, n:]
    for k in range(TOP_K):
        y_lo, y_hi = _unpack_bf16_pairs(y4_ref[k])
        acc_lo = acc_lo + gate[:, k:k + 1] * y_lo
        acc_hi = acc_hi + gate[:, k:k + 1] * y_hi
    out_ref[...] = _rms(jnp.concatenate([acc_lo, acc_hi], axis=1), g_ref[...])


def _combine_call(x1, gate_t, y4, g_final):
    t, d = x1.shape
    tt = COMBINE_ROWS
    return pl.pallas_call(
        _combine_kernel,
        grid=(t // tt,),
        in_specs=[
            pl.BlockSpec((tt, d), lambda i: (i, 0)),
            pl.BlockSpec((tt, TOP_K), lambda i: (i, 0)),
            pl.BlockSpec((TOP_K, tt, d // 2), lambda i: (0, i, 0)),
            pl.BlockSpec((1, d), lambda i: (0, 0)),
        ],
        out_specs=pl.BlockSpec((tt, d), lambda i: (i, 0)),
        out_shape=jax.ShapeDtypeStruct((t, d), F32),
        compiler_params=pltpu.CompilerParams(dimension_semantics=("arbitrary",)),
        name="combine_norm",
    )(x1, gate_t, y4, g_final)


def _block_diag(w):
    hh, n, _ = w.shape
    eye = jnp.eye(hh, dtype=w.dtype)
    return jnp.einsum("hij,hg->higj", w, eye).reshape(hh * n, hh * n)


def kernel(x, mem, norm_mix_g, w_in, conv_w, conv_b, w_rg_a, b_rg_a, w_rg_i, b_rg_i, lru_lambda, w_pool,
           pool_scale, mem_norm_g, w_mem_kv, w_out, norm_ffn_g, w_router, b_router, w_gu, b_gu, w_down,
           b_down, final_norm_g):
    b, s, d = x.shape
    t = b * s
    l = 0
    row = lambda v: v.reshape(1, -1)

    kv = _kv_call(mem, row(mem_norm_g[l]), w_mem_kv[l].astype(BF16))
    kh = kv[..., :D_XATTN].reshape(b, N_MEM, XATTN_HEADS, XATTN_HEAD_DIM)
    vh = kv[..., D_XATTN:].reshape(b, N_MEM, XATTN_HEADS, XATTN_HEAD_DIM)
    eye_h = jnp.eye(XATTN_HEADS, dtype=F32)
    kbd = jnp.einsum("bmhd,hg->bhdgm", kh, eye_h).reshape(b, D_XATTN, XATTN_HEADS * N_MEM).astype(BF16)
    vbd = jnp.einsum("bmhd,hg->bgmhd", vh, eye_h).reshape(b, XATTN_HEADS * N_MEM, D_XATTN).astype(BF16)

    heads_per = MXU_DIM // RNN_HEAD_DIM
    wg = jnp.stack([
        jnp.concatenate([_block_diag(w_rg_a[l, c * heads_per:(c + 1) * heads_per]),
                         _block_diag(w_rg_i[l, c * heads_per:(c + 1) * heads_per])], axis=1)
        for c in range(D_RNN // MXU_DIM)]).astype(BF16)

    x1, hn, idx, gate = _mix_call(
        x, row(norm_mix_g[l]), w_in[l].astype(BF16), conv_w[l], row(conv_b[l]), wg, row(b_rg_a[l]),
        row(b_rg_i[l]), row(lru_lambda[l]), _block_diag(w_pool[l]).astype(BF16), row(pool_scale[l]), kbd, vbd,
        w_out[l].astype(BF16), row(norm_ffn_g[l]), w_router[l].T.astype(BF16), b_router[l].reshape(-1, 1))

    n_blocks = _num_blocks(t)
    dest, meta = _route_call(idx)
    be, bv, bi = meta[0, :n_blocks], meta[1, :n_blocks], meta[2, :n_blocks]
    fill_start, fill_n, n_used = meta[3, :N_EXPERTS], meta[4, :N_EXPERTS], meta[5, :1]

    xs = _sc_dispatch_call(hn, dest, n_blocks * EXPERT_ROWS)
    xs = _pad_fill_call(fill_start, fill_n, n_used, xs)
    ys = _expert_call(be, bv, bi, xs, w_gu[l], b_gu[l].reshape(N_EXPERTS, 1, -1), w_down[l],
                      b_down[l].reshape(N_EXPERTS, 1, -1))
    y4 = _sc_gather_call(ys, dest)
    out = _combine_call(x1.reshape(t, d), gate.T, y4, row(final_norm_g))
    return out.reshape(b, s, d)
```

```python
import functools
import math

import jax
import jax.numpy as jnp
from jax import lax
from jax.experimental import pallas as pl
from jax.experimental.pallas import tpu as pltpu
from jax.experimental.pallas import tpu_sc as plsc

D_MODEL = 1024
N_MEM = 256
D_RNN = 512
RNN_HEADS = 8
RNN_HEAD_DIM = D_RNN // RNN_HEADS
CONV_WIDTH = 4
LRU_C = 8.0
D_POOL = 256
POOL_WINDOWS = (2, 4, 8, 16)
POOL_GROUP_DIM = D_POOL // len(POOL_WINDOWS)
MAX_WINDOW = max(POOL_WINDOWS)
D_XATTN = 256
XATTN_HEADS = 4
XATTN_HEAD_DIM = D_XATTN // XATTN_HEADS
D_IN = 2 * D_RNN + D_POOL + D_XATTN
N_EXPERTS = 32
TOP_K = 4
D_FF = D_MODEL
SWIGLU_ALPHA = 1.702
SWIGLU_LIMIT = 7.0
RMS_EPS = 1e-6

SUBLANES = 8
MXU_DIM = 256
VMEM_LIMIT_BYTES = 56 * 1024 * 1024

MIX_ROWS = 512
HIST = 16
EXPERT_ROWS = 528
COMBINE_ROWS = 512
ROUTE_CHUNK = 1024
SC_ROWS = 128

BF16 = jnp.bfloat16
F32 = jnp.float32


def _rms(xf, g):
    return xf * lax.rsqrt(jnp.mean(xf * xf, axis=-1, keepdims=True) + RMS_EPS) * g


def _dot(a, b):
    return jnp.dot(a, b, preferred_element_type=F32)


def _pack_bf16_pairs(v):
    n = v.shape[1] // 2
    bits = lax.bitcast_convert_type(v.astype(BF16).astype(F32), jnp.uint32)
    return (bits[:, :n] >> 16) | (bits[:, n:] & jnp.uint32(0xFFFF0000))


def _unpack_bf16_pairs(w):
    lo = lax.bitcast_convert_type(w << 16, F32)
    hi = lax.bitcast_convert_type(w & jnp.uint32(0xFFFF0000), F32)
    return lo, hi


def _kv_kernel(mem_ref, g_ref, w_ref, kv_ref):
    mn = _rms(mem_ref[0], g_ref[...])
    kv_ref[0] = _dot(mn.astype(BF16), w_ref[...])


def _kv_call(mem, g, w_kv_bf):
    b = mem.shape[0]
    return pl.pallas_call(
        _kv_kernel,
        grid=(b,),
        in_specs=[
            pl.BlockSpec((1, N_MEM, D_MODEL), lambda i: (i, 0, 0)),
            pl.BlockSpec((1, D_MODEL), lambda i: (0, 0)),
            pl.BlockSpec((D_MODEL, 2 * D_XATTN), lambda i: (0, 0)),
        ],
        out_specs=pl.BlockSpec((1, N_MEM, 2 * D_XATTN), lambda i: (i, 0, 0)),
        out_shape=jax.ShapeDtypeStruct((b, N_MEM, 2 * D_XATTN), F32),
        name="kv_proj",
    )(mem, g, w_kv_bf)


def _mix_kernel(x_ref, g_mix_ref, w_in_ref, conv_w_ref, conv_b_ref, wg_ref, b_a_ref, b_i_ref, lam_ref,
                w_pool_ref, pool_scale_ref, kbd_ref, vbd_ref, w_out_ref, g_ffn_ref, w_rt_ref, b_r_ref,
                x1_ref, hn_ref, idx_ref, gate_ref,
                urnn_buf, upool_buf, hcar, a_s, b_s, mixed_s):
    tm = MIX_ROWS
    s = pl.program_id(1)

    @pl.when(s == 0)
    def _():
        urnn_buf[0:HIST, :] = jnp.zeros((HIST, D_RNN), F32)
        upool_buf[0:HIST, :] = jnp.zeros((HIST, D_POOL), F32)
        hcar[...] = jnp.zeros((1, D_RNN), F32)

    x = x_ref[0]
    h = _rms(x, g_mix_ref[...])
    proj = _dot(h.astype(BF16), w_in_ref[...])
    g_rnn = proj[:, D_RNN:2 * D_RNN]
    q_mem = proj[:, 2 * D_RNN + D_POOL:]
    urnn_buf[HIST:, :] = proj[:, :D_RNN]
    upool_buf[HIST:, :] = proj[:, 2 * D_RNN:2 * D_RNN + D_POOL]

    row = lax.broadcasted_iota(jnp.int32, (tm, 1), 0)
    grow = row + s * tm

    uc = conv_b_ref[...] + conv_w_ref[CONV_WIDTH - 1:CONV_WIDTH, :] * urnn_buf[pl.ds(HIST, tm), :]
    for k in range(CONV_WIDTH - 1):
        off = HIST - (CONV_WIDTH - 1) + k
        uc = uc + conv_w_ref[k:k + 1, :] * urnn_buf[pl.ds(off, tm), :]

    ucb = uc.astype(BF16)
    pre = [_dot(ucb[:, c * MXU_DIM:(c + 1) * MXU_DIM], wg_ref[c]) for c in range(D_RNN // MXU_DIM)]
    pre_a = jnp.concatenate([p[:, :MXU_DIM] for p in pre], axis=1)
    pre_i = jnp.concatenate([p[:, MXU_DIM:] for p in pre], axis=1)
    r = jax.nn.sigmoid(pre_a + b_a_ref[...])
    ig = jax.nn.sigmoid(pre_i + b_i_ref[...])
    lam = lam_ref[...]
    softplus_neg = jnp.maximum(-lam, 0.0) + jnp.log(1.0 + jnp.exp(-jnp.abs(lam)))
    log_a = (-LRU_C) * r * softplus_neg
    a = jnp.exp(log_a)
    mult = jnp.sqrt(jnp.maximum(1.0 - a * a, 0.0))
    mult = jnp.where(grow == 0, 1.0, mult)
    bx = mult * (ig * uc)

    row8 = lax.broadcasted_iota(jnp.int32, (tm, D_RNN), 0) & (SUBLANES - 1)
    d = 1
    while d < SUBLANES:
        a_sh = pltpu.roll(a, d, axis=0)
        b_sh = pltpu.roll(bx, d, axis=0)
        m = row8 >= d
        bx = jnp.where(m, a * b_sh + bx, bx)
        a = jnp.where(m, a * a_sh, a)
        d *= 2
    a_s[...] = a
    b_s[...] = bx

    def carry_body(g, carry):
        i0 = pl.multiple_of(g * SUBLANES, SUBLANES)
        hv = a_s[pl.ds(i0, SUBLANES), :] * carry + b_s[pl.ds(i0, SUBLANES), :]
        b_s[pl.ds(i0, SUBLANES), :] = hv
        return jnp.broadcast_to(hv[SUBLANES - 1:SUBLANES, :], (SUBLANES, D_RNN))

    carry = lax.fori_loop(0, tm // SUBLANES, carry_body,
                          jnp.broadcast_to(hcar[...], (SUBLANES, D_RNN)), unroll=4)
    hcar[...] = carry[0:1, :]
    y_rnn = b_s[...] * jax.nn.gelu(g_rnn)
    mixed_s[:, :D_RNN] = y_rnn.astype(BF16)

    e = upool_buf[...]
    s2 = e + pltpu.roll(e, 1, axis=0)
    s4 = s2 + pltpu.roll(s2, 2, axis=0)
    s8 = s4 + pltpu.roll(s4, 4, axis=0)
    s16 = s8 + pltpu.roll(s8, 8, axis=0)
    grp = lax.broadcasted_iota(jnp.int32, (tm, D_POOL), 1) // POOL_GROUP_DIM
    wsum = jnp.where(grp == 0, s2[HIST:], jnp.where(grp == 1, s4[HIST:], jnp.where(grp == 2, s8[HIST:], s16[HIST:])))
    win = jnp.where(grp == 0, 2.0, jnp.where(grp == 1, 4.0, jnp.where(grp == 2, 8.0, 16.0)))
    pos = (grow + 1).astype(F32)
    dpool = wsum / jnp.minimum(pos, win) - e[HIST:]
    y_pool = _dot(dpool.astype(BF16), w_pool_ref[...]) * pool_scale_ref[...]
    mixed_s[:, D_RNN:D_RNN + D_POOL] = y_pool.astype(BF16)

    sc = _dot(q_mem.astype(BF16), kbd_ref[0]) * (XATTN_HEAD_DIM ** -0.5)
    ps = []
    for hh in range(XATTN_HEADS):
        sh = sc[:, hh * N_MEM:(hh + 1) * N_MEM]
        ph = jnp.exp(sh - jnp.max(sh, axis=-1, keepdims=True))
        ps.append((ph * (1.0 / jnp.sum(ph, axis=-1, keepdims=True))).astype(BF16))
    y_mem = _dot(jnp.concatenate(ps, axis=1), vbd_ref[0])
    mixed_s[:, D_RNN + D_POOL:] = y_mem.astype(BF16)

    x1 = x + _dot(mixed_s[...], w_out_ref[...])
    x1_ref[0] = x1

    urnn_buf[0:HIST, :] = urnn_buf[tm:tm + HIST, :]
    upool_buf[0:HIST, :] = upool_buf[tm:tm + HIST, :]

    hn = _rms(x1, g_ffn_ref[...]).astype(BF16)
    hn_ref[...] = _pack_bf16_pairs(hn)
    logits = lax.dot_general(w_rt_ref[...], hn, (((1,), (1,)), ((), ())),
                             preferred_element_type=F32) + b_r_ref[...]
    eid = lax.broadcasted_iota(jnp.int32, (N_EXPERTS, tm), 0)
    vals, idxs = [], []
    for _ in range(TOP_K):
        mx = jnp.max(logits, axis=0, keepdims=True)
        ix = jnp.min(jnp.where(logits == mx, eid, N_EXPERTS), axis=0, keepdims=True)
        vals.append(mx)
        idxs.append(ix)
        logits = jnp.where(eid == ix, -jnp.inf, logits)
    ex = [jnp.exp(v - vals[0]) for v in vals]
    den = ex[0] + ex[1] + ex[2] + ex[3]
    idx_ref[...] = jnp.concatenate(idxs, axis=0)
    gate_ref[...] = jnp.concatenate([e_ / den for e_ in ex], axis=0)


def _mix_call(x, g_mix, w_in, conv_w, conv_b, wg, b_a, b_i, lam, w_pool, pool_scale, kbd, vbd, w_out,
              g_ffn, w_rt, b_r):
    b, s, d = x.shape
    tm = MIX_ROWS
    ns = s // tm
    const2 = lambda shape: pl.BlockSpec(shape, lambda bi, si: (0, 0))
    const3 = lambda shape: pl.BlockSpec(shape, lambda bi, si: (0, 0, 0))
    return pl.pallas_call(
        _mix_kernel,
        grid=(b, ns),
        in_specs=[
            pl.BlockSpec((1, tm, d), lambda bi, si: (bi, si, 0)),
            const2((1, d)),
            const2((d, D_IN)),
            const2((CONV_WIDTH, D_RNN)),
            const2((1, D_RNN)),
            const3((D_RNN // MXU_DIM, MXU_DIM, 2 * MXU_DIM)),
            const2((1, D_RNN)),
            const2((1, D_RNN)),
            const2((1, D_RNN)),
            const2((D_POOL, D_POOL)),
            const2((1, D_POOL)),
            pl.BlockSpec((1, D_XATTN, XATTN_HEADS * N_MEM), lambda bi, si: (bi, 0, 0)),
            pl.BlockSpec((1, XATTN_HEADS * N_MEM, D_XATTN), lambda bi, si: (bi, 0, 0)),
            const2((d, d)),
            const2((1, d)),
            const2((N_EXPERTS, d)),
            const2((N_EXPERTS, 1)),
        ],
        out_specs=[
            pl.BlockSpec((1, tm, d), lambda bi, si: (bi, si, 0)),
            pl.BlockSpec((tm, d // 2), lambda bi, si: (bi * ns + si, 0)),
            pl.BlockSpec((TOP_K, tm), lambda bi, si: (0, bi * ns + si)),
            pl.BlockSpec((TOP_K, tm), lambda bi, si: (0, bi * ns + si)),
        ],
        out_shape=[
            jax.ShapeDtypeStruct((b, s, d), F32),
            jax.ShapeDtypeStruct((b * s, d // 2), jnp.uint32),
            jax.ShapeDtypeStruct((TOP_K, b * s), jnp.int32),
            jax.ShapeDtypeStruct((TOP_K, b * s), F32),
        ],
        scratch_shapes=[
            pltpu.VMEM((HIST + tm, D_RNN), F32),
            pltpu.VMEM((HIST + tm, D_POOL), F32),
            pltpu.VMEM((1, D_RNN), F32),
            pltpu.VMEM((tm, D_RNN), F32),
            pltpu.VMEM((tm, D_RNN), F32),
            pltpu.VMEM((tm, d), BF16),
        ],
        compiler_params=pltpu.CompilerParams(
            dimension_semantics=("arbitrary", "arbitrary"), vmem_limit_bytes=VMEM_LIMIT_BYTES),
        name="mixer_router",
    )(x, g_mix, w_in, conv_w, conv_b, wg, b_a, b_i, lam, w_pool, pool_scale, kbd, vbd, w_out, g_ffn, w_rt, b_r)


def _num_blocks(t):
    bm = EXPERT_ROWS
    return (t * TOP_K + N_EXPERTS * (bm - 1) + bm - 1) // bm


def _route_kernel(idx_ref, dest_ref, meta_ref, rank_s, carry_s, start_s, tri_s):
    p = pl.program_id(0)
    c = pl.program_id(1)
    nc = pl.num_programs(1)
    ch = ROUTE_CHUNK
    bm = float(EXPERT_ROWS)
    nbl = meta_ref.shape[1]
    eid = lax.broadcasted_iota(jnp.int32, (N_EXPERTS, ch), 0)
    idxc = idx_ref[...]

    @pl.when(jnp.logical_and(p == 0, c == 0))
    def _():
        carry_s[...] = jnp.zeros(carry_s.shape, F32)
        tri_s[...] = (lax.broadcasted_iota(jnp.int32, (ch, ch), 0)
                      < lax.broadcasted_iota(jnp.int32, (ch, ch), 1)).astype(BF16)

    @pl.when(p == 0)
    def _():
        sel = jnp.zeros((N_EXPERTS, ch), F32)
        for k in range(TOP_K):
            sel = sel + (idxc[k:k + 1, :] == eid).astype(F32)
        rank_s[c] = _dot(sel.astype(BF16), tri_s[...]) + carry_s[:, 0:1]
        carry_s[...] = carry_s[...] + jnp.sum(sel, axis=1, keepdims=True)

    @pl.when(jnp.logical_and(p == 0, c == nc - 1))
    def _():
        counts = carry_s[...]

        def div_bm(v):
            q = jnp.floor(v * (1.0 / bm))
            return q + jnp.where((q + 1.0) * bm <= v, 1.0, 0.0) - jnp.where(q * bm > v, 1.0, 0.0)

        padded = div_bm(counts + (bm - 1.0)) * bm
        e128 = lax.broadcasted_iota(jnp.int32, counts.shape, 0)
        pad_end = padded
        sh = 1
        while sh < N_EXPERTS:
            pad_end = pad_end + jnp.where(e128 >= sh, pltpu.roll(pad_end, sh, axis=0), 0.0)
            sh *= 2
        pad_start = pad_end - padded
        start_s[...] = pad_start
        total = pad_end[N_EXPERTS - 1:N_EXPERTS, 0:1]
        lane = lax.broadcasted_iota(jnp.int32, (1, nbl), 1)
        lane_f = lane.astype(F32)
        n_used = div_bm(total)
        bclamp = jnp.minimum(lane_f, n_used - 1.0)
        bstart = bclamp * bm
        pe, ps, cn, pd = pad_end[:, 0:1], pad_start[:, 0:1], counts[:, 0:1], padded[:, 0:1]
        be = jnp.minimum(jnp.sum((pe <= bstart).astype(F32), axis=0, keepdims=True), N_EXPERTS - 1.0)
        esub = lax.broadcasted_iota(jnp.int32, (N_EXPERTS, nbl), 0)
        onehot = esub.astype(F32) == be
        bv = jnp.sum(jnp.where(onehot, cn - (bstart - ps), 0.0), axis=0, keepdims=True)
        bv = jnp.where(lane_f * bm < total, jnp.clip(bv, 0.0, bm), 0.0)
        own = esub == lane
        fill_start = jnp.sum(jnp.where(own, ps + cn, 0.0), axis=0, keepdims=True)
        fill_n = jnp.sum(jnp.where(own, pd - cn, 0.0), axis=0, keepdims=True)
        later = jnp.logical_and(esub.astype(F32) > be, cn > 0.0)
        nxt = jnp.min(jnp.where(later, esub.astype(F32), float(N_EXPERTS)), axis=0, keepdims=True)
        nxt = jnp.where(nxt == float(N_EXPERTS), -1.0, nxt)
        zero = jnp.zeros((1, nbl), F32)
        meta_ref[...] = jnp.concatenate(
            [be, bv, bclamp, fill_start, fill_n, jnp.broadcast_to(n_used, (1, nbl)), nxt, zero],
            axis=0).astype(jnp.int32)

    @pl.when(p == 1)
    def _():
        base = start_s[:, 0:1] + rank_s[c]
        rows = [jnp.sum(jnp.where(idxc[k:k + 1, :] == eid, base, 0.0), axis=0, keepdims=True)
                for k in range(TOP_K)]
        dest_ref[...] = jnp.concatenate(rows, axis=0).astype(jnp.int32)


def _route_call(idx):
    k, t = idx.shape
    ch = ROUTE_CHUNK
    nc = t // ch
    nbl = -(-_num_blocks(t) // 128) * 128
    return pl.pallas_call(
        _route_kernel,
        grid=(2, nc),
        in_specs=[pl.BlockSpec((k, ch), lambda p, c: (0, c))],
        out_specs=[
            pl.BlockSpec((k, ch), lambda p, c: (0, c * p)),
            pl.BlockSpec((8, nbl), lambda p, c: (0, 0)),
        ],
        out_shape=[
            jax.ShapeDtypeStruct((k, t), jnp.int32),
            jax.ShapeDtypeStruct((8, nbl), jnp.int32),
        ],
        scratch_shapes=[
            pltpu.VMEM((nc, N_EXPERTS, ch), F32),
            pltpu.VMEM((N_EXPERTS, 128), F32),
            pltpu.VMEM((N_EXPERTS, 128), F32),
            pltpu.VMEM((ch, ch), BF16),
        ],
        compiler_params=pltpu.CompilerParams(dimension_semantics=("arbitrary", "arbitrary")),
        name="route_tables",
    )(idx)


def _sc_workers():
    info = plsc.get_sparse_core_info()
    return info.num_cores, info.num_cores * info.num_subcores


def _sc_dispatch_call(hn, dest, n_rows):
    t, w = hn.shape
    nc, nw = _sc_workers()
    per_w = t // nw
    ch = SC_ROWS
    mesh = plsc.VectorSubcoreMesh(core_axis_name="c", subcore_axis_name="s")

    @functools.partial(
        pl.kernel, mesh=mesh,
        out_type=jax.ShapeDtypeStruct((n_rows, w), hn.dtype),
        scratch_types=[pltpu.VMEM((ch,), jnp.int32), pltpu.VMEM((ch, w), hn.dtype)],
    )
    def body(hn_hbm, dest_hbm, xs_hbm, idx_v, rows_v):
        wid = lax.axis_index("s") * nc + lax.axis_index("c")

        @pl.loop(0, per_w // ch)
        def _(ci):
            base = pl.multiple_of(wid * per_w + ci * ch, ch)
            pltpu.sync_copy(hn_hbm.at[pl.ds(base, ch)], rows_v)
            for k in range(TOP_K):
                pltpu.sync_copy(dest_hbm.at[k, pl.ds(base, ch)], idx_v)
                pltpu.sync_copy(rows_v, xs_hbm.at[idx_v])

    return body(hn, dest)


def _sc_gather_call(ys, dest):
    k_, t = dest.shape
    w = ys.shape[1]
    nc, nw = _sc_workers()
    per_w = t // nw
    ch = SC_ROWS
    mesh = plsc.VectorSubcoreMesh(core_axis_name="c", subcore_axis_name="s")

    @functools.partial(
        pl.kernel, mesh=mesh,
        out_type=jax.ShapeDtypeStruct((k_, t, w), ys.dtype),
        scratch_types=[pltpu.VMEM((ch,), jnp.int32), pltpu.VMEM((ch, w), ys.dtype)],
    )
    def body(ys_hbm, dest_hbm, y4_hbm, idx_v, rows_v):
        wid = lax.axis_index("s") * nc + lax.axis_index("c")

        @pl.loop(0, per_w // ch)
        def _(ci):
            base = pl.multiple_of(wid * per_w + ci * ch, ch)
            for k in range(k_):
                pltpu.sync_copy(dest_hbm.at[k, pl.ds(base, ch)], idx_v)
                pltpu.sync_copy(ys_hbm.at[idx_v], rows_v)
                pltpu.sync_copy(rows_v, y4_hbm.at[k, pl.ds(base, ch)])

    return body(ys, dest)


def _pad_fill_kernel(fill_start_ref, fill_n_ref, n_used_ref, xs_in_ref, xs_ref, zero_s, zsem):
    del xs_in_ref
    bm = EXPERT_ROWS
    n_blocks = xs_ref.shape[0] // bm
    zero_s[...] = jnp.zeros(zero_s.shape, zero_s.dtype)

    def go(cp, wait):
        if wait:
            cp.wait()
        else:
            cp.start()

    def fill(e, wait):
        n = fill_n_ref[e]
        st = fill_start_ref[e]
        head = n & (SUBLANES - 1)
        for j in range(SUBLANES - 1):
            cp = pltpu.make_async_copy(zero_s.at[pl.ds(0, 1), :], xs_ref.at[pl.ds(st + j, 1), :], zsem)
            pl.when(j < head)(functools.partial(go, cp, wait))
        st = pl.multiple_of(st + head, SUBLANES)
        bit = 1 << ((bm - 1).bit_length() - 1)
        while bit >= SUBLANES:
            cp = pltpu.make_async_copy(zero_s.at[pl.ds(0, bit), :], xs_ref.at[pl.ds(st, bit), :], zsem)
            pl.when((n & bit) != 0)(functools.partial(go, cp, wait))
            st = pl.multiple_of(st + (n & bit), SUBLANES)
            bit //= 2

    def tail(blk, wait):
        cp = pltpu.make_async_copy(zero_s, xs_ref.at[pl.ds(pl.multiple_of(blk * bm, bm), bm), :], zsem)
        go(cp, wait)

    for wait in (False, True):
        lax.fori_loop(0, N_EXPERTS, lambda e, c, wait=wait: (fill(e, wait), c)[1], 0)
        lax.fori_loop(n_used_ref[0], n_blocks, lambda blk, c, wait=wait: (tail(blk, wait), c)[1], 0)


def _pad_fill_call(fill_start, fill_n, n_used, xs):
    grid_spec = pltpu.PrefetchScalarGridSpec(
        num_scalar_prefetch=3,
        grid=(1,),
        in_specs=[pl.BlockSpec(memory_space=pl.ANY)],
        out_specs=pl.BlockSpec(memory_space=pl.ANY),
        scratch_shapes=[pltpu.VMEM((EXPERT_ROWS, xs.shape[1]), xs.dtype), pltpu.SemaphoreType.DMA],
    )
    return pl.pallas_call(
        _pad_fill_kernel,
        grid_spec=grid_spec,
        out_shape=jax.ShapeDtypeStruct(xs.shape, xs.dtype),
        input_output_aliases={3: 0},
        compiler_params=pltpu.CompilerParams(dimension_semantics=("arbitrary",)),
        name="pad_fill",
    )(fill_start, fill_n, n_used, xs)


def _expert_kernel(be_ref, bv_ref, bi_ref, nx_ref, x_ref, wgu_hbm, bgu_ref, wd_hbm, bd_ref, y_ref,
                   wgu_st, wd_st, wgu_bf, wd_bf, h_s, wsem):
    del bi_ref
    i = pl.program_id(0)
    e = be_ref[i]
    e_prev = be_ref[jnp.maximum(i - 1, 0)]
    valid = bv_ref[i]

    def weight_copies(ex):
        return (pltpu.make_async_copy(wgu_hbm.at[ex], wgu_st, wsem.at[0]),
                pltpu.make_async_copy(wd_hbm.at[ex], wd_st, wsem.at[1]))

    @pl.when(i == 0)
    def _():
        for cp in weight_copies(e):
            cp.start()

    @pl.when(jnp.logical_and(valid > 0, jnp.logical_or(i == 0, e != e_prev)))
    def _():
        for cp in weight_copies(e):
            cp.wait()
        wgu_bf[...] = wgu_st[...].astype(BF16)
        wd_bf[...] = wd_st[...].astype(BF16)
        nxt = nx_ref[i]

        @pl.when(nxt >= 0)
        def _():
            for cp in weight_copies(nxt):
                cp.start()

    @pl.when(valid > 0)
    def _():
        x_lo, x_hi = _unpack_bf16_pairs(x_ref[...])
        xb = jnp.concatenate([x_lo.astype(BF16), x_hi.astype(BF16)], axis=1)
        for c in range(D_FF // (2 * MXU_DIM)):
            lo = c * 2 * MXU_DIM
            hi = lo + 2 * MXU_DIM
            g = _dot(xb, wgu_bf[:, lo:hi]) + bgu_ref[0, :, lo:hi]
            up = _dot(xb, wgu_bf[:, D_FF + lo:D_FF + hi]) + bgu_ref[0, :, D_FF + lo:D_FF + hi]
            g = jnp.minimum(g, SWIGLU_LIMIT)
            up = jnp.clip(up, -SWIGLU_LIMIT, SWIGLU_LIMIT)
            glu = g * jax.nn.sigmoid(g * SWIGLU_ALPHA)
            h_s[:, lo:hi] = ((up + 1.0) * glu).astype(BF16)
        y_ref[...] = _pack_bf16_pairs(_dot(h_s[...], wd_bf[...]) + bd_ref[0])

    @pl.when(valid == 0)
    def _():
        y_ref[...] = jnp.zeros(y_ref.shape, y_ref.dtype)


def _expert_call(block_e, block_valid, block_idx, block_next, xs, w_gu, b_gu, w_down, b_down):
    n_pad, dw = xs.shape
    d = 2 * dw
    bm = EXPERT_ROWS
    nb = n_pad // bm
    grid_spec = pltpu.PrefetchScalarGridSpec(
        num_scalar_prefetch=4,
        grid=(nb,),
        in_specs=[
            pl.BlockSpec((bm, dw), lambda i, be, bv, bi, nx: (bi[i], 0)),
            pl.BlockSpec(memory_space=pl.ANY),
            pl.BlockSpec((1, 1, 2 * D_FF), lambda i, be, bv, bi, nx: (be[i], 0, 0)),
            pl.BlockSpec(memory_space=pl.ANY),
            pl.BlockSpec((1, 1, d), lambda i, be, bv, bi, nx: (be[i], 0, 0)),
        ],
        out_specs=pl.BlockSpec((bm, dw), lambda i, be, bv, bi, nx: (i, 0)),
        scratch_shapes=[
            pltpu.VMEM((d, 2 * D_FF), F32),
            pltpu.VMEM((D_FF, d), F32),
            pltpu.VMEM((d, 2 * D_FF), BF16),
            pltpu.VMEM((D_FF, d), BF16),
            pltpu.VMEM((bm, D_FF), BF16),
            pltpu.SemaphoreType.DMA((2,)),
        ],
    )
    return pl.pallas_call(
        _expert_kernel,
        grid_spec=grid_spec,
        out_shape=jax.ShapeDtypeStruct((n_pad, dw), jnp.uint32),
        compiler_params=pltpu.CompilerParams(
            dimension_semantics=("arbitrary",), vmem_limit_bytes=VMEM_LIMIT_BYTES),
        name="expert_ffn",
    )(block_e, block_valid, block_idx, block_next, xs, w_gu, b_gu, w_down, b_down)


def _combine_kernel(x1_ref, gate_ref, y4_ref, g_ref, out_ref):
    x1 = x1_ref[...]
    n = x1.shape[1] // 2
    gate = gate_ref[...]
    acc_lo, acc_hi = x1[:, :n], x1[:, n:]
    for k in range(TOP_K):
        y_lo, y_hi = _unpack_bf16_pairs(y4_ref[k])
        acc_lo = acc_lo + gate[:, k:k + 1] * y_lo
        acc_hi = acc_hi + gate[:, k:k + 1] * y_hi
    out_ref[...] = _rms(jnp.concatenate([acc_lo, acc_hi], axis=1), g_ref[...])


def _combine_call(x1, gate_t, y4, g_final):
    t, d = x1.shape
    tt = COMBINE_ROWS
    return pl.pallas_call(
        _combine_kernel,
        grid=(t // tt,),
        in_specs=[
            pl.BlockSpec((tt, d), lambda i: (i, 0)),
            pl.BlockSpec((tt, TOP_K), lambda i: (i, 0)),
            pl.BlockSpec((TOP_K, tt, d // 2), lambda i: (0, i, 0)),
            pl.BlockSpec((1, d), lambda i: (0, 0)),
        ],
        out_specs=pl.BlockSpec((tt, d), lambda i: (i, 0)),
        out_shape=jax.ShapeDtypeStruct((t, d), F32),
        compiler_params=pltpu.CompilerParams(dimension_semantics=("arbitrary",)),
        name="combine_norm",
    )(x1, gate_t, y4, g_final)


def _block_diag(w):
    hh, n, _ = w.shape
    eye = jnp.eye(hh, dtype=w.dtype)
    return jnp.einsum("hij,hg->higj", w, eye).reshape(hh * n, hh * n)


def kernel(x, mem, norm_mix_g, w_in, conv_w, conv_b, w_rg_a, b_rg_a, w_rg_i, b_rg_i, lru_lambda, w_pool,
           pool_scale, mem_norm_g, w_mem_kv, w_out, norm_ffn_g, w_router, b_router, w_gu, b_gu, w_down,
           b_down, final_norm_g):
    b, s, d = x.shape
    t = b * s
    l = 0
    row = lambda v: v.reshape(1, -1)

    kv = _kv_call(mem, row(mem_norm_g[l]), w_mem_kv[l].astype(BF16))
    kh = kv[..., :D_XATTN].reshape(b, N_MEM, XATTN_HEADS, XATTN_HEAD_DIM)
    vh = kv[..., D_XATTN:].reshape(b, N_MEM, XATTN_HEADS, XATTN_HEAD_DIM)
    eye_h = jnp.eye(XATTN_HEADS, dtype=F32)
    kbd = jnp.einsum("bmhd,hg->bhdgm", kh, eye_h).reshape(b, D_XATTN, XATTN_HEADS * N_MEM).astype(BF16)
    vbd = jnp.einsum("bmhd,hg->bgmhd", vh, eye_h).reshape(b, XATTN_HEADS * N_MEM, D_XATTN).astype(BF16)

    heads_per = MXU_DIM // RNN_HEAD_DIM
    wg = jnp.stack([
        jnp.concatenate([_block_diag(w_rg_a[l, c * heads_per:(c + 1) * heads_per]),
                         _block_diag(w_rg_i[l, c * heads_per:(c + 1) * heads_per])], axis=1)
        for c in range(D_RNN // MXU_DIM)]).astype(BF16)

    x1, hn, idx, gate = _mix_call(
        x, row(norm_mix_g[l]), w_in[l].astype(BF16), conv_w[l], row(conv_b[l]), wg, row(b_rg_a[l]),
        row(b_rg_i[l]), row(lru_lambda[l]), _block_diag(w_pool[l]).astype(BF16), row(pool_scale[l]), kbd, vbd,
        w_out[l].astype(BF16), row(norm_ffn_g[l]), w_router[l].T.astype(BF16), b_router[l].reshape(-1, 1))

    n_blocks = _num_blocks(t)
    dest, meta = _route_call(idx)
    be, bv, bi, nx = meta[0, :n_blocks], meta[1, :n_blocks], meta[2, :n_blocks], meta[6, :n_blocks]
    fill_start, fill_n, n_used = meta[3, :N_EXPERTS], meta[4, :N_EXPERTS], meta[5, :1]

    xs = _sc_dispatch_call(hn, dest, n_blocks * EXPERT_ROWS)
    xs = _pad_fill_call(fill_start, fill_n, n_used, xs)
    ys = _expert_call(be, bv, bi, nx, xs, w_gu[l], b_gu[l].reshape(N_EXPERTS, 1, -1), w_down[l],
                      b_down[l].reshape(N_EXPERTS, 1, -1))
    y4 = _sc_gather_call(ys, dest)
    out = _combine_call(x1.reshape(t, d), gate.T, y4, row(final_norm_g))
    return out.reshape(b, s, d)
```

```python
import functools
import math

import jax
import jax.numpy as jnp
from jax import lax
from jax.experimental import pallas as pl
from jax.experimental.pallas import tpu as pltpu
from jax.experimental.pallas import tpu_sc as plsc

D_MODEL = 1024
N_MEM = 256
D_RNN = 512
RNN_HEADS = 8
RNN_HEAD_DIM = D_RNN // RNN_HEADS
CONV_WIDTH = 4
LRU_C = 8.0
D_POOL = 256
POOL_WINDOWS = (2, 4, 8, 16)
POOL_GROUP_DIM = D_POOL // len(POOL_WINDOWS)
MAX_WINDOW = max(POOL_WINDOWS)
D_XATTN = 256
XATTN_HEADS = 4
XATTN_HEAD_DIM = D_XATTN // XATTN_HEADS
D_IN = 2 * D_RNN + D_POOL + D_XATTN
N_EXPERTS = 32
TOP_K = 4
D_FF = D_MODEL
SWIGLU_ALPHA = 1.702
SWIGLU_LIMIT = 7.0
RMS_EPS = 1e-6

SUBLANES = 8
MXU_DIM = 256
VMEM_LIMIT_BYTES = 56 * 1024 * 1024

MIX_ROWS = 512
HIST = 16
EXPERT_ROWS = 528
COMBINE_ROWS = 512
ROUTE_CHUNK = 1024
SC_ROWS = 128
TOKEN_PARTS = 4

BF16 = jnp.bfloat16
F32 = jnp.float32


def _rms(xf, g):
    return xf * lax.rsqrt(jnp.mean(xf * xf, axis=-1, keepdims=True) + RMS_EPS) * g


def _dot(a, b):
    return jnp.dot(a, b, preferred_element_type=F32)


def _pack_bf16_pairs(v):
    n = v.shape[1] // 2
    bits = lax.bitcast_convert_type(v.astype(BF16).astype(F32), jnp.uint32)
    return (bits[:, :n] >> 16) | (bits[:, n:] & jnp.uint32(0xFFFF0000))


def _unpack_bf16_pairs(w):
    lo = lax.bitcast_convert_type(w << 16, F32)
    hi = lax.bitcast_convert_type(w & jnp.uint32(0xFFFF0000), F32)
    return lo, hi


def _kv_kernel(mem_ref, g_ref, w_ref, kv_ref):
    mn = _rms(mem_ref[0], g_ref[...])
    kv_ref[0] = _dot(mn.astype(BF16), w_ref[...])


def _kv_call(mem, g, w_kv_bf):
    b = mem.shape[0]
    return pl.pallas_call(
        _kv_kernel,
        grid=(b,),
        in_specs=[
            pl.BlockSpec((1, N_MEM, D_MODEL), lambda i: (i, 0, 0)),
            pl.BlockSpec((1, D_MODEL), lambda i: (0, 0)),
            pl.BlockSpec((D_MODEL, 2 * D_XATTN), lambda i: (0, 0)),
        ],
        out_specs=pl.BlockSpec((1, N_MEM, 2 * D_XATTN), lambda i: (i, 0, 0)),
        out_shape=jax.ShapeDtypeStruct((b, N_MEM, 2 * D_XATTN), F32),
        name="kv_proj",
    )(mem, g, w_kv_bf)


def _mix_kernel(x_ref, g_mix_ref, w_in_ref, conv_w_ref, conv_b_ref, wg_ref, b_a_ref, b_i_ref, lam_ref,
                w_pool_ref, pool_scale_ref, kbd_ref, vbd_ref, w_out_ref, g_ffn_ref, w_rt_ref, b_r_ref,
                x1_ref, hn_ref, idx_ref, gate_ref,
                urnn_buf, upool_buf, hcar, mixed_s):
    tm = MIX_ROWS
    s = pl.program_id(1)

    @pl.when(s == 0)
    def _():
        urnn_buf[0:HIST, :] = jnp.zeros((HIST, D_RNN), F32)
        upool_buf[0:HIST, :] = jnp.zeros((HIST, D_POOL), F32)
        hcar[...] = jnp.zeros((1, D_RNN), F32)

    x = x_ref[0]
    h = _rms(x, g_mix_ref[...])
    proj = _dot(h.astype(BF16), w_in_ref[...])
    g_rnn = proj[:, D_RNN:2 * D_RNN]
    q_mem = proj[:, 2 * D_RNN + D_POOL:]
    urnn_buf[HIST:, :] = proj[:, :D_RNN]
    upool_buf[HIST:, :] = proj[:, 2 * D_RNN:2 * D_RNN + D_POOL]

    row = lax.broadcasted_iota(jnp.int32, (tm, 1), 0)
    grow = row + s * tm

    e = upool_buf[...]
    s2 = e + pltpu.roll(e, 1, axis=0)
    s4 = s2 + pltpu.roll(s2, 2, axis=0)
    s8 = s4 + pltpu.roll(s4, 4, axis=0)
    s16 = s8 + pltpu.roll(s8, 8, axis=0)
    grp = lax.broadcasted_iota(jnp.int32, (tm, D_POOL), 1) // POOL_GROUP_DIM
    wsum = jnp.where(grp == 0, s2[HIST:], jnp.where(grp == 1, s4[HIST:], jnp.where(grp == 2, s8[HIST:], s16[HIST:])))
    win = jnp.where(grp == 0, 2.0, jnp.where(grp == 1, 4.0, jnp.where(grp == 2, 8.0, 16.0)))
    pos = (grow + 1).astype(F32)
    dpool = wsum / jnp.minimum(pos, win) - e[HIST:]
    y_pool = _dot(dpool.astype(BF16), w_pool_ref[...]) * pool_scale_ref[...]
    mixed_s[:, D_RNN:D_RNN + D_POOL] = y_pool.astype(BF16)

    sc = _dot(q_mem.astype(BF16), kbd_ref[0]) * (XATTN_HEAD_DIM ** -0.5)
    ps = []
    for hh in range(XATTN_HEADS):
        sh = sc[:, hh * N_MEM:(hh + 1) * N_MEM]
        ph = jnp.exp(sh - jnp.max(sh, axis=-1, keepdims=True))
        ps.append((ph * (1.0 / jnp.sum(ph, axis=-1, keepdims=True))).astype(BF16))
    y_mem = _dot(jnp.concatenate(ps, axis=1), vbd_ref[0])
    mixed_s[:, D_RNN + D_POOL:] = y_mem.astype(BF16)

    uc = conv_b_ref[...] + conv_w_ref[CONV_WIDTH - 1:CONV_WIDTH, :] * urnn_buf[pl.ds(HIST, tm), :]
    for k in range(CONV_WIDTH - 1):
        off = HIST - (CONV_WIDTH - 1) + k
        uc = uc + conv_w_ref[k:k + 1, :] * urnn_buf[pl.ds(off, tm), :]

    ucb = uc.astype(BF16)
    pre = [_dot(ucb[:, c * MXU_DIM:(c + 1) * MXU_DIM], wg_ref[c]) for c in range(D_RNN // MXU_DIM)]
    pre_a = jnp.concatenate([p[:, :MXU_DIM] for p in pre], axis=1)
    pre_i = jnp.concatenate([p[:, MXU_DIM:] for p in pre], axis=1)
    r = jax.nn.sigmoid(pre_a + b_a_ref[...])
    ig = jax.nn.sigmoid(pre_i + b_i_ref[...])
    lam = lam_ref[...]
    softplus_neg = jnp.maximum(-lam, 0.0) + jnp.log(1.0 + jnp.exp(-jnp.abs(lam)))
    log_a = (-LRU_C) * r * softplus_neg
    a = jnp.exp(log_a)
    mult = jnp.sqrt(jnp.maximum(1.0 - a * a, 0.0))
    mult = jnp.where(grow == 0, 1.0, mult)
    bx = mult * (ig * uc)

    ng = tm // SUBLANES
    a = a.reshape(ng, SUBLANES, D_RNN)
    bx = bx.reshape(ng, SUBLANES, D_RNN)
    row8 = lax.broadcasted_iota(jnp.int32, (ng, SUBLANES, D_RNN), 1)
    d = 1
    while d < SUBLANES:
        a_sh = pltpu.roll(a, d, axis=1)
        b_sh = pltpu.roll(bx, d, axis=1)
        m = row8 >= d
        bx = jnp.where(m, a * b_sh + bx, bx)
        a = jnp.where(m, a * a_sh, a)
        d *= 2
    carry = jnp.broadcast_to(hcar[...], (SUBLANES, D_RNN))
    hs = []
    for g in range(ng):
        hv = a[g] * carry + bx[g]
        hs.append(hv)
        carry = jnp.broadcast_to(hv[SUBLANES - 1:SUBLANES, :], (SUBLANES, D_RNN))
    hcar[...] = carry[0:1, :]
    y_rnn = jnp.concatenate(hs, axis=0) * jax.nn.gelu(g_rnn)
    mixed_s[:, :D_RNN] = y_rnn.astype(BF16)

    x1 = x + _dot(mixed_s[...], w_out_ref[...])
    x1_ref[0] = x1

    urnn_buf[0:HIST, :] = urnn_buf[tm:tm + HIST, :]
    upool_buf[0:HIST, :] = upool_buf[tm:tm + HIST, :]

    hn = _rms(x1, g_ffn_ref[...]).astype(BF16)
    hn_ref[...] = _pack_bf16_pairs(hn)
    logits = lax.dot_general(w_rt_ref[...], hn, (((1,), (1,)), ((), ())),
                             preferred_element_type=F32) + b_r_ref[...]
    eid = lax.broadcasted_iota(jnp.int32, (N_EXPERTS, tm), 0)
    vals, idxs = [], []
    for _ in range(TOP_K):
        mx = jnp.max(logits, axis=0, keepdims=True)
        ix = jnp.min(jnp.where(logits == mx, eid, N_EXPERTS), axis=0, keepdims=True)
        vals.append(mx)
        idxs.append(ix)
        logits = jnp.where(eid == ix, -jnp.inf, logits)
    ex = [jnp.exp(v - vals[0]) for v in vals]
    den = ex[0] + ex[1] + ex[2] + ex[3]
    idx_ref[...] = jnp.concatenate(idxs, axis=0)
    gate_ref[...] = jnp.concatenate([e_ / den for e_ in ex], axis=0)


def _mix_call(x, g_mix, w_in, conv_w, conv_b, wg, b_a, b_i, lam, w_pool, pool_scale, kbd, vbd, w_out,
              g_ffn, w_rt, b_r):
    b, s, d = x.shape
    tm = MIX_ROWS
    ns = s // tm
    const2 = lambda shape: pl.BlockSpec(shape, lambda bi, si: (0, 0))
    const3 = lambda shape: pl.BlockSpec(shape, lambda bi, si: (0, 0, 0))
    return pl.pallas_call(
        _mix_kernel,
        grid=(b, ns),
        in_specs=[
            pl.BlockSpec((1, tm, d), lambda bi, si: (bi, si, 0)),
            const2((1, d)),
            const2((d, D_IN)),
            const2((CONV_WIDTH, D_RNN)),
            const2((1, D_RNN)),
            const3((D_RNN // MXU_DIM, MXU_DIM, 2 * MXU_DIM)),
            const2((1, D_RNN)),
            const2((1, D_RNN)),
            const2((1, D_RNN)),
            const2((D_POOL, D_POOL)),
            const2((1, D_POOL)),
            pl.BlockSpec((1, D_XATTN, XATTN_HEADS * N_MEM), lambda bi, si: (bi, 0, 0)),
            pl.BlockSpec((1, XATTN_HEADS * N_MEM, D_XATTN), lambda bi, si: (bi, 0, 0)),
            const2((d, d)),
            const2((1, d)),
            const2((N_EXPERTS, d)),
            const2((N_EXPERTS, 1)),
        ],
        out_specs=[
            pl.BlockSpec((1, tm, d), lambda bi, si: (bi, si, 0)),
            pl.BlockSpec((tm, d // 2), lambda bi, si: (bi * ns + si, 0)),
            pl.BlockSpec((TOP_K, tm), lambda bi, si: (0, bi * ns + si)),
            pl.BlockSpec((TOP_K, tm), lambda bi, si: (0, bi * ns + si)),
        ],
        out_shape=[
            jax.ShapeDtypeStruct((b, s, d), F32),
            jax.ShapeDtypeStruct((b * s, d // 2), jnp.uint32),
            jax.ShapeDtypeStruct((TOP_K, b * s), jnp.int32),
            jax.ShapeDtypeStruct((TOP_K, b * s), F32),
        ],
        scratch_shapes=[
            pltpu.VMEM((HIST + tm, D_RNN), F32),
            pltpu.VMEM((HIST + tm, D_POOL), F32),
            pltpu.VMEM((1, D_RNN), F32),
            pltpu.VMEM((tm, d), BF16),
        ],
        compiler_params=pltpu.CompilerParams(
            dimension_semantics=("arbitrary", "arbitrary"), vmem_limit_bytes=VMEM_LIMIT_BYTES),
        name="mixer_router",
    )(x, g_mix, w_in, conv_w, conv_b, wg, b_a, b_i, lam, w_pool, pool_scale, kbd, vbd, w_out, g_ffn, w_rt, b_r)


def _num_blocks(t):
    bm = EXPERT_ROWS
    return (t * TOP_K + N_EXPERTS * (bm - 1) + bm - 1) // bm


def _route_kernel(idx_ref, dest_ref, meta_ref, rank_s, carry_s, start_s, tri_s):
    p = pl.program_id(0)
    c = pl.program_id(1)
    nc = pl.num_programs(1)
    ch = ROUTE_CHUNK
    bm = float(EXPERT_ROWS)
    nbl = meta_ref.shape[1]
    eid = lax.broadcasted_iota(jnp.int32, (N_EXPERTS, ch), 0)
    idxc = idx_ref[...]

    @pl.when(jnp.logical_and(p == 0, c == 0))
    def _():
        carry_s[...] = jnp.zeros(carry_s.shape, F32)
        tri_s[...] = (lax.broadcasted_iota(jnp.int32, (ch, ch), 0)
                      < lax.broadcasted_iota(jnp.int32, (ch, ch), 1)).astype(BF16)

    @pl.when(p == 0)
    def _():
        sel = jnp.zeros((N_EXPERTS, ch), F32)
        for k in range(TOP_K):
            sel = sel + (idxc[k:k + 1, :] == eid).astype(F32)
        rank_s[c] = _dot(sel.astype(BF16), tri_s[...]) + carry_s[:, 0:1]
        carry_s[...] = carry_s[...] + jnp.sum(sel, axis=1, keepdims=True)

    @pl.when(jnp.logical_and(p == 0, c == nc - 1))
    def _():
        counts = carry_s[...]

        def div_bm(v):
            q = jnp.floor(v * (1.0 / bm))
            return q + jnp.where((q + 1.0) * bm <= v, 1.0, 0.0) - jnp.where(q * bm > v, 1.0, 0.0)

        padded = div_bm(counts + (bm - 1.0)) * bm
        e128 = lax.broadcasted_iota(jnp.int32, counts.shape, 0)
        pad_end = padded
        sh = 1
        while sh < N_EXPERTS:
            pad_end = pad_end + jnp.where(e128 >= sh, pltpu.roll(pad_end, sh, axis=0), 0.0)
            sh *= 2
        pad_start = pad_end - padded
        start_s[...] = pad_start
        total = pad_end[N_EXPERTS - 1:N_EXPERTS, 0:1]
        lane = lax.broadcasted_iota(jnp.int32, (1, nbl), 1)
        lane_f = lane.astype(F32)
        n_used = div_bm(total)
        bclamp = jnp.minimum(lane_f, n_used - 1.0)
        bstart = bclamp * bm
        pe, ps, cn, pd = pad_end[:, 0:1], pad_start[:, 0:1], counts[:, 0:1], padded[:, 0:1]
        be = jnp.minimum(jnp.sum((pe <= bstart).astype(F32), axis=0, keepdims=True), N_EXPERTS - 1.0)
        esub = lax.broadcasted_iota(jnp.int32, (N_EXPERTS, nbl), 0)
        onehot = esub.astype(F32) == be
        bv = jnp.sum(jnp.where(onehot, cn - (bstart - ps), 0.0), axis=0, keepdims=True)
        bv = jnp.where(lane_f * bm < total, jnp.clip(bv, 0.0, bm), 0.0)
        own = esub == lane
        fill_start = jnp.sum(jnp.where(own, ps + cn, 0.0), axis=0, keepdims=True)
        fill_n = jnp.sum(jnp.where(own, pd - cn, 0.0), axis=0, keepdims=True)
        later = jnp.logical_and(esub.astype(F32) > be, cn > 0.0)
        nxt = jnp.min(jnp.where(later, esub.astype(F32), float(N_EXPERTS)), axis=0, keepdims=True)
        nxt = jnp.where(nxt == float(N_EXPERTS), -1.0, nxt)
        zero = jnp.zeros((1, nbl), F32)
        meta_ref[...] = jnp.concatenate(
            [be, bv, bclamp, fill_start, fill_n, jnp.broadcast_to(n_used, (1, nbl)), nxt, zero],
            axis=0).astype(jnp.int32)

    @pl.when(p == 1)
    def _():
        base = start_s[:, 0:1] + rank_s[c]
        rows = [jnp.sum(jnp.where(idxc[k:k + 1, :] == eid, base, 0.0), axis=0, keepdims=True)
                for k in range(TOP_K)]
        dest_ref[...] = jnp.concatenate(rows, axis=0).astype(jnp.int32)


def _route_call(idx):
    k, t = idx.shape
    ch = ROUTE_CHUNK
    nc = t // ch
    nbl = -(-_num_blocks(t) // 128) * 128
    return pl.pallas_call(
        _route_kernel,
        grid=(2, nc),
        in_specs=[pl.BlockSpec((k, ch), lambda p, c: (0, c))],
        out_specs=[
            pl.BlockSpec((k, ch), lambda p, c: (0, c * p)),
            pl.BlockSpec((8, nbl), lambda p, c: (0, 0)),
        ],
        out_shape=[
            jax.ShapeDtypeStruct((k, t), jnp.int32),
            jax.ShapeDtypeStruct((8, nbl), jnp.int32),
        ],
        scratch_shapes=[
            pltpu.VMEM((nc, N_EXPERTS, ch), F32),
            pltpu.VMEM((N_EXPERTS, 128), F32),
            pltpu.VMEM((N_EXPERTS, 128), F32),
            pltpu.VMEM((ch, ch), BF16),
        ],
        compiler_params=pltpu.CompilerParams(dimension_semantics=("arbitrary", "arbitrary")),
        name="route_tables",
    )(idx)


def _sc_workers():
    info = plsc.get_sparse_core_info()
    return info.num_cores, info.num_cores * info.num_subcores


def _sc_dispatch_call(hn, dest, n_rows):
    t, w = hn.shape
    nc, nw = _sc_workers()
    per_w = t // nw
    ch = SC_ROWS
    mesh = plsc.VectorSubcoreMesh(core_axis_name="c", subcore_axis_name="s")

    @functools.partial(
        pl.kernel, mesh=mesh,
        out_type=jax.ShapeDtypeStruct((n_rows, w), hn.dtype),
        scratch_types=[pltpu.VMEM((ch,), jnp.int32), pltpu.VMEM((ch, w), hn.dtype)],
    )
    def body(hn_hbm, dest_hbm, xs_hbm, idx_v, rows_v):
        wid = lax.axis_index("s") * nc + lax.axis_index("c")

        @pl.loop(0, per_w // ch)
        def _(ci):
            base = pl.multiple_of(wid * per_w + ci * ch, ch)
            pltpu.sync_copy(hn_hbm.at[pl.ds(base, ch)], rows_v)
            for k in range(TOP_K):
                pltpu.sync_copy(dest_hbm.at[k, pl.ds(base, ch)], idx_v)
                pltpu.sync_copy(rows_v, xs_hbm.at[idx_v])

    return body(hn, dest)


def _sc_gather_call(ys, dest, part):
    k_, t = dest.shape
    tp = t // TOKEN_PARTS
    w = ys.shape[1]
    nc, nw = _sc_workers()
    per_w = tp // nw
    ch = SC_ROWS
    mesh = plsc.VectorSubcoreMesh(core_axis_name="c", subcore_axis_name="s")

    @functools.partial(
        pl.kernel, mesh=mesh,
        out_type=jax.ShapeDtypeStruct((k_, tp, w), ys.dtype),
        scratch_types=[pltpu.VMEM((ch,), jnp.int32), pltpu.VMEM((ch, w), ys.dtype)],
    )
    def body(ys_hbm, dest_hbm, y4_hbm, idx_v, rows_v):
        wid = lax.axis_index("s") * nc + lax.axis_index("c")

        @pl.loop(0, per_w // ch)
        def _(ci):
            base = pl.multiple_of(wid * per_w + ci * ch, ch)
            for k in range(k_):
                pltpu.sync_copy(dest_hbm.at[k, pl.ds(part * tp + base, ch)], idx_v)
                pltpu.sync_copy(ys_hbm.at[idx_v], rows_v)
                pltpu.sync_copy(rows_v, y4_hbm.at[k, pl.ds(base, ch)])

    return body(ys, dest)


def _pad_fill_kernel(fill_start_ref, fill_n_ref, n_used_ref, xs_in_ref, xs_ref, zero_s, zsem):
    del xs_in_ref
    bm = EXPERT_ROWS
    n_blocks = xs_ref.shape[0] // bm
    zero_s[...] = jnp.zeros(zero_s.shape, zero_s.dtype)

    def go(cp, wait):
        if wait:
            cp.wait()
        else:
            cp.start()

    def fill(e, wait):
        n = fill_n_ref[e]
        st = fill_start_ref[e]
        head = n & (SUBLANES - 1)
        for j in range(SUBLANES - 1):
            cp = pltpu.make_async_copy(zero_s.at[pl.ds(0, 1), :], xs_ref.at[pl.ds(st + j, 1), :], zsem)
            pl.when(j < head)(functools.partial(go, cp, wait))
        st = pl.multiple_of(st + head, SUBLANES)
        bit = 1 << ((bm - 1).bit_length() - 1)
        while bit >= SUBLANES:
            cp = pltpu.make_async_copy(zero_s.at[pl.ds(0, bit), :], xs_ref.at[pl.ds(st, bit), :], zsem)
            pl.when((n & bit) != 0)(functools.partial(go, cp, wait))
            st = pl.multiple_of(st + (n & bit), SUBLANES)
            bit //= 2

    def tail(blk, wait):
        cp = pltpu.make_async_copy(zero_s, xs_ref.at[pl.ds(pl.multiple_of(blk * bm, bm), bm), :], zsem)
        go(cp, wait)

    for wait in (False, True):
        lax.fori_loop(0, N_EXPERTS, lambda e, c, wait=wait: (fill(e, wait), c)[1], 0)
        lax.fori_loop(n_used_ref[0], n_blocks, lambda blk, c, wait=wait: (tail(blk, wait), c)[1], 0)


def _pad_fill_call(fill_start, fill_n, n_used, xs):
    grid_spec = pltpu.PrefetchScalarGridSpec(
        num_scalar_prefetch=3,
        grid=(1,),
        in_specs=[pl.BlockSpec(memory_space=pl.ANY)],
        out_specs=pl.BlockSpec(memory_space=pl.ANY),
        scratch_shapes=[pltpu.VMEM((EXPERT_ROWS, xs.shape[1]), xs.dtype), pltpu.SemaphoreType.DMA],
    )
    return pl.pallas_call(
        _pad_fill_kernel,
        grid_spec=grid_spec,
        out_shape=jax.ShapeDtypeStruct(xs.shape, xs.dtype),
        input_output_aliases={3: 0},
        compiler_params=pltpu.CompilerParams(dimension_semantics=("arbitrary",)),
        name="pad_fill",
    )(fill_start, fill_n, n_used, xs)


def _expert_kernel(be_ref, bv_ref, bi_ref, nx_ref, x_ref, wgu_hbm, bgu_ref, wd_hbm, bd_ref, y_ref,
                   wgu_st, wd_st, wgu_bf, wd_bf, wsem):
    del bi_ref
    i = pl.program_id(0)
    e = be_ref[i]
    e_prev = be_ref[jnp.maximum(i - 1, 0)]
    valid = bv_ref[i]

    def weight_copies(ex):
        return (pltpu.make_async_copy(wgu_hbm.at[ex], wgu_st, wsem.at[0]),
                pltpu.make_async_copy(wd_hbm.at[ex], wd_st, wsem.at[1]))

    @pl.when(i == 0)
    def _():
        for cp in weight_copies(e):
            cp.start()

    @pl.when(jnp.logical_and(valid > 0, jnp.logical_or(i == 0, e != e_prev)))
    def _():
        for cp in weight_copies(e):
            cp.wait()
        wgu_bf[...] = wgu_st[...].astype(BF16)
        wd_bf[...] = wd_st[...].astype(BF16)
        nxt = nx_ref[i]

        @pl.when(nxt >= 0)
        def _():
            for cp in weight_copies(nxt):
                cp.start()

    @pl.when(valid > 0)
    def _():
        x_lo, x_hi = _unpack_bf16_pairs(x_ref[...])
        xb = jnp.concatenate([x_lo.astype(BF16), x_hi.astype(BF16)], axis=1)
        hs = []
        for c in range(D_FF // (2 * MXU_DIM)):
            lo = c * 2 * MXU_DIM
            hi = lo + 2 * MXU_DIM
            g = _dot(xb, wgu_bf[:, lo:hi]) + bgu_ref[0, :, lo:hi]
            up = _dot(xb, wgu_bf[:, D_FF + lo:D_FF + hi]) + bgu_ref[0, :, D_FF + lo:D_FF + hi]
            g = jnp.minimum(g, SWIGLU_LIMIT)
            up = jnp.clip(up, -SWIGLU_LIMIT, SWIGLU_LIMIT)
            glu = g * jax.nn.sigmoid(g * SWIGLU_ALPHA)
            hs.append(((up + 1.0) * glu).astype(BF16))
        y_ref[...] = _pack_bf16_pairs(_dot(jnp.concatenate(hs, axis=1), wd_bf[...]) + bd_ref[0])

    @pl.when(valid == 0)
    def _():
        y_ref[...] = jnp.zeros(y_ref.shape, y_ref.dtype)


def _expert_call(block_e, block_valid, block_idx, block_next, xs, w_gu, b_gu, w_down, b_down):
    n_pad, dw = xs.shape
    d = 2 * dw
    bm = EXPERT_ROWS
    nb = n_pad // bm
    grid_spec = pltpu.PrefetchScalarGridSpec(
        num_scalar_prefetch=4,
        grid=(nb,),
        in_specs=[
            pl.BlockSpec((bm, dw), lambda i, be, bv, bi, nx: (bi[i], 0)),
            pl.BlockSpec(memory_space=pl.ANY),
            pl.BlockSpec((1, 1, 2 * D_FF), lambda i, be, bv, bi, nx: (be[i], 0, 0)),
            pl.BlockSpec(memory_space=pl.ANY),
            pl.BlockSpec((1, 1, d), lambda i, be, bv, bi, nx: (be[i], 0, 0)),
        ],
        out_specs=pl.BlockSpec((bm, dw), lambda i, be, bv, bi, nx: (i, 0)),
        scratch_shapes=[
            pltpu.VMEM((d, 2 * D_FF), F32),
            pltpu.VMEM((D_FF, d), F32),
            pltpu.VMEM((d, 2 * D_FF), BF16),
            pltpu.VMEM((D_FF, d), BF16),
            pltpu.SemaphoreType.DMA((2,)),
        ],
    )
    return pl.pallas_call(
        _expert_kernel,
        grid_spec=grid_spec,
        out_shape=jax.ShapeDtypeStruct((n_pad, dw), jnp.uint32),
        compiler_params=pltpu.CompilerParams(
            dimension_semantics=("arbitrary",), vmem_limit_bytes=VMEM_LIMIT_BYTES),
        name="expert_ffn",
    )(block_e, block_valid, block_idx, block_next, xs, w_gu, b_gu, w_down, b_down)


def _combine_kernel(x1_ref, gate_ref, y4_ref, g_ref, out_ref):
    x1 = x1_ref[...]
    n = x1.shape[1] // 2
    gate = gate_ref[...]
    acc_lo, acc_hi = x1[:, :n], x1[:, n:]
    for k in range(TOP_K):
        y_lo, y_hi = _unpack_bf16_pairs(y4_ref[k])
        acc_lo = acc_lo + gate[:, k:k + 1] * y_lo
        acc_hi = acc_hi + gate[:, k:k + 1] * y_hi
    out_ref[...] = _rms(jnp.concatenate([acc_lo, acc_hi], axis=1), g_ref[...])


def _combine_into_kernel(x1_ref, gate_ref, y4_ref, g_ref, prev_ref, out_ref):
    del prev_ref
    _combine_kernel(x1_ref, gate_ref, y4_ref, g_ref, out_ref)


def _combine_call(x1, gate_t, y4_part, g_final, part, out_prev):
    t, d = x1.shape
    tt = COMBINE_ROWS
    nb = t // TOKEN_PARTS // tt
    off = part * nb
    in_specs = [
        pl.BlockSpec((tt, d), lambda i: (off + i, 0)),
        pl.BlockSpec((tt, TOP_K), lambda i: (off + i, 0)),
        pl.BlockSpec((TOP_K, tt, d // 2), lambda i: (0, i, 0)),
        pl.BlockSpec((1, d), lambda i: (0, 0)),
    ]
    args = [x1, gate_t, y4_part, g_final]
    body = _combine_kernel
    aliases = {}
    if out_prev is not None:
        in_specs.append(pl.BlockSpec(memory_space=pl.ANY))
        args.append(out_prev)
        aliases = {len(args) - 1: 0}
        body = _combine_into_kernel
    return pl.pallas_call(
        body,
        grid=(nb,),
        in_specs=in_specs,
        out_specs=pl.BlockSpec((tt, d), lambda i: (off + i, 0)),
        out_shape=jax.ShapeDtypeStruct((t, d), F32),
        input_output_aliases=aliases,
        compiler_params=pltpu.CompilerParams(dimension_semantics=("arbitrary",)),
        name="combine_norm",
    )(*args)


def _block_diag(w):
    hh, n, _ = w.shape
    eye = jnp.eye(hh, dtype=w.dtype)
    return jnp.einsum("hij,hg->higj", w, eye).reshape(hh * n, hh * n)


def kernel(x, mem, norm_mix_g, w_in, conv_w, conv_b, w_rg_a, b_rg_a, w_rg_i, b_rg_i, lru_lambda, w_pool,
           pool_scale, mem_norm_g, w_mem_kv, w_out, norm_ffn_g, w_router, b_router, w_gu, b_gu, w_down,
           b_down, final_norm_g):
    b, s, d = x.shape
    t = b * s
    l = 0
    row = lambda v: v.reshape(1, -1)

    kv = _kv_call(mem, row(mem_norm_g[l]), w_mem_kv[l].astype(BF16))
    kh = kv[..., :D_XATTN].reshape(b, N_MEM, XATTN_HEADS, XATTN_HEAD_DIM)
    vh = kv[..., D_XATTN:].reshape(b, N_MEM, XATTN_HEADS, XATTN_HEAD_DIM)
    eye_h = jnp.eye(XATTN_HEADS, dtype=F32)
    kbd = jnp.einsum("bmhd,hg->bhdgm", kh, eye_h).reshape(b, D_XATTN, XATTN_HEADS * N_MEM).astype(BF16)
    vbd = jnp.einsum("bmhd,hg->bgmhd", vh, eye_h).reshape(b, XATTN_HEADS * N_MEM, D_XATTN).astype(BF16)

    heads_per = MXU_DIM // RNN_HEAD_DIM
    wg = jnp.stack([
        jnp.concatenate([_block_diag(w_rg_a[l, c * heads_per:(c + 1) * heads_per]),
                         _block_diag(w_rg_i[l, c * heads_per:(c + 1) * heads_per])], axis=1)
        for c in range(D_RNN // MXU_DIM)]).astype(BF16)

    x1, hn, idx, gate = _mix_call(
        x, row(norm_mix_g[l]), w_in[l].astype(BF16), conv_w[l], row(conv_b[l]), wg, row(b_rg_a[l]),
        row(b_rg_i[l]), row(lru_lambda[l]), _block_diag(w_pool[l]).astype(BF16), row(pool_scale[l]), kbd, vbd,
        w_out[l].astype(BF16), row(norm_ffn_g[l]), w_router[l].T.astype(BF16), b_router[l].reshape(-1, 1))

    n_blocks = _num_blocks(t)
    dest, meta = _route_call(idx)
    be, bv, bi, nx = meta[0, :n_blocks], meta[1, :n_blocks], meta[2, :n_blocks], meta[6, :n_blocks]
    fill_start, fill_n, n_used = meta[3, :N_EXPERTS], meta[4, :N_EXPERTS], meta[5, :1]

    xs = _sc_dispatch_call(hn, dest, n_blocks * EXPERT_ROWS)
    xs = _pad_fill_call(fill_start, fill_n, n_used, xs)
    ys = _expert_call(be, bv, bi, nx, xs, w_gu[l], b_gu[l].reshape(N_EXPERTS, 1, -1), w_down[l],
                      b_down[l].reshape(N_EXPERTS, 1, -1))
    x1 = x1.reshape(t, d)
    gate_t = gate.T
    out = None
    for part in range(TOKEN_PARTS):
        y4 = _sc_gather_call(ys, dest, part)
        out = _combine_call(x1, gate_t, y4, row(final_norm_g), part, out)
    return out.reshape(b, s, d)
```

```python
import functools
import math

import jax
import jax.numpy as jnp
from jax import lax
from jax.experimental import pallas as pl
from jax.experimental.pallas import tpu as pltpu
from jax.experimental.pallas import tpu_sc as plsc

D_MODEL = 1024
N_MEM = 256
D_RNN = 512
RNN_HEADS = 8
RNN_HEAD_DIM = D_RNN // RNN_HEADS
CONV_WIDTH = 4
LRU_C = 8.0
D_POOL = 256
POOL_WINDOWS = (2, 4, 8, 16)
POOL_GROUP_DIM = D_POOL // len(POOL_WINDOWS)
MAX_WINDOW = max(POOL_WINDOWS)
D_XATTN = 256
XATTN_HEADS = 4
XATTN_HEAD_DIM = D_XATTN // XATTN_HEADS
D_IN = 2 * D_RNN + D_POOL + D_XATTN
N_EXPERTS = 32
TOP_K = 4
D_FF = D_MODEL
SWIGLU_ALPHA = 1.702
SWIGLU_LIMIT = 7.0
RMS_EPS = 1e-6

SUBLANES = 8
MXU_DIM = 256
VMEM_LIMIT_BYTES = 56 * 1024 * 1024

MIX_ROWS = 512
HIST = 16
EXPERT_ROWS = 528
COMBINE_ROWS = 512
ROUTE_CHUNK = 1024
SC_ROWS = 128
TOKEN_PARTS = 1
MIX_PHASE_STAGES = (1 + D_IN // MXU_DIM, XATTN_HEADS + 10, D_MODEL // MXU_DIM + 2)

BF16 = jnp.bfloat16
F32 = jnp.float32


def _rms(xf, g):
    return xf * lax.rsqrt(jnp.mean(xf * xf, axis=-1, keepdims=True) + RMS_EPS) * g


def _dot(a, b):
    return jnp.dot(a, b, preferred_element_type=F32)


def _pack_bf16_pairs(v):
    n = v.shape[1] // 2
    bits = lax.bitcast_convert_type(v.astype(BF16).astype(F32), jnp.uint32)
    return (bits[:, :n] >> 16) | (bits[:, n:] & jnp.uint32(0xFFFF0000))


def _unpack_bf16_pairs(w):
    lo = lax.bitcast_convert_type(w << 16, F32)
    hi = lax.bitcast_convert_type(w & jnp.uint32(0xFFFF0000), F32)
    return lo, hi


def _kv_kernel(mem_ref, g_ref, w_ref, kv_ref):
    mn = _rms(mem_ref[0], g_ref[...])
    kv_ref[0] = _dot(mn.astype(BF16), w_ref[...])


def _kv_call(mem, g, w_kv_bf):
    b = mem.shape[0]
    return pl.pallas_call(
        _kv_kernel,
        grid=(b,),
        in_specs=[
            pl.BlockSpec((1, N_MEM, D_MODEL), lambda i: (i, 0, 0)),
            pl.BlockSpec((1, D_MODEL), lambda i: (0, 0)),
            pl.BlockSpec((D_MODEL, 2 * D_XATTN), lambda i: (0, 0)),
        ],
        out_specs=pl.BlockSpec((1, N_MEM, 2 * D_XATTN), lambda i: (i, 0, 0)),
        out_shape=jax.ShapeDtypeStruct((b, N_MEM, 2 * D_XATTN), F32),
        name="kv_proj",
    )(mem, g, w_kv_bf)


def _mix_kernel(x_ref, g_mix_ref, w_in_ref, conv_w_ref, conv_b_ref, wg_ref, b_a_ref, b_i_ref, lam_ref,
                w_pool_ref, pool_scale_ref, kbd_ref, vbd_ref, w_out_ref, g_ffn_ref, w_rt_ref, b_r_ref,
                x1_ref, hn_ref, idx_ref, gate_ref,
                urnn_buf, upool_buf, hcar, mixed_s):
    tiles = [
        _mix_tile(x_ref.at[b], g_mix_ref, w_in_ref, conv_w_ref, conv_b_ref, wg_ref, b_a_ref, b_i_ref, lam_ref,
                  w_pool_ref, pool_scale_ref, kbd_ref.at[b], vbd_ref.at[b], w_out_ref, g_ffn_ref, w_rt_ref,
                  b_r_ref, x1_ref.at[b], hn_ref.at[b], idx_ref.at[b], gate_ref.at[b],
                  urnn_buf.at[b], upool_buf.at[b], hcar.at[b], mixed_s.at[b])
        for b in range(x_ref.shape[0])]
    n_phase = len(MIX_PHASE_STAGES)
    for step in range(len(tiles) + n_phase - 1):
        active = [(tiles[j], MIX_PHASE_STAGES[step - j]) for j in range(len(tiles)) if 0 <= step - j < n_phase]
        ticks = max(n for _, n in active)
        done = [0] * len(active)
        for i in range(ticks):
            for a, (tile, n) in enumerate(active):
                while done[a] * ticks < (i + 1) * n:
                    next(tile, None)
                    done[a] += 1
    done_mark = object()
    assert all(next(tile, done_mark) is done_mark for tile in tiles), "MIX_PHASE_STAGES does not cover _mix_tile"


def _mix_tile(x_ref, g_mix_ref, w_in_ref, conv_w_ref, conv_b_ref, wg_ref, b_a_ref, b_i_ref, lam_ref,
              w_pool_ref, pool_scale_ref, kbd_ref, vbd_ref, w_out_ref, g_ffn_ref, w_rt_ref, b_r_ref,
              x1_ref, hn_ref, idx_ref, gate_ref,
              urnn_buf, upool_buf, hcar, mixed_s):
    tm = MIX_ROWS
    cw = MXU_DIM
    s = pl.program_id(0)

    @pl.when(s == 0)
    def _():
        urnn_buf[0:HIST, :] = jnp.zeros((HIST, D_RNN), F32)
        upool_buf[0:HIST, :] = jnp.zeros((HIST, D_POOL), F32)
        hcar[...] = jnp.zeros((1, D_RNN), F32)

    x = x_ref[...]
    hb = _rms(x, g_mix_ref[...]).astype(BF16)
    yield
    proj = []
    for c in range(D_IN // cw):
        proj.append(_dot(hb, w_in_ref[:, c * cw:(c + 1) * cw]))
        yield
    urnn_buf[HIST:, :] = jnp.concatenate(proj[:D_RNN // cw], axis=1)
    g_rnn = jnp.concatenate(proj[D_RNN // cw:2 * D_RNN // cw], axis=1)
    upool_buf[HIST:, :] = proj[2 * D_RNN // cw]
    q_mem = proj[2 * D_RNN // cw + 1]

    row = lax.broadcasted_iota(jnp.int32, (tm, 1), 0)
    grow = row + s * tm

    e = upool_buf[...]
    s2 = e + pltpu.roll(e, 1, axis=0)
    s4 = s2 + pltpu.roll(s2, 2, axis=0)
    s8 = s4 + pltpu.roll(s4, 4, axis=0)
    s16 = s8 + pltpu.roll(s8, 8, axis=0)
    grp = lax.broadcasted_iota(jnp.int32, (tm, D_POOL), 1) // POOL_GROUP_DIM
    wsum = jnp.where(grp == 0, s2[HIST:], jnp.where(grp == 1, s4[HIST:], jnp.where(grp == 2, s8[HIST:], s16[HIST:])))
    win = jnp.where(grp == 0, 2.0, jnp.where(grp == 1, 4.0, jnp.where(grp == 2, 8.0, 16.0)))
    pos = (grow + 1).astype(F32)
    dpool = wsum / jnp.minimum(pos, win) - e[HIST:]
    y_pool = _dot(dpool.astype(BF16), w_pool_ref[...]) * pool_scale_ref[...]
    mixed_s[:, D_RNN:D_RNN + D_POOL] = y_pool.astype(BF16)
    yield

    sc = _dot(q_mem.astype(BF16), kbd_ref[...]) * (XATTN_HEAD_DIM ** -0.5)
    ps = []
    for hh in range(XATTN_HEADS):
        sh = sc[:, hh * N_MEM:(hh + 1) * N_MEM]
        ph = jnp.exp(sh - jnp.max(sh, axis=-1, keepdims=True))
        ps.append((ph * (1.0 / jnp.sum(ph, axis=-1, keepdims=True))).astype(BF16))
        yield
    y_mem = _dot(jnp.concatenate(ps, axis=1), vbd_ref[...])
    mixed_s[:, D_RNN + D_POOL:] = y_mem.astype(BF16)
    yield

    uc = conv_b_ref[...] + conv_w_ref[CONV_WIDTH - 1:CONV_WIDTH, :] * urnn_buf[pl.ds(HIST, tm), :]
    for k in range(CONV_WIDTH - 1):
        off = HIST - (CONV_WIDTH - 1) + k
        uc = uc + conv_w_ref[k:k + 1, :] * urnn_buf[pl.ds(off, tm), :]
    yield

    ucb = uc.astype(BF16)
    pre = [_dot(ucb[:, c * cw:(c + 1) * cw], wg_ref[c]) for c in range(D_RNN // cw)]
    pre_a = jnp.concatenate([p[:, :cw] for p in pre], axis=1)
    pre_i = jnp.concatenate([p[:, cw:] for p in pre], axis=1)
    r = jax.nn.sigmoid(pre_a + b_a_ref[...])
    ig = jax.nn.sigmoid(pre_i + b_i_ref[...])
    yield
    lam = lam_ref[...]
    softplus_neg = jnp.maximum(-lam, 0.0) + jnp.log(1.0 + jnp.exp(-jnp.abs(lam)))
    log_a = (-LRU_C) * r * softplus_neg
    a = jnp.exp(log_a)
    mult = jnp.sqrt(jnp.maximum(1.0 - a * a, 0.0))
    mult = jnp.where(grow == 0, 1.0, mult)
    bx = mult * (ig * uc)
    yield

    ng = tm // SUBLANES
    a = a.reshape(ng, SUBLANES, D_RNN)
    bx = bx.reshape(ng, SUBLANES, D_RNN)
    row8 = lax.broadcasted_iota(jnp.int32, (1, SUBLANES, D_RNN), 1)
    d = 1
    while d < SUBLANES:
        a_sh = pltpu.roll(a, d, axis=1)
        b_sh = pltpu.roll(bx, d, axis=1)
        m = row8 >= d
        bx = jnp.where(m, a * b_sh + bx, bx)
        a = jnp.where(m, a * a_sh, a)
        d *= 2
        yield
    carry = jnp.broadcast_to(hcar[...], (SUBLANES, D_RNN))
    hs = []
    for g in range(ng):
        hv = a[g] * carry + bx[g]
        hs.append(hv)
        carry = jnp.broadcast_to(hv[SUBLANES - 1:SUBLANES, :], (SUBLANES, D_RNN))
    hcar[...] = carry[0:1, :]
    yield
    y_rnn = jnp.concatenate(hs, axis=0) * jax.nn.gelu(g_rnn)
    mixed_s[:, :D_RNN] = y_rnn.astype(BF16)
    urnn_buf[0:HIST, :] = urnn_buf[tm:tm + HIST, :]
    upool_buf[0:HIST, :] = upool_buf[tm:tm + HIST, :]
    yield

    mixed = mixed_s[...]
    x1s = []
    for c in range(D_MODEL // cw):
        x1s.append(x[:, c * cw:(c + 1) * cw] + _dot(mixed, w_out_ref[:, c * cw:(c + 1) * cw]))
        yield
    x1 = jnp.concatenate(x1s, axis=1)
    x1_ref[...] = x1

    hn = _rms(x1, g_ffn_ref[...]).astype(BF16)
    hn_ref[...] = _pack_bf16_pairs(hn)
    yield
    logits =lax.dot_general(w_rt_ref[...], hn, (((1,), (1,)), ((), ())),
                             preferred_element_type=F32) + b_r_ref[...]
    eid = lax.broadcasted_iota(jnp.int32, (N_EXPERTS, tm), 0)
    vals, idxs = [], []
    for _ in range(TOP_K):
        mx = jnp.max(logits, axis=0, keepdims=True)
        ix = jnp.min(jnp.where(logits == mx, eid, N_EXPERTS), axis=0, keepdims=True)
        vals.append(mx)
        idxs.append(ix)
        logits = jnp.where(eid == ix, -jnp.inf, logits)
    ex = [jnp.exp(v - vals[0]) for v in vals]
    den = ex[0] + ex[1] + ex[2] + ex[3]
    idx_ref[...] = jnp.concatenate(idxs, axis=0)
    gate_ref[...] = jnp.concatenate([e_ / den for e_ in ex], axis=0)


def _mix_call(x, g_mix, w_in, conv_w, conv_b, wg, b_a, b_i, lam, w_pool, pool_scale, kbd, vbd, w_out,
              g_ffn, w_rt, b_r):
    b, s, d = x.shape
    tm = MIX_ROWS
    ns = s // tm
    const2 = lambda shape: pl.BlockSpec(shape, lambda si: (0, 0))
    const3 = lambda shape: pl.BlockSpec(shape, lambda si: (0, 0, 0))
    return pl.pallas_call(
        _mix_kernel,
        grid=(ns,),
        in_specs=[
            pl.BlockSpec((b, tm, d), lambda si: (0, si, 0)),
            const2((1, d)),
            const2((d, D_IN)),
            const2((CONV_WIDTH, D_RNN)),
            const2((1, D_RNN)),
            const3((D_RNN // MXU_DIM, MXU_DIM, 2 * MXU_DIM)),
            const2((1, D_RNN)),
            const2((1, D_RNN)),
            const2((1, D_RNN)),
            const2((D_POOL, D_POOL)),
            const2((1, D_POOL)),
            const3((b, D_XATTN, XATTN_HEADS * N_MEM)),
            const3((b, XATTN_HEADS * N_MEM, D_XATTN)),
            const2((d, d)),
            const2((1, d)),
            const2((N_EXPERTS, d)),
            const2((N_EXPERTS, 1)),
        ],
        out_specs=[
            pl.BlockSpec((b, tm, d), lambda si: (0, si, 0)),
            pl.BlockSpec((b, tm, d // 2), lambda si: (0, si, 0)),
            pl.BlockSpec((b, TOP_K, tm), lambda si: (0, 0, si)),
            pl.BlockSpec((b, TOP_K, tm), lambda si: (0, 0, si)),
        ],
        out_shape=[
            jax.ShapeDtypeStruct((b, s, d), F32),
            jax.ShapeDtypeStruct((b, s, d // 2), jnp.uint32),
            jax.ShapeDtypeStruct((b, TOP_K, s), jnp.int32),
            jax.ShapeDtypeStruct((b, TOP_K, s), F32),
        ],
        scratch_shapes=[
            pltpu.VMEM((b, HIST + tm, D_RNN), F32),
            pltpu.VMEM((b, HIST + tm, D_POOL), F32),
            pltpu.VMEM((b, 1, D_RNN), F32),
            pltpu.VMEM((b, tm, d), BF16),
        ],
        compiler_params=pltpu.CompilerParams(
            dimension_semantics=("arbitrary",), vmem_limit_bytes=VMEM_LIMIT_BYTES),
        name="mixer_router",
    )(x, g_mix, w_in, conv_w, conv_b, wg, b_a, b_i, lam, w_pool, pool_scale, kbd, vbd, w_out, g_ffn, w_rt, b_r)


def _num_blocks(t):
    bm = EXPERT_ROWS
    return (t * TOP_K + N_EXPERTS * (bm - 1) + bm - 1) // bm


def _route_kernel(idx_ref, dest_ref, meta_ref, rank_s, carry_s, start_s, tri_s):
    p = pl.program_id(0)
    c = pl.program_id(1)
    nc = pl.num_programs(1)
    ch = ROUTE_CHUNK
    bm = float(EXPERT_ROWS)
    nbl = meta_ref.shape[1]
    eid = lax.broadcasted_iota(jnp.int32, (N_EXPERTS, ch), 0)
    idxc = idx_ref[...]

    @pl.when(jnp.logical_and(p == 0, c == 0))
    def _():
        carry_s[...] = jnp.zeros(carry_s.shape, F32)
        tri_s[...] = (lax.broadcasted_iota(jnp.int32, (ch, ch), 0)
                      < lax.broadcasted_iota(jnp.int32, (ch, ch), 1)).astype(BF16)

    @pl.when(p == 0)
    def _():
        sel = jnp.zeros((N_EXPERTS, ch), F32)
        for k in range(TOP_K):
            sel = sel + (idxc[k:k + 1, :] == eid).astype(F32)
        rank_s[c] = _dot(sel.astype(BF16), tri_s[...]) + carry_s[:, 0:1]
        carry_s[...] = carry_s[...] + jnp.sum(sel, axis=1, keepdims=True)

    @pl.when(jnp.logical_and(p == 0, c == nc - 1))
    def _():
        counts = carry_s[...]

        def div_bm(v):
            q = jnp.floor(v * (1.0 / bm))
            return q + jnp.where((q + 1.0) * bm <= v, 1.0, 0.0) - jnp.where(q * bm > v, 1.0, 0.0)

        padded = div_bm(counts + (bm - 1.0)) * bm
        e128 = lax.broadcasted_iota(jnp.int32, counts.shape, 0)
        pad_end = padded
        sh = 1
        while sh < N_EXPERTS:
            pad_end = pad_end + jnp.where(e128 >= sh, pltpu.roll(pad_end, sh, axis=0), 0.0)
            sh *= 2
        pad_start = pad_end - padded
        start_s[...] = pad_start
        total = pad_end[N_EXPERTS - 1:N_EXPERTS, 0:1]
        lane = lax.broadcasted_iota(jnp.int32, (1, nbl), 1)
        lane_f = lane.astype(F32)
        n_used = div_bm(total)
        bclamp = jnp.minimum(lane_f, n_used - 1.0)
        bstart = bclamp * bm
        pe, ps, cn, pd = pad_end[:, 0:1], pad_start[:, 0:1], counts[:, 0:1], padded[:, 0:1]
        be = jnp.minimum(jnp.sum((pe <= bstart).astype(F32), axis=0, keepdims=True), N_EXPERTS - 1.0)
        esub = lax.broadcasted_iota(jnp.int32, (N_EXPERTS, nbl), 0)
        onehot = esub.astype(F32) == be
        bv = jnp.sum(jnp.where(onehot, cn - (bstart - ps), 0.0), axis=0, keepdims=True)
        bv = jnp.where(lane_f * bm < total, jnp.clip(bv, 0.0, bm), 0.0)
        own = esub == lane
        fill_start = jnp.sum(jnp.where(own, ps + cn, 0.0), axis=0, keepdims=True)
        fill_n = jnp.sum(jnp.where(own, pd - cn, 0.0), axis=0, keepdims=True)
        later = jnp.logical_and(esub.astype(F32) > be, cn > 0.0)
        nxt = jnp.min(jnp.where(later, esub.astype(F32), float(N_EXPERTS)), axis=0, keepdims=True)
        nxt = jnp.where(nxt == float(N_EXPERTS), -1.0, nxt)
        zero = jnp.zeros((1, nbl), F32)
        meta_ref[...] = jnp.concatenate(
            [be, bv, bclamp, fill_start, fill_n, jnp.broadcast_to(n_used, (1, nbl)), nxt, zero],
            axis=0).astype(jnp.int32)

    @pl.when(p == 1)
    def _():
        base = start_s[:, 0:1] + rank_s[c]
        rows = [jnp.sum(jnp.where(idxc[k:k + 1, :] == eid, base, 0.0), axis=0, keepdims=True)
                for k in range(TOP_K)]
        dest_ref[...] = jnp.concatenate(rows, axis=0).astype(jnp.int32)


def _route_call(idx):
    k, t = idx.shape
    ch = ROUTE_CHUNK
    nc = t // ch
    nbl = -(-_num_blocks(t) // 128) * 128
    return pl.pallas_call(
        _route_kernel,
        grid=(2, nc),
        in_specs=[pl.BlockSpec((k, ch), lambda p, c: (0, c))],
        out_specs=[
            pl.BlockSpec((k, ch), lambda p, c: (0, c * p)),
            pl.BlockSpec((8, nbl), lambda p, c: (0, 0)),
        ],
        out_shape=[
            jax.ShapeDtypeStruct((k, t), jnp.int32),
            jax.ShapeDtypeStruct((8, nbl), jnp.int32),
        ],
        scratch_shapes=[
            pltpu.VMEM((nc, N_EXPERTS, ch), F32),
            pltpu.VMEM((N_EXPERTS, 128), F32),
            pltpu.VMEM((N_EXPERTS, 128), F32),
            pltpu.VMEM((ch, ch), BF16),
        ],
        compiler_params=pltpu.CompilerParams(dimension_semantics=("arbitrary", "arbitrary")),
        name="route_tables",
    )(idx)


def _sc_workers():
    info = plsc.get_sparse_core_info()
    return info.num_cores, info.num_cores * info.num_subcores


def _sc_dispatch_call(hn, dest, n_rows):
    t, w = hn.shape
    nc, nw = _sc_workers()
    per_w = t // nw
    ch = SC_ROWS
    mesh = plsc.VectorSubcoreMesh(core_axis_name="c", subcore_axis_name="s")

    @functools.partial(
        pl.kernel, mesh=mesh,
        out_type=jax.ShapeDtypeStruct((n_rows, w), hn.dtype),
        scratch_types=[pltpu.VMEM((ch,), jnp.int32), pltpu.VMEM((ch, w), hn.dtype)],
    )
    def body(hn_hbm, dest_hbm, xs_hbm, idx_v, rows_v):
        wid = lax.axis_index("s") * nc + lax.axis_index("c")

        @pl.loop(0, per_w // ch)
        def _(ci):
            base = pl.multiple_of(wid * per_w + ci * ch, ch)
            pltpu.sync_copy(hn_hbm.at[pl.ds(base, ch)], rows_v)
            for k in range(TOP_K):
                pltpu.sync_copy(dest_hbm.at[k, pl.ds(base, ch)], idx_v)
                pltpu.sync_copy(rows_v, xs_hbm.at[idx_v])

    return body(hn, dest)


def _sc_gather_call(ys, dest, part):
    k_, t = dest.shape
    tp = t // TOKEN_PARTS
    w = ys.shape[1]
    nc, nw = _sc_workers()
    per_w = tp // nw
    ch = SC_ROWS
    mesh = plsc.VectorSubcoreMesh(core_axis_name="c", subcore_axis_name="s")

    @functools.partial(
        pl.kernel, mesh=mesh,
        out_type=jax.ShapeDtypeStruct((k_, tp, w), ys.dtype),
        scratch_types=[pltpu.VMEM((ch,), jnp.int32), pltpu.VMEM((ch, w), ys.dtype)],
    )
    def body(ys_hbm, dest_hbm, y4_hbm, idx_v, rows_v):
        wid = lax.axis_index("s") * nc + lax.axis_index("c")

        @pl.loop(0, per_w // ch)
        def _(ci):
            base = pl.multiple_of(wid * per_w + ci * ch, ch)
            for k in range(k_):
                pltpu.sync_copy(dest_hbm.at[k, pl.ds(part * tp + base, ch)], idx_v)
                pltpu.sync_copy(ys_hbm.at[idx_v], rows_v)
                pltpu.sync_copy(rows_v, y4_hbm.at[k, pl.ds(base, ch)])

    return body(ys, dest)


def _pad_fill_kernel(fill_start_ref, fill_n_ref, n_used_ref, xs_in_ref, xs_ref, zero_s, zsem):
    del xs_in_ref
    bm = EXPERT_ROWS
    n_blocks = xs_ref.shape[0] // bm
    zero_s[...] = jnp.zeros(zero_s.shape, zero_s.dtype)

    def go(cp, wait):
        if wait:
            cp.wait()
        else:
            cp.start()

    def fill(e, wait):
        n = fill_n_ref[e]
        st = fill_start_ref[e]
        head = n & (SUBLANES - 1)
        for j in range(SUBLANES - 1):
            cp = pltpu.make_async_copy(zero_s.at[pl.ds(0, 1), :], xs_ref.at[pl.ds(st + j, 1), :], zsem)
            pl.when(j < head)(functools.partial(go, cp, wait))
        st = pl.multiple_of(st + head, SUBLANES)
        bit = 1 << ((bm - 1).bit_length() - 1)
        while bit >= SUBLANES:
            cp = pltpu.make_async_copy(zero_s.at[pl.ds(0, bit), :], xs_ref.at[pl.ds(st, bit), :], zsem)
            pl.when((n & bit) != 0)(functools.partial(go, cp, wait))
            st = pl.multiple_of(st + (n & bit), SUBLANES)
            bit //= 2

    def tail(blk, wait):
        cp = pltpu.make_async_copy(zero_s, xs_ref.at[pl.ds(pl.multiple_of(blk * bm, bm), bm), :], zsem)
        go(cp, wait)

    for wait in (False, True):
        lax.fori_loop(0, N_EXPERTS, lambda e, c, wait=wait: (fill(e, wait), c)[1], 0)
        lax.fori_loop(n_used_ref[0], n_blocks, lambda blk, c, wait=wait: (tail(blk, wait), c)[1], 0)


def _pad_fill_call(fill_start, fill_n, n_used, xs):
    grid_spec = pltpu.PrefetchScalarGridSpec(
        num_scalar_prefetch=3,
        grid=(1,),
        in_specs=[pl.BlockSpec(memory_space=pl.ANY)],
        out_specs=pl.BlockSpec(memory_space=pl.ANY),
        scratch_shapes=[pltpu.VMEM((EXPERT_ROWS, xs.shape[1]), xs.dtype), pltpu.SemaphoreType.DMA],
    )
    return pl.pallas_call(
        _pad_fill_kernel,
        grid_spec=grid_spec,
        out_shape=jax.ShapeDtypeStruct(xs.shape, xs.dtype),
        input_output_aliases={3: 0},
        compiler_params=pltpu.CompilerParams(dimension_semantics=("arbitrary",)),
        name="pad_fill",
    )(fill_start, fill_n, n_used, xs)


def _expert_kernel(be_ref, bv_ref, bi_ref, nx_ref, x_ref, wgu_hbm, bgu_ref, wd_hbm, bd_ref, y_ref,
                   wgu_st, wd_st, wgu_bf, wd_bf, wsem):
    del bi_ref
    i = pl.program_id(0)
    e = be_ref[i]
    e_prev = be_ref[jnp.maximum(i - 1, 0)]
    valid = bv_ref[i]

    def weight_copies(ex):
        return (pltpu.make_async_copy(wgu_hbm.at[ex], wgu_st, wsem.at[0]),
                pltpu.make_async_copy(wd_hbm.at[ex], wd_st, wsem.at[1]))

    @pl.when(i == 0)
    def _():
        for cp in weight_copies(e):
            cp.start()

    @pl.when(jnp.logical_and(valid > 0, jnp.logical_or(i == 0, e != e_prev)))
    def _():
        for cp in weight_copies(e):
            cp.wait()
        wgu_bf[...] = wgu_st[...].astype(BF16)
        wd_bf[...] = wd_st[...].astype(BF16)
        nxt = nx_ref[i]

        @pl.when(nxt >= 0)
        def _():
            for cp in weight_copies(nxt):
                cp.start()

    @pl.when(valid > 0)
    def _():
        x_lo, x_hi = _unpack_bf16_pairs(x_ref[...])
        xb = jnp.concatenate([x_lo.astype(BF16), x_hi.astype(BF16)], axis=1)
        hs = []
        for c in range(D_FF // (2 * MXU_DIM)):
            lo = c * 2 * MXU_DIM
            hi = lo + 2 * MXU_DIM
            g = _dot(xb, wgu_bf[:, lo:hi]) + bgu_ref[0, :, lo:hi]
            up = _dot(xb, wgu_bf[:, D_FF + lo:D_FF + hi]) + bgu_ref[0, :, D_FF + lo:D_FF + hi]
            g = jnp.minimum(g, SWIGLU_LIMIT)
            up = jnp.clip(up, -SWIGLU_LIMIT, SWIGLU_LIMIT)
            glu = g * jax.nn.sigmoid(g * SWIGLU_ALPHA)
            hs.append(((up + 1.0) * glu).astype(BF16))
        y_ref[...] = _pack_bf16_pairs(_dot(jnp.concatenate(hs, axis=1), wd_bf[...]) + bd_ref[0])

    @pl.when(valid == 0)
    def _():
        y_ref[...] = jnp.zeros(y_ref.shape, y_ref.dtype)


def _expert_call(block_e, block_valid, block_idx, block_next, xs, w_gu, b_gu, w_down, b_down):
    n_pad, dw = xs.shape
    d = 2 * dw
    bm = EXPERT_ROWS
    nb = n_pad // bm
    grid_spec = pltpu.PrefetchScalarGridSpec(
        num_scalar_prefetch=4,
        grid=(nb,),
        in_specs=[
            pl.BlockSpec((bm, dw), lambda i, be, bv, bi, nx: (bi[i], 0)),
            pl.BlockSpec(memory_space=pl.ANY),
            pl.BlockSpec((1, 1, 2 * D_FF), lambda i, be, bv, bi, nx: (be[i], 0, 0)),
            pl.BlockSpec(memory_space=pl.ANY),
            pl.BlockSpec((1, 1, d), lambda i, be, bv, bi, nx: (be[i], 0, 0)),
        ],
        out_specs=pl.BlockSpec((bm, dw), lambda i, be, bv, bi, nx: (i, 0)),
        scratch_shapes=[
            pltpu.VMEM((d, 2 * D_FF), F32),
            pltpu.VMEM((D_FF, d), F32),
            pltpu.VMEM((d, 2 * D_FF), BF16),
            pltpu.VMEM((D_FF, d), BF16),
            pltpu.SemaphoreType.DMA((2,)),
        ],
    )
    return pl.pallas_call(
        _expert_kernel,
        grid_spec=grid_spec,
        out_shape=jax.ShapeDtypeStruct((n_pad, dw), jnp.uint32),
        compiler_params=pltpu.CompilerParams(
            dimension_semantics=("arbitrary",), vmem_limit_bytes=VMEM_LIMIT_BYTES),
        name="expert_ffn",
    )(block_e, block_valid, block_idx, block_next, xs, w_gu, b_gu, w_down, b_down)


def _combine_kernel(x1_ref, gate_ref, y4_ref, g_ref, out_ref):
    x1 = x1_ref[...]
    n = x1.shape[1] // 2
    gate = gate_ref[...]
    acc_lo, acc_hi = x1[:, :n], x1[:, n:]
    for k in range(TOP_K):
        y_lo, y_hi = _unpack_bf16_pairs(y4_ref[k])
        acc_lo = acc_lo + gate[:, k:k + 1] * y_lo
        acc_hi = acc_hi + gate[:, k:k + 1] * y_hi
    out_ref[...] = _rms(jnp.concatenate([acc_lo, acc_hi], axis=1), g_ref[...])


def _combine_into_kernel(x1_ref, gate_ref, y4_ref, g_ref, prev_ref, out_ref):
    del prev_ref
    _combine_kernel(x1_ref, gate_ref, y4_ref, g_ref, out_ref)


def _combine_call(x1, gate_t, y4_part, g_final, part, out_prev):
    t, d = x1.shape
    tt = COMBINE_ROWS
    nb = t // TOKEN_PARTS // tt
    off = part * nb
    in_specs = [
        pl.BlockSpec((tt, d), lambda i: (off + i, 0)),
        pl.BlockSpec((tt, TOP_K), lambda i: (off + i, 0)),
        pl.BlockSpec((TOP_K, tt, d // 2), lambda i: (0, i, 0)),
        pl.BlockSpec((1, d), lambda i: (0, 0)),
    ]
    args = [x1, gate_t, y4_part, g_final]
    body = _combine_kernel
    aliases = {}
    if out_prev is not None:
        in_specs.append(pl.BlockSpec(memory_space=pl.ANY))
        args.append(out_prev)
        aliases = {len(args) - 1: 0}
        body = _combine_into_kernel
    return pl.pallas_call(
        body,
        grid=(nb,),
        in_specs=in_specs,
        out_specs=pl.BlockSpec((tt, d), lambda i: (off + i, 0)),
        out_shape=jax.ShapeDtypeStruct((t, d), F32),
        input_output_aliases=aliases,
        compiler_params=pltpu.CompilerParams(dimension_semantics=("arbitrary",)),
        name="combine_norm",
    )(*args)


def _block_diag(w):
    hh, n, _ = w.shape
    eye = jnp.eye(hh, dtype=w.dtype)
    return jnp.einsum("hij,hg->higj", w, eye).reshape(hh * n, hh * n)


def kernel(x, mem, norm_mix_g, w_in, conv_w, conv_b, w_rg_a, b_rg_a, w_rg_i, b_rg_i, lru_lambda, w_pool,
           pool_scale, mem_norm_g, w_mem_kv, w_out, norm_ffn_g, w_router, b_router, w_gu, b_gu, w_down,
           b_down, final_norm_g):
    b, s, d = x.shape
    t = b * s
    l = 0
    row = lambda v: v.reshape(1, -1)

    kv = _kv_call(mem, row(mem_norm_g[l]), w_mem_kv[l].astype(BF16))
    kh = kv[..., :D_XATTN].reshape(b, N_MEM, XATTN_HEADS, XATTN_HEAD_DIM)
    vh = kv[..., D_XATTN:].reshape(b, N_MEM, XATTN_HEADS, XATTN_HEAD_DIM)
    eye_h = jnp.eye(XATTN_HEADS, dtype=F32)
    kbd = jnp.einsum("bmhd,hg->bhdgm", kh, eye_h).reshape(b, D_XATTN, XATTN_HEADS * N_MEM).astype(BF16)
    vbd = jnp.einsum("bmhd,hg->bgmhd", vh, eye_h).reshape(b, XATTN_HEADS * N_MEM, D_XATTN).astype(BF16)

    heads_per = MXU_DIM // RNN_HEAD_DIM
    wg = jnp.stack([
        jnp.concatenate([_block_diag(w_rg_a[l, c * heads_per:(c + 1) * heads_per]),
                         _block_diag(w_rg_i[l, c * heads_per:(c + 1) * heads_per])], axis=1)
        for c in range(D_RNN // MXU_DIM)]).astype(BF16)

    x1, hn, idx, gate = _mix_call(
        x, row(norm_mix_g[l]), w_in[l].astype(BF16), conv_w[l], row(conv_b[l]), wg, row(b_rg_a[l]),
        row(b_rg_i[l]), row(lru_lambda[l]), _block_diag(w_pool[l]).astype(BF16), row(pool_scale[l]), kbd, vbd,
        w_out[l].astype(BF16), row(norm_ffn_g[l]), w_router[l].T.astype(BF16), b_router[l].reshape(-1, 1))

    hn = hn.reshape(t, d // 2)
    idx = jnp.transpose(idx, (1, 0, 2)).reshape(TOP_K, t)
    gate = jnp.transpose(gate, (1, 0, 2)).reshape(TOP_K, t)
    n_blocks = _num_blocks(t)
    dest, meta = _route_call(idx)
    be, bv, bi, nx = meta[0, :n_blocks], meta[1, :n_blocks], meta[2, :n_blocks], meta[6, :n_blocks]
    fill_start, fill_n, n_used = meta[3, :N_EXPERTS], meta[4, :N_EXPERTS], meta[5, :1]

    xs = _sc_dispatch_call(hn, dest, n_blocks * EXPERT_ROWS)
    xs = _pad_fill_call(fill_start, fill_n, n_used, xs)
    ys = _expert_call(be, bv, bi, nx, xs, w_gu[l], b_gu[l].reshape(N_EXPERTS, 1, -1), w_down[l],
                      b_down[l].reshape(N_EXPERTS, 1, -1))
    x1 = x1.reshape(t, d)
    gate_t = gate.T
    out = None
    for part in range(TOKEN_PARTS):
        y4 = _sc_gather_call(ys, dest, part)
        out = _combine_call(x1, gate_t, y4, row(final_norm_g), part, out)
    return out.reshape(b, s, d)
```

```python
import functools
import math

import jax
import jax.numpy as jnp
from jax import lax
from jax.experimental import pallas as pl
from jax.experimental.pallas import tpu as pltpu
from jax.experimental.pallas import tpu_sc as plsc

D_MODEL = 1024
N_MEM = 256
D_RNN = 512
RNN_HEADS = 8
RNN_HEAD_DIM = D_RNN // RNN_HEADS
CONV_WIDTH = 4
LRU_C = 8.0
D_POOL = 256
POOL_WINDOWS = (2, 4, 8, 16)
POOL_GROUP_DIM = D_POOL // len(POOL_WINDOWS)
MAX_WINDOW = max(POOL_WINDOWS)
D_XATTN = 256
XATTN_HEADS = 4
XATTN_HEAD_DIM = D_XATTN // XATTN_HEADS
D_IN = 2 * D_RNN + D_POOL + D_XATTN
N_EXPERTS = 32
TOP_K = 4
D_FF = D_MODEL
SWIGLU_ALPHA = 1.702
SWIGLU_LIMIT = 7.0
RMS_EPS = 1e-6

SUBLANES = 8
MXU_DIM = 256
VMEM_LIMIT_BYTES = 56 * 1024 * 1024

MIX_ROWS = 512
HIST = 16
EXPERT_ROWS = 1072
COMBINE_ROWS = 512
ROUTE_CHUNK = 1024
SC_ROWS = 128
TOKEN_PARTS = 1
MIX_PHASE_STAGES = (1 + D_IN // MXU_DIM, XATTN_HEADS + 10, D_MODEL // MXU_DIM + 2)

BF16 = jnp.bfloat16
F32 = jnp.float32


def _rms(xf, g):
    return xf * lax.rsqrt(jnp.mean(xf * xf, axis=-1, keepdims=True) + RMS_EPS) * g


def _dot(a, b):
    return jnp.dot(a, b, preferred_element_type=F32)


def _pack_bf16_pairs(v):
    n = v.shape[1] // 2
    bits = lax.bitcast_convert_type(v.astype(BF16).astype(F32), jnp.uint32)
    return (bits[:, :n] >> 16) | (bits[:, n:] & jnp.uint32(0xFFFF0000))


def _unpack_bf16_pairs(w):
    lo = lax.bitcast_convert_type(w << 16, F32)
    hi = lax.bitcast_convert_type(w & jnp.uint32(0xFFFF0000), F32)
    return lo, hi


def _kv_kernel(mem_ref, g_ref, w_ref, kv_ref):
    mn = _rms(mem_ref[0], g_ref[...])
    kv_ref[0] = _dot(mn.astype(BF16), w_ref[...])


def _kv_call(mem, g, w_kv_bf):
    b = mem.shape[0]
    return pl.pallas_call(
        _kv_kernel,
        grid=(b,),
        in_specs=[
            pl.BlockSpec((1, N_MEM, D_MODEL), lambda i: (i, 0, 0)),
            pl.BlockSpec((1, D_MODEL), lambda i: (0, 0)),
            pl.BlockSpec((D_MODEL, 2 * D_XATTN), lambda i: (0, 0)),
        ],
        out_specs=pl.BlockSpec((1, N_MEM, 2 * D_XATTN), lambda i: (i, 0, 0)),
        out_shape=jax.ShapeDtypeStruct((b, N_MEM, 2 * D_XATTN), F32),
        name="kv_proj",
    )(mem, g, w_kv_bf)


def _mix_kernel(x_ref, g_mix_ref, w_in_ref, conv_w_ref, conv_b_ref, wg_ref, b_a_ref, b_i_ref, lam_ref,
                w_pool_ref, pool_scale_ref, kbd_ref, vbd_ref, w_out_ref, g_ffn_ref, w_rt_ref, b_r_ref,
                x1_ref, hn_ref, idx_ref, gate_ref,
                urnn_buf, upool_buf, hcar, mixed_s):
    tiles = [
        _mix_tile(x_ref.at[b], g_mix_ref, w_in_ref, conv_w_ref, conv_b_ref, wg_ref, b_a_ref, b_i_ref, lam_ref,
                  w_pool_ref, pool_scale_ref, kbd_ref.at[b], vbd_ref.at[b], w_out_ref, g_ffn_ref, w_rt_ref,
                  b_r_ref, x1_ref.at[b], hn_ref.at[b], idx_ref.at[b], gate_ref.at[b],
                  urnn_buf.at[b], upool_buf.at[b], hcar.at[b], mixed_s.at[b])
        for b in range(x_ref.shape[0])]
    n_phase = len(MIX_PHASE_STAGES)
    for step in range(len(tiles) + n_phase - 1):
        active = [(tiles[j], MIX_PHASE_STAGES[step - j]) for j in range(len(tiles)) if 0 <= step - j < n_phase]
        ticks = max(n for _, n in active)
        done = [0] * len(active)
        for i in range(ticks):
            for a, (tile, n) in enumerate(active):
                while done[a] * ticks < (i + 1) * n:
                    next(tile, None)
                    done[a] += 1
    done_mark = object()
    assert all(next(tile, done_mark) is done_mark for tile in tiles), "MIX_PHASE_STAGES does not cover _mix_tile"


def _mix_tile(x_ref, g_mix_ref, w_in_ref, conv_w_ref, conv_b_ref, wg_ref, b_a_ref, b_i_ref, lam_ref,
              w_pool_ref, pool_scale_ref, kbd_ref, vbd_ref, w_out_ref, g_ffn_ref, w_rt_ref, b_r_ref,
              x1_ref, hn_ref, idx_ref, gate_ref,
              urnn_buf, upool_buf, hcar, mixed_s):
    tm = MIX_ROWS
    cw = MXU_DIM
    s = pl.program_id(0)

    @pl.when(s == 0)
    def _():
        urnn_buf[0:HIST, :] = jnp.zeros((HIST, D_RNN), F32)
        upool_buf[0:HIST, :] = jnp.zeros((HIST, D_POOL), F32)
        hcar[...] = jnp.zeros((1, D_RNN), F32)

    x = x_ref[...]
    hb = _rms(x, g_mix_ref[...]).astype(BF16)
    yield
    proj = []
    for c in range(D_IN // cw):
        proj.append(_dot(hb, w_in_ref[:, c * cw:(c + 1) * cw]))
        yield
    urnn_buf[HIST:, :] = jnp.concatenate(proj[:D_RNN // cw], axis=1)
    g_rnn = jnp.concatenate(proj[D_RNN // cw:2 * D_RNN // cw], axis=1)
    upool_buf[HIST:, :] = proj[2 * D_RNN // cw]
    q_mem = proj[2 * D_RNN // cw + 1]

    row = lax.broadcasted_iota(jnp.int32, (tm, 1), 0)
    grow = row + s * tm

    e = upool_buf[...]
    s2 = e + pltpu.roll(e, 1, axis=0)
    s4 = s2 + pltpu.roll(s2, 2, axis=0)
    s8 = s4 + pltpu.roll(s4, 4, axis=0)
    s16 = s8 + pltpu.roll(s8, 8, axis=0)
    grp = lax.broadcasted_iota(jnp.int32, (tm, D_POOL), 1) // POOL_GROUP_DIM
    wsum = jnp.where(grp == 0, s2[HIST:], jnp.where(grp == 1, s4[HIST:], jnp.where(grp == 2, s8[HIST:], s16[HIST:])))
    win = jnp.where(grp == 0, 2.0, jnp.where(grp == 1, 4.0, jnp.where(grp == 2, 8.0, 16.0)))
    pos = (grow + 1).astype(F32)
    dpool = wsum / jnp.minimum(pos, win) - e[HIST:]
    y_pool = _dot(dpool.astype(BF16), w_pool_ref[...]) * pool_scale_ref[...]
    mixed_s[:, D_RNN:D_RNN + D_POOL] = y_pool.astype(BF16)
    yield

    sc = _dot(q_mem.astype(BF16), kbd_ref[...]) * (XATTN_HEAD_DIM ** -0.5)
    ps = []
    for hh in range(XATTN_HEADS):
        sh = sc[:, hh * N_MEM:(hh + 1) * N_MEM]
        ph = jnp.exp(sh - jnp.max(sh, axis=-1, keepdims=True))
        ps.append((ph * (1.0 / jnp.sum(ph, axis=-1, keepdims=True))).astype(BF16))
        yield
    y_mem = _dot(jnp.concatenate(ps, axis=1), vbd_ref[...])
    mixed_s[:, D_RNN + D_POOL:] = y_mem.astype(BF16)
    yield

    uc = conv_b_ref[...] + conv_w_ref[CONV_WIDTH - 1:CONV_WIDTH, :] * urnn_buf[pl.ds(HIST, tm), :]
    for k in range(CONV_WIDTH - 1):
        off = HIST - (CONV_WIDTH - 1) + k
        uc = uc + conv_w_ref[k:k + 1, :] * urnn_buf[pl.ds(off, tm), :]
    yield

    ucb = uc.astype(BF16)
    pre = [_dot(ucb[:, c * cw:(c + 1) * cw], wg_ref[c]) for c in range(D_RNN // cw)]
    pre_a = jnp.concatenate([p[:, :cw] for p in pre], axis=1)
    pre_i = jnp.concatenate([p[:, cw:] for p in pre], axis=1)
    r = jax.nn.sigmoid(pre_a + b_a_ref[...])
    ig = jax.nn.sigmoid(pre_i + b_i_ref[...])
    yield
    lam = lam_ref[...]
    softplus_neg = jnp.maximum(-lam, 0.0) + jnp.log(1.0 + jnp.exp(-jnp.abs(lam)))
    log_a = (-LRU_C) * r * softplus_neg
    a = jnp.exp(log_a)
    mult = jnp.sqrt(jnp.maximum(1.0 - a * a, 0.0))
    mult = jnp.where(grow == 0, 1.0, mult)
    bx = mult * (ig * uc)
    yield

    ng = tm // SUBLANES
    a = a.reshape(ng, SUBLANES, D_RNN)
    bx = bx.reshape(ng, SUBLANES, D_RNN)
    row8 = lax.broadcasted_iota(jnp.int32, (1, SUBLANES, D_RNN), 1)
    d = 1
    while d < SUBLANES:
        a_sh = pltpu.roll(a, d, axis=1)
        b_sh = pltpu.roll(bx, d, axis=1)
        m = row8 >= d
        bx = jnp.where(m, a * b_sh + bx, bx)
        a = jnp.where(m, a * a_sh, a)
        d *= 2
        yield
    carry = jnp.broadcast_to(hcar[...], (SUBLANES, D_RNN))
    hs = []
    for g in range(ng):
        hv = a[g] * carry + bx[g]
        hs.append(hv)
        carry = jnp.broadcast_to(hv[SUBLANES - 1:SUBLANES, :], (SUBLANES, D_RNN))
    hcar[...] = carry[0:1, :]
    yield
    y_rnn = jnp.concatenate(hs, axis=0) * jax.nn.gelu(g_rnn)
    mixed_s[:, :D_RNN] = y_rnn.astype(BF16)
    urnn_buf[0:HIST, :] = urnn_buf[tm:tm + HIST, :]
    upool_buf[0:HIST, :] = upool_buf[tm:tm + HIST, :]
    yield

    mixed = mixed_s[...]
    x1s = []
    for c in range(D_MODEL // cw):
        x1s.append(x[:, c * cw:(c + 1) * cw] + _dot(mixed, w_out_ref[:, c * cw:(c + 1) * cw]))
        yield
    x1 = jnp.concatenate(x1s, axis=1)
    x1_ref[...] = x1

    hn = _rms(x1, g_ffn_ref[...]).astype(BF16)
    hn_ref[...] = _pack_bf16_pairs(hn)
    yield
    logits =lax.dot_general(w_rt_ref[...], hn, (((1,), (1,)), ((), ())),
                             preferred_element_type=F32) + b_r_ref[...]
    eid = lax.broadcasted_iota(jnp.int32, (N_EXPERTS, tm), 0)
    vals, idxs = [], []
    for _ in range(TOP_K):
        mx = jnp.max(logits, axis=0, keepdims=True)
        ix = jnp.min(jnp.where(logits == mx, eid, N_EXPERTS), axis=0, keepdims=True)
        vals.append(mx)
        idxs.append(ix)
        logits = jnp.where(eid == ix, -jnp.inf, logits)
    ex = [jnp.exp(v - vals[0]) for v in vals]
    den = ex[0] + ex[1] + ex[2] + ex[3]
    idx_ref[...] = jnp.concatenate(idxs, axis=0)
    gate_ref[...] = jnp.concatenate([e_ / den for e_ in ex], axis=0)


def _mix_call(x, g_mix, w_in, conv_w, conv_b, wg, b_a, b_i, lam, w_pool, pool_scale, kbd, vbd, w_out,
              g_ffn, w_rt, b_r):
    b, s, d = x.shape
    tm = MIX_ROWS
    ns = s // tm
    const2 = lambda shape: pl.BlockSpec(shape, lambda si: (0, 0))
    const3 = lambda shape: pl.BlockSpec(shape, lambda si: (0, 0, 0))
    return pl.pallas_call(
        _mix_kernel,
        grid=(ns,),
        in_specs=[
            pl.BlockSpec((b, tm, d), lambda si: (0, si, 0)),
            const2((1, d)),
            const2((d, D_IN)),
            const2((CONV_WIDTH, D_RNN)),
            const2((1, D_RNN)),
            const3((D_RNN // MXU_DIM, MXU_DIM, 2 * MXU_DIM)),
            const2((1, D_RNN)),
            const2((1, D_RNN)),
            const2((1, D_RNN)),
            const2((D_POOL, D_POOL)),
            const2((1, D_POOL)),
            const3((b, D_XATTN, XATTN_HEADS * N_MEM)),
            const3((b, XATTN_HEADS * N_MEM, D_XATTN)),
            const2((d, d)),
            const2((1, d)),
            const2((N_EXPERTS, d)),
            const2((N_EXPERTS, 1)),
        ],
        out_specs=[
            pl.BlockSpec((b, tm, d), lambda si: (0, si, 0)),
            pl.BlockSpec((b, tm, d // 2), lambda si: (0, si, 0)),
            pl.BlockSpec((b, TOP_K, tm), lambda si: (0, 0, si)),
            pl.BlockSpec((b, TOP_K, tm), lambda si: (0, 0, si)),
        ],
        out_shape=[
            jax.ShapeDtypeStruct((b, s, d), F32),
            jax.ShapeDtypeStruct((b, s, d // 2), jnp.uint32),
            jax.ShapeDtypeStruct((b, TOP_K, s), jnp.int32),
            jax.ShapeDtypeStruct((b, TOP_K, s), F32),
        ],
        scratch_shapes=[
            pltpu.VMEM((b, HIST + tm, D_RNN), F32),
            pltpu.VMEM((b, HIST + tm, D_POOL), F32),
            pltpu.VMEM((b, 1, D_RNN), F32),
            pltpu.VMEM((b, tm, d), BF16),
        ],
        compiler_params=pltpu.CompilerParams(
            dimension_semantics=("arbitrary",), vmem_limit_bytes=VMEM_LIMIT_BYTES),
        name="mixer_router",
    )(x, g_mix, w_in, conv_w, conv_b, wg, b_a, b_i, lam, w_pool, pool_scale, kbd, vbd, w_out, g_ffn, w_rt, b_r)


def _num_blocks(t):
    bm = EXPERT_ROWS
    return (t * TOP_K + N_EXPERTS * (bm - 1) + bm - 1) // bm


def _route_kernel(idx_ref, dest_ref, meta_ref, rank_s, carry_s, start_s, tri_s):
    p = pl.program_id(0)
    c = pl.program_id(1)
    nc = pl.num_programs(1)
    ch = ROUTE_CHUNK
    bm = float(EXPERT_ROWS)
    nbl = meta_ref.shape[1]
    eid = lax.broadcasted_iota(jnp.int32, (N_EXPERTS, ch), 0)
    idxc = idx_ref[...]

    @pl.when(jnp.logical_and(p == 0, c == 0))
    def _():
        carry_s[...] = jnp.zeros(carry_s.shape, F32)
        tri_s[...] = (lax.broadcasted_iota(jnp.int32, (ch, ch), 0)
                      < lax.broadcasted_iota(jnp.int32, (ch, ch), 1)).astype(BF16)

    @pl.when(p == 0)
    def _():
        sel = jnp.zeros((N_EXPERTS, ch), F32)
        for k in range(TOP_K):
            sel = sel + (idxc[k:k + 1, :] == eid).astype(F32)
        rank_s[c] = _dot(sel.astype(BF16), tri_s[...]) + carry_s[:, 0:1]
        carry_s[...] = carry_s[...] + jnp.sum(sel, axis=1, keepdims=True)

    @pl.when(jnp.logical_and(p == 0, c == nc - 1))
    def _():
        counts = carry_s[...]

        def div_bm(v):
            q = jnp.floor(v * (1.0 / bm))
            return q + jnp.where((q + 1.0) * bm <= v, 1.0, 0.0) - jnp.where(q * bm > v, 1.0, 0.0)

        padded = div_bm(counts + (bm - 1.0)) * bm
        e128 = lax.broadcasted_iota(jnp.int32, counts.shape, 0)
        pad_end = padded
        sh = 1
        while sh < N_EXPERTS:
            pad_end = pad_end + jnp.where(e128 >= sh, pltpu.roll(pad_end, sh, axis=0), 0.0)
            sh *= 2
        pad_start = pad_end - padded
        start_s[...] = pad_start
        total = pad_end[N_EXPERTS - 1:N_EXPERTS, 0:1]
        lane = lax.broadcasted_iota(jnp.int32, (1, nbl), 1)
        lane_f = lane.astype(F32)
        n_used = div_bm(total)
        bclamp = jnp.minimum(lane_f, n_used - 1.0)
        bstart = bclamp * bm
        pe, ps, cn, pd = pad_end[:, 0:1], pad_start[:, 0:1], counts[:, 0:1], padded[:, 0:1]
        be = jnp.minimum(jnp.sum((pe <= bstart).astype(F32), axis=0, keepdims=True), N_EXPERTS - 1.0)
        esub = lax.broadcasted_iota(jnp.int32, (N_EXPERTS, nbl), 0)
        onehot = esub.astype(F32) == be
        bv = jnp.sum(jnp.where(onehot, cn - (bstart - ps), 0.0), axis=0, keepdims=True)
        bv = jnp.where(lane_f * bm < total, jnp.clip(bv, 0.0, bm), 0.0)
        own = esub == lane
        fill_start = jnp.sum(jnp.where(own, ps + cn, 0.0), axis=0, keepdims=True)
        fill_n = jnp.sum(jnp.where(own, pd - cn, 0.0), axis=0, keepdims=True)
        later = jnp.logical_and(esub.astype(F32) > be, cn > 0.0)
        nxt = jnp.min(jnp.where(later, esub.astype(F32), float(N_EXPERTS)), axis=0, keepdims=True)
        nxt = jnp.where(nxt == float(N_EXPERTS), -1.0, nxt)
        zero = jnp.zeros((1, nbl), F32)
        meta_ref[...] = jnp.concatenate(
            [be, bv, bclamp, fill_start, fill_n, jnp.broadcast_to(n_used, (1, nbl)), nxt, zero],
            axis=0).astype(jnp.int32)

    @pl.when(p == 1)
    def _():
        base = start_s[:, 0:1] + rank_s[c]
        rows = [jnp.sum(jnp.where(idxc[k:k + 1, :] == eid, base, 0.0), axis=0, keepdims=True)
                for k in range(TOP_K)]
        dest_ref[...] = jnp.concatenate(rows, axis=0).astype(jnp.int32)


def _route_call(idx):
    k, t = idx.shape
    ch = ROUTE_CHUNK
    nc = t // ch
    nbl = -(-_num_blocks(t) // 128) * 128
    return pl.pallas_call(
        _route_kernel,
        grid=(2, nc),
        in_specs=[pl.BlockSpec((k, ch), lambda p, c: (0, c))],
        out_specs=[
            pl.BlockSpec((k, ch), lambda p, c: (0, c * p)),
            pl.BlockSpec((8, nbl), lambda p, c: (0, 0)),
        ],
        out_shape=[
            jax.ShapeDtypeStruct((k, t), jnp.int32),
            jax.ShapeDtypeStruct((8, nbl), jnp.int32),
        ],
        scratch_shapes=[
            pltpu.VMEM((nc, N_EXPERTS, ch), F32),
            pltpu.VMEM((N_EXPERTS, 128), F32),
            pltpu.VMEM((N_EXPERTS, 128), F32),
            pltpu.VMEM((ch, ch), BF16),
        ],
        compiler_params=pltpu.CompilerParams(dimension_semantics=("arbitrary", "arbitrary")),
        name="route_tables",
    )(idx)


def _sc_workers():
    info = plsc.get_sparse_core_info()
    return info.num_cores, info.num_cores * info.num_subcores


def _sc_dispatch_call(hn, dest, n_rows):
    t, w = hn.shape
    nc, nw = _sc_workers()
    per_w = t // nw
    ch = SC_ROWS
    mesh = plsc.VectorSubcoreMesh(core_axis_name="c", subcore_axis_name="s")

    @functools.partial(
        pl.kernel, mesh=mesh,
        out_type=jax.ShapeDtypeStruct((n_rows, w), hn.dtype),
        scratch_types=[pltpu.VMEM((ch,), jnp.int32), pltpu.VMEM((ch, w), hn.dtype)],
    )
    def body(hn_hbm, dest_hbm, xs_hbm, idx_v, rows_v):
        wid = lax.axis_index("s") * nc + lax.axis_index("c")

        @pl.loop(0, per_w // ch)
        def _(ci):
            base = pl.multiple_of(wid * per_w + ci * ch, ch)
            pltpu.sync_copy(hn_hbm.at[pl.ds(base, ch)], rows_v)
            for k in range(TOP_K):
                pltpu.sync_copy(dest_hbm.at[k, pl.ds(base, ch)], idx_v)
                pltpu.sync_copy(rows_v, xs_hbm.at[idx_v])

    return body(hn, dest)


def _sc_gather_call(ys, dest, part):
    k_, t = dest.shape
    tp = t // TOKEN_PARTS
    w = ys.shape[1]
    nc, nw = _sc_workers()
    per_w = tp // nw
    ch = SC_ROWS
    mesh = plsc.VectorSubcoreMesh(core_axis_name="c", subcore_axis_name="s")

    @functools.partial(
        pl.kernel, mesh=mesh,
        out_type=jax.ShapeDtypeStruct((k_, tp, w), ys.dtype),
        scratch_types=[pltpu.VMEM((ch,), jnp.int32), pltpu.VMEM((ch, w), ys.dtype)],
    )
    def body(ys_hbm, dest_hbm, y4_hbm, idx_v, rows_v):
        wid = lax.axis_index("s") * nc + lax.axis_index("c")

        @pl.loop(0, per_w // ch)
        def _(ci):
            base = pl.multiple_of(wid * per_w + ci * ch, ch)
            for k in range(k_):
                pltpu.sync_copy(dest_hbm.at[k, pl.ds(part * tp + base, ch)], idx_v)
                pltpu.sync_copy(ys_hbm.at[idx_v], rows_v)
                pltpu.sync_copy(rows_v, y4_hbm.at[k, pl.ds(base, ch)])

    return body(ys, dest)


def _pad_fill_kernel(fill_start_ref, fill_n_ref, n_used_ref, xs_in_ref, xs_ref, zero_s, zsem):
    del xs_in_ref
    bm = EXPERT_ROWS
    n_blocks = xs_ref.shape[0] // bm
    zero_s[...] = jnp.zeros(zero_s.shape, zero_s.dtype)

    def go(cp, wait):
        if wait:
            cp.wait()
        else:
            cp.start()

    def fill(e, wait):
        n = fill_n_ref[e]
        st = fill_start_ref[e]
        head = n & (SUBLANES - 1)
        for j in range(SUBLANES - 1):
            cp = pltpu.make_async_copy(zero_s.at[pl.ds(0, 1), :], xs_ref.at[pl.ds(st + j, 1), :], zsem)
            pl.when(j < head)(functools.partial(go, cp, wait))
        st = pl.multiple_of(st + head, SUBLANES)
        bit = 1 << ((bm - 1).bit_length() - 1)
        while bit >= SUBLANES:
            cp = pltpu.make_async_copy(zero_s.at[pl.ds(0, bit), :], xs_ref.at[pl.ds(st, bit), :], zsem)
            pl.when((n & bit) != 0)(functools.partial(go, cp, wait))
            st = pl.multiple_of(st + (n & bit), SUBLANES)
            bit //= 2

    def tail(blk, wait):
        cp = pltpu.make_async_copy(zero_s, xs_ref.at[pl.ds(pl.multiple_of(blk * bm, bm), bm), :], zsem)
        go(cp, wait)

    for wait in (False, True):
        lax.fori_loop(0, N_EXPERTS, lambda e, c, wait=wait: (fill(e, wait), c)[1], 0)
        lax.fori_loop(n_used_ref[0], n_blocks, lambda blk, c, wait=wait: (tail(blk, wait), c)[1], 0)


def _pad_fill_call(fill_start, fill_n, n_used, xs):
    grid_spec = pltpu.PrefetchScalarGridSpec(
        num_scalar_prefetch=3,
        grid=(1,),
        in_specs=[pl.BlockSpec(memory_space=pl.ANY)],
        out_specs=pl.BlockSpec(memory_space=pl.ANY),
        scratch_shapes=[pltpu.VMEM((EXPERT_ROWS, xs.shape[1]), xs.dtype), pltpu.SemaphoreType.DMA],
    )
    return pl.pallas_call(
        _pad_fill_kernel,
        grid_spec=grid_spec,
        out_shape=jax.ShapeDtypeStruct(xs.shape, xs.dtype),
        input_output_aliases={3: 0},
        compiler_params=pltpu.CompilerParams(dimension_semantics=("arbitrary",)),
        name="pad_fill",
    )(fill_start, fill_n, n_used, xs)


def _expert_kernel(be_ref, bv_ref, bi_ref, nx_ref, x_ref, wgu_hbm, bgu_ref, wd_hbm, bd_ref, y_ref,
                   wgu_st, wd_st, wgu_bf, wd_bf, wsem):
    del bi_ref
    i = pl.program_id(0)
    e = be_ref[i]
    e_prev = be_ref[jnp.maximum(i - 1, 0)]
    valid = bv_ref[i]

    def weight_copies(ex):
        return (pltpu.make_async_copy(wgu_hbm.at[ex], wgu_st, wsem.at[0]),
                pltpu.make_async_copy(wd_hbm.at[ex], wd_st, wsem.at[1]))

    @pl.when(i == 0)
    def _():
        for cp in weight_copies(e):
            cp.start()

    @pl.when(jnp.logical_and(valid > 0, jnp.logical_or(i == 0, e != e_prev)))
    def _():
        for cp in weight_copies(e):
            cp.wait()
        wgu_bf[...] = wgu_st[...].astype(BF16)
        wd_bf[...] = wd_st[...].astype(BF16)
        nxt = nx_ref[i]

        @pl.when(nxt >= 0)
        def _():
            for cp in weight_copies(nxt):
                cp.start()

    @pl.when(valid > 0)
    def _():
        x_lo, x_hi = _unpack_bf16_pairs(x_ref[...])
        xb = jnp.concatenate([x_lo.astype(BF16), x_hi.astype(BF16)], axis=1)
        hs = []
        for c in range(D_FF // (2 * MXU_DIM)):
            lo = c * 2 * MXU_DIM
            hi = lo + 2 * MXU_DIM
            g = _dot(xb, wgu_bf[:, lo:hi]) + bgu_ref[0, :, lo:hi]
            up = _dot(xb, wgu_bf[:, D_FF + lo:D_FF + hi]) + bgu_ref[0, :, D_FF + lo:D_FF + hi]
            g = jnp.minimum(g, SWIGLU_LIMIT)
            up = jnp.clip(up, -SWIGLU_LIMIT, SWIGLU_LIMIT)
            glu = g * jax.nn.sigmoid(g * SWIGLU_ALPHA)
            hs.append(((up + 1.0) * glu).astype(BF16))
        y_ref[...] = _pack_bf16_pairs(_dot(jnp.concatenate(hs, axis=1), wd_bf[...]) + bd_ref[0])

    @pl.when(valid == 0)
    def _():
        y_ref[...] = jnp.zeros(y_ref.shape, y_ref.dtype)


def _expert_call(block_e, block_valid, block_idx, block_next, xs, w_gu, b_gu, w_down, b_down):
    n_pad, dw = xs.shape
    d = 2 * dw
    bm = EXPERT_ROWS
    nb = n_pad // bm
    grid_spec = pltpu.PrefetchScalarGridSpec(
        num_scalar_prefetch=4,
        grid=(nb,),
        in_specs=[
            pl.BlockSpec((bm, dw), lambda i, be, bv, bi, nx: (bi[i], 0)),
            pl.BlockSpec(memory_space=pl.ANY),
            pl.BlockSpec((1, 1, 2 * D_FF), lambda i, be, bv, bi, nx: (be[i], 0, 0)),
            pl.BlockSpec(memory_space=pl.ANY),
            pl.BlockSpec((1, 1, d), lambda i, be, bv, bi, nx: (be[i], 0, 0)),
        ],
        out_specs=pl.BlockSpec((bm, dw), lambda i, be, bv, bi, nx: (i, 0)),
        scratch_shapes=[
            pltpu.VMEM((d, 2 * D_FF), F32),
            pltpu.VMEM((D_FF, d), F32),
            pltpu.VMEM((d, 2 * D_FF), BF16),
            pltpu.VMEM((D_FF, d), BF16),
            pltpu.SemaphoreType.DMA((2,)),
        ],
    )
    return pl.pallas_call(
        _expert_kernel,
        grid_spec=grid_spec,
        out_shape=jax.ShapeDtypeStruct((n_pad, dw), jnp.uint32),
        compiler_params=pltpu.CompilerParams(
            dimension_semantics=("arbitrary",), vmem_limit_bytes=VMEM_LIMIT_BYTES),
        name="expert_ffn",
    )(block_e, block_valid, block_idx, block_next, xs, w_gu, b_gu, w_down, b_down)


def _combine_kernel(x1_ref, gate_ref, y4_ref, g_ref, out_ref):
    x1 = x1_ref[...]
    n = x1.shape[1] // 2
    gate = gate_ref[...]
    acc_lo, acc_hi = x1[:, :n], x1[:, n:]
    for k in range(TOP_K):
        y_lo, y_hi = _unpack_bf16_pairs(y4_ref[k])
        acc_lo = acc_lo + gate[:, k:k + 1] * y_lo
        acc_hi = acc_hi + gate[:, k:k + 1] * y_hi
    out_ref[...] = _rms(jnp.concatenate([acc_lo, acc_hi], axis=1), g_ref[...])


def _combine_into_kernel(x1_ref, gate_ref, y4_ref, g_ref, prev_ref, out_ref):
    del prev_ref
    _combine_kernel(x1_ref, gate_ref, y4_ref, g_ref, out_ref)


def _combine_call(x1, gate_t, y4_part, g_final, part, out_prev):
    t, d = x1.shape
    tt = COMBINE_ROWS
    nb = t // TOKEN_PARTS // tt
    off = part * nb
    in_specs = [
        pl.BlockSpec((tt, d), lambda i: (off + i, 0)),
        pl.BlockSpec((tt, TOP_K), lambda i: (off + i, 0)),
        pl.BlockSpec((TOP_K, tt, d // 2), lambda i: (0, i, 0)),
        pl.BlockSpec((1, d), lambda i: (0, 0)),
    ]
    args = [x1, gate_t, y4_part, g_final]
    body = _combine_kernel
    aliases = {}
    if out_prev is not None:
        in_specs.append(pl.BlockSpec(memory_space=pl.ANY))
        args.append(out_prev)
        aliases = {len(args) - 1: 0}
        body = _combine_into_kernel
    return pl.pallas_call(
        body,
        grid=(nb,),
        in_specs=in_specs,
        out_specs=pl.BlockSpec((tt, d), lambda i: (off + i, 0)),
        out_shape=jax.ShapeDtypeStruct((t, d), F32),
        input_output_aliases=aliases,
        compiler_params=pltpu.CompilerParams(dimension_semantics=("arbitrary",)),
        name="combine_norm",
    )(*args)


def _block_diag(w):
    hh, n, _ = w.shape
    eye = jnp.eye(hh, dtype=w.dtype)
    return jnp.einsum("hij,hg->higj", w, eye).reshape(hh * n, hh * n)


def kernel(x, mem, norm_mix_g, w_in, conv_w, conv_b, w_rg_a, b_rg_a, w_rg_i, b_rg_i, lru_lambda, w_pool,
           pool_scale, mem_norm_g, w_mem_kv, w_out, norm_ffn_g, w_router, b_router, w_gu, b_gu, w_down,
           b_down, final_norm_g):
    b, s, d = x.shape
    t = b * s
    l = 0
    row = lambda v: v.reshape(1, -1)

    kv = _kv_call(mem, row(mem_norm_g[l]), w_mem_kv[l].astype(BF16))
    kh = kv[..., :D_XATTN].reshape(b, N_MEM, XATTN_HEADS, XATTN_HEAD_DIM)
    vh = kv[..., D_XATTN:].reshape(b, N_MEM, XATTN_HEADS, XATTN_HEAD_DIM)
    eye_h = jnp.eye(XATTN_HEADS, dtype=F32)
    kbd = jnp.einsum("bmhd,hg->bhdgm", kh, eye_h).reshape(b, D_XATTN, XATTN_HEADS * N_MEM).astype(BF16)
    vbd = jnp.einsum("bmhd,hg->bgmhd", vh, eye_h).reshape(b, XATTN_HEADS * N_MEM, D_XATTN).astype(BF16)

    heads_per = MXU_DIM // RNN_HEAD_DIM
    wg = jnp.stack([
        jnp.concatenate([_block_diag(w_rg_a[l, c * heads_per:(c + 1) * heads_per]),
                         _block_diag(w_rg_i[l, c * heads_per:(c + 1) * heads_per])], axis=1)
        for c in range(D_RNN // MXU_DIM)]).astype(BF16)

    x1, hn, idx, gate = _mix_call(
        x, row(norm_mix_g[l]), w_in[l].astype(BF16), conv_w[l], row(conv_b[l]), wg, row(b_rg_a[l]),
        row(b_rg_i[l]), row(lru_lambda[l]), _block_diag(w_pool[l]).astype(BF16), row(pool_scale[l]), kbd, vbd,
        w_out[l].astype(BF16), row(norm_ffn_g[l]), w_router[l].T.astype(BF16), b_router[l].reshape(-1, 1))

    hn = hn.reshape(t, d // 2)
    idx = jnp.transpose(idx, (1, 0, 2)).reshape(TOP_K, t)
    gate = jnp.transpose(gate, (1, 0, 2)).reshape(TOP_K, t)
    n_blocks = _num_blocks(t)
    dest, meta = _route_call(idx)
    be, bv, bi, nx = meta[0, :n_blocks], meta[1, :n_blocks], meta[2, :n_blocks], meta[6, :n_blocks]
    fill_start, fill_n, n_used = meta[3, :N_EXPERTS], meta[4, :N_EXPERTS], meta[5, :1]

    xs = _sc_dispatch_call(hn, dest, n_blocks * EXPERT_ROWS)
    xs = _pad_fill_call(fill_start, fill_n, n_used, xs)
    ys = _expert_call(be, bv, bi, nx, xs, w_gu[l], b_gu[l].reshape(N_EXPERTS, 1, -1), w_down[l],
                      b_down[l].reshape(N_EXPERTS, 1, -1))
    x1 = x1.reshape(t, d)
    gate_t = gate.T
    out = None
    for part in range(TOKEN_PARTS):
        y4 = _sc_gather_call(ys, dest, part)
        out = _combine_call(x1, gate_t, y4, row(final_norm_g), part, out)
    return out.reshape(b, s, d)
```

```python
import functools
import math

import jax
import jax.numpy as jnp
from jax import lax
from jax.experimental import pallas as pl
from jax.experimental.pallas import tpu as pltpu
from jax.experimental.pallas import tpu_sc as plsc

D_MODEL = 1024
N_MEM = 256
D_RNN = 512
RNN_HEADS = 8
RNN_HEAD_DIM = D_RNN // RNN_HEADS
CONV_WIDTH = 4
LRU_C = 8.0
D_POOL = 256
POOL_WINDOWS = (2, 4, 8, 16)
POOL_GROUP_DIM = D_POOL // len(POOL_WINDOWS)
MAX_WINDOW = max(POOL_WINDOWS)
D_XATTN = 256
XATTN_HEADS = 4
XATTN_HEAD_DIM = D_XATTN // XATTN_HEADS
D_IN = 2 * D_RNN + D_POOL + D_XATTN
N_EXPERTS = 32
TOP_K = 4
D_FF = D_MODEL
SWIGLU_ALPHA = 1.702
SWIGLU_LIMIT = 7.0
RMS_EPS = 1e-6

SUBLANES = 8
MXU_DIM = 256
VMEM_LIMIT_BYTES = 56 * 1024 * 1024

MIX_ROWS = 512
HIST = 16
EXPERT_ROWS = 528
COMBINE_ROWS = 512
ROUTE_CHUNK = 1024
SC_ROWS = 128
TOKEN_PARTS = 1
MIX_PHASE_STAGES = (1 + D_IN // MXU_DIM, XATTN_HEADS + 10, D_MODEL // MXU_DIM + 2)

BF16 = jnp.bfloat16
F32 = jnp.float32


def _rms(xf, g):
    return xf * lax.rsqrt(jnp.mean(xf * xf, axis=-1, keepdims=True) + RMS_EPS) * g


def _dot(a, b):
    return jnp.dot(a, b, preferred_element_type=F32)


def _pack_bf16_pairs(v):
    n = v.shape[1] // 2
    bits = lax.bitcast_convert_type(v.astype(BF16).astype(F32), jnp.uint32)
    return (bits[:, :n] >> 16) | (bits[:, n:] & jnp.uint32(0xFFFF0000))


def _unpack_bf16_pairs(w):
    lo = lax.bitcast_convert_type(w << 16, F32)
    hi = lax.bitcast_convert_type(w & jnp.uint32(0xFFFF0000), F32)
    return lo, hi


def _kv_kernel(mem_ref, g_ref, w_ref, kv_ref):
    mn = _rms(mem_ref[0], g_ref[...])
    kv_ref[0] = _dot(mn.astype(BF16), w_ref[...])


def _kv_call(mem, g, w_kv_bf):
    b = mem.shape[0]
    return pl.pallas_call(
        _kv_kernel,
        grid=(b,),
        in_specs=[
            pl.BlockSpec((1, N_MEM, D_MODEL), lambda i: (i, 0, 0)),
            pl.BlockSpec((1, D_MODEL), lambda i: (0, 0)),
            pl.BlockSpec((D_MODEL, 2 * D_XATTN), lambda i: (0, 0)),
        ],
        out_specs=pl.BlockSpec((1, N_MEM, 2 * D_XATTN), lambda i: (i, 0, 0)),
        out_shape=jax.ShapeDtypeStruct((b, N_MEM, 2 * D_XATTN), F32),
        name="kv_proj",
    )(mem, g, w_kv_bf)


def _mix_kernel(x_ref, g_mix_ref, w_in_ref, conv_w_ref, conv_b_ref, wg_ref, b_a_ref, b_i_ref, lam_ref,
                w_pool_ref, pool_scale_ref, kbd_ref, vbd_ref, w_out_ref, g_ffn_ref, w_rt_ref, b_r_ref,
                x1_ref, hn_ref, idx_ref, gate_ref,
                urnn_buf, upool_buf, hcar, mixed_s):
    tiles = [
        _mix_tile(x_ref.at[b], g_mix_ref, w_in_ref, conv_w_ref, conv_b_ref, wg_ref, b_a_ref, b_i_ref, lam_ref,
                  w_pool_ref, pool_scale_ref, kbd_ref.at[b], vbd_ref.at[b], w_out_ref, g_ffn_ref, w_rt_ref,
                  b_r_ref, x1_ref.at[b], hn_ref.at[b], idx_ref.at[b], gate_ref.at[b],
                  urnn_buf.at[b], upool_buf.at[b], hcar.at[b], mixed_s.at[b])
        for b in range(x_ref.shape[0])]
    n_phase = len(MIX_PHASE_STAGES)
    for step in range(len(tiles) + n_phase - 1):
        active = [(tiles[j], MIX_PHASE_STAGES[step - j]) for j in range(len(tiles)) if 0 <= step - j < n_phase]
        ticks = max(n for _, n in active)
        done = [0] * len(active)
        for i in range(ticks):
            for a, (tile, n) in enumerate(active):
                while done[a] * ticks < (i + 1) * n:
                    next(tile, None)
                    done[a] += 1
    done_mark = object()
    assert all(next(tile, done_mark) is done_mark for tile in tiles), "MIX_PHASE_STAGES does not cover _mix_tile"


def _mix_tile(x_ref, g_mix_ref, w_in_ref, conv_w_ref, conv_b_ref, wg_ref, b_a_ref, b_i_ref, lam_ref,
              w_pool_ref, pool_scale_ref, kbd_ref, vbd_ref, w_out_ref, g_ffn_ref, w_rt_ref, b_r_ref,
              x1_ref, hn_ref, idx_ref, gate_ref,
              urnn_buf, upool_buf, hcar, mixed_s):
    tm = MIX_ROWS
    cw = MXU_DIM
    s = pl.program_id(0)

    @pl.when(s == 0)
    def _():
        urnn_buf[0:HIST, :] = jnp.zeros((HIST, D_RNN), F32)
        upool_buf[0:HIST, :] = jnp.zeros((HIST, D_POOL), F32)
        hcar[...] = jnp.zeros((1, D_RNN), F32)

    x = x_ref[...]
    hb = _rms(x, g_mix_ref[...]).astype(BF16)
    yield
    proj = []
    for c in range(D_IN // cw):
        proj.append(_dot(hb, w_in_ref[:, c * cw:(c + 1) * cw]))
        yield
    urnn_buf[HIST:, :] = jnp.concatenate(proj[:D_RNN // cw], axis=1)
    g_rnn = jnp.concatenate(proj[D_RNN // cw:2 * D_RNN // cw], axis=1)
    upool_buf[HIST:, :] = proj[2 * D_RNN // cw]
    q_mem = proj[2 * D_RNN // cw + 1]

    row = lax.broadcasted_iota(jnp.int32, (tm, 1), 0)
    grow = row + s * tm

    e = upool_buf[...]
    s2 = e + pltpu.roll(e, 1, axis=0)
    s4 = s2 + pltpu.roll(s2, 2, axis=0)
    s8 = s4 + pltpu.roll(s4, 4, axis=0)
    s16 = s8 + pltpu.roll(s8, 8, axis=0)
    grp = lax.broadcasted_iota(jnp.int32, (tm, D_POOL), 1) // POOL_GROUP_DIM
    wsum = jnp.where(grp == 0, s2[HIST:], jnp.where(grp == 1, s4[HIST:], jnp.where(grp == 2, s8[HIST:], s16[HIST:])))
    win = jnp.where(grp == 0, 2.0, jnp.where(grp == 1, 4.0, jnp.where(grp == 2, 8.0, 16.0)))
    pos = (grow + 1).astype(F32)
    dpool = wsum / jnp.minimum(pos, win) - e[HIST:]
    y_pool = _dot(dpool.astype(BF16), w_pool_ref[...]) * pool_scale_ref[...]
    mixed_s[:, D_RNN:D_RNN + D_POOL] = y_pool.astype(BF16)
    yield

    sc = _dot(q_mem.astype(BF16), kbd_ref[...]) * (XATTN_HEAD_DIM ** -0.5)
    ps = []
    for hh in range(XATTN_HEADS):
        sh = sc[:, hh * N_MEM:(hh + 1) * N_MEM]
        ph = jnp.exp(sh - jnp.max(sh, axis=-1, keepdims=True))
        ps.append((ph * (1.0 / jnp.sum(ph, axis=-1, keepdims=True))).astype(BF16))
        yield
    y_mem = _dot(jnp.concatenate(ps, axis=1), vbd_ref[...])
    mixed_s[:, D_RNN + D_POOL:] = y_mem.astype(BF16)
    yield

    uc = conv_b_ref[...] + conv_w_ref[CONV_WIDTH - 1:CONV_WIDTH, :] * urnn_buf[pl.ds(HIST, tm), :]
    for k in range(CONV_WIDTH - 1):
        off = HIST - (CONV_WIDTH - 1) + k
        uc = uc + conv_w_ref[k:k + 1, :] * urnn_buf[pl.ds(off, tm), :]
    yield

    ucb = uc.astype(BF16)
    pre = [_dot(ucb[:, c * cw:(c + 1) * cw], wg_ref[c]) for c in range(D_RNN // cw)]
    pre_a = jnp.concatenate([p[:, :cw] for p in pre], axis=1)
    pre_i = jnp.concatenate([p[:, cw:] for p in pre], axis=1)
    r = jax.nn.sigmoid(pre_a + b_a_ref[...])
    ig = jax.nn.sigmoid(pre_i + b_i_ref[...])
    yield
    lam = lam_ref[...]
    softplus_neg = jnp.maximum(-lam, 0.0) + jnp.log(1.0 + jnp.exp(-jnp.abs(lam)))
    log_a = (-LRU_C) * r * softplus_neg
    a = jnp.exp(log_a)
    mult = jnp.sqrt(jnp.maximum(1.0 - a * a, 0.0))
    mult = jnp.where(grow == 0, 1.0, mult)
    bx = mult * (ig * uc)
    yield

    ng = tm // SUBLANES
    a = a.reshape(ng, SUBLANES, D_RNN)
    bx = bx.reshape(ng, SUBLANES, D_RNN)
    row8 = lax.broadcasted_iota(jnp.int32, (1, SUBLANES, D_RNN), 1)
    d = 1
    while d < SUBLANES:
        a_sh = pltpu.roll(a, d, axis=1)
        b_sh = pltpu.roll(bx, d, axis=1)
        m = row8 >= d
        bx = jnp.where(m, a * b_sh + bx, bx)
        a = jnp.where(m, a * a_sh, a)
        d *= 2
        yield
    carry = jnp.broadcast_to(hcar[...], (SUBLANES, D_RNN))
    hs = []
    for g in range(ng):
        hv = a[g] * carry + bx[g]
        hs.append(hv)
        carry = jnp.broadcast_to(hv[SUBLANES - 1:SUBLANES, :], (SUBLANES, D_RNN))
    hcar[...] = carry[0:1, :]
    yield
    y_rnn = jnp.concatenate(hs, axis=0) * jax.nn.gelu(g_rnn)
    mixed_s[:, :D_RNN] = y_rnn.astype(BF16)
    urnn_buf[0:HIST, :] = urnn_buf[tm:tm + HIST, :]
    upool_buf[0:HIST, :] = upool_buf[tm:tm + HIST, :]
    yield

    mixed = mixed_s[...]
    x1s = []
    for c in range(D_MODEL // cw):
        x1s.append(x[:, c * cw:(c + 1) * cw] + _dot(mixed, w_out_ref[:, c * cw:(c + 1) * cw]))
        yield
    x1 = jnp.concatenate(x1s, axis=1)
    x1_ref[...] = x1

    hn = _rms(x1, g_ffn_ref[...]).astype(BF16)
    hn_ref[...] = _pack_bf16_pairs(hn)
    yield
    logits =lax.dot_general(w_rt_ref[...], hn, (((1,), (1,)), ((), ())),
                             preferred_element_type=F32) + b_r_ref[...]
    eid = lax.broadcasted_iota(jnp.int32, (N_EXPERTS, tm), 0)
    vals, idxs = [], []
    for _ in range(TOP_K):
        mx = jnp.max(logits, axis=0, keepdims=True)
        ix = jnp.min(jnp.where(logits == mx, eid, N_EXPERTS), axis=0, keepdims=True)
        vals.append(mx)
        idxs.append(ix)
        logits = jnp.where(eid == ix, -jnp.inf, logits)
    ex = [jnp.exp(v - vals[0]) for v in vals]
    den = ex[0] + ex[1] + ex[2] + ex[3]
    idx_ref[...] = jnp.concatenate(idxs, axis=0)
    gate_ref[...] = jnp.concatenate([e_ / den for e_ in ex], axis=0)


def _mix_call(x, g_mix, w_in, conv_w, conv_b, wg, b_a, b_i, lam, w_pool, pool_scale, kbd, vbd, w_out,
              g_ffn, w_rt, b_r):
    b, s, d = x.shape
    tm = MIX_ROWS
    ns = s // tm
    const2 = lambda shape: pl.BlockSpec(shape, lambda si: (0, 0))
    const3 = lambda shape: pl.BlockSpec(shape, lambda si: (0, 0, 0))
    return pl.pallas_call(
        _mix_kernel,
        grid=(ns,),
        in_specs=[
            pl.BlockSpec((b, tm, d), lambda si: (0, si, 0)),
            const2((1, d)),
            const2((d, D_IN)),
            const2((CONV_WIDTH, D_RNN)),
            const2((1, D_RNN)),
            const3((D_RNN // MXU_DIM, MXU_DIM, 2 * MXU_DIM)),
            const2((1, D_RNN)),
            const2((1, D_RNN)),
            const2((1, D_RNN)),
            const2((D_POOL, D_POOL)),
            const2((1, D_POOL)),
            const3((b, D_XATTN, XATTN_HEADS * N_MEM)),
            const3((b, XATTN_HEADS * N_MEM, D_XATTN)),
            const2((d, d)),
            const2((1, d)),
            const2((N_EXPERTS, d)),
            const2((N_EXPERTS, 1)),
        ],
        out_specs=[
            pl.BlockSpec((b, tm, d), lambda si: (0, si, 0)),
            pl.BlockSpec((b, tm, d // 2), lambda si: (0, si, 0)),
            pl.BlockSpec((b, TOP_K, tm), lambda si: (0, 0, si)),
            pl.BlockSpec((b, TOP_K, tm), lambda si: (0, 0, si)),
        ],
        out_shape=[
            jax.ShapeDtypeStruct((b, s, d), F32),
            jax.ShapeDtypeStruct((b, s, d // 2), jnp.uint32),
            jax.ShapeDtypeStruct((b, TOP_K, s), jnp.int32),
            jax.ShapeDtypeStruct((b, TOP_K, s), F32),
        ],
        scratch_shapes=[
            pltpu.VMEM((b, HIST + tm, D_RNN), F32),
            pltpu.VMEM((b, HIST + tm, D_POOL), F32),
            pltpu.VMEM((b, 1, D_RNN), F32),
            pltpu.VMEM((b, tm, d), BF16),
        ],
        compiler_params=pltpu.CompilerParams(
            dimension_semantics=("arbitrary",), vmem_limit_bytes=VMEM_LIMIT_BYTES),
        name="mixer_router",
    )(x, g_mix, w_in, conv_w, conv_b, wg, b_a, b_i, lam, w_pool, pool_scale, kbd, vbd, w_out, g_ffn, w_rt, b_r)


def _num_blocks(t):
    bm = EXPERT_ROWS
    return (t * TOP_K + N_EXPERTS * (bm - 1) + bm - 1) // bm


def _route_kernel(idx_ref, dest_ref, meta_ref, rank_s, carry_s, start_s, tri_s):
    p = pl.program_id(0)
    c = pl.program_id(1)
    nc = pl.num_programs(1)
    ch = ROUTE_CHUNK
    bm = float(EXPERT_ROWS)
    nbl = meta_ref.shape[1]
    eid = lax.broadcasted_iota(jnp.int32, (N_EXPERTS, ch), 0)
    idxc = idx_ref[...]

    @pl.when(jnp.logical_and(p == 0, c == 0))
    def _():
        carry_s[...] = jnp.zeros(carry_s.shape, F32)
        tri_s[...] = (lax.broadcasted_iota(jnp.int32, (ch, ch), 0)
                      < lax.broadcasted_iota(jnp.int32, (ch, ch), 1)).astype(BF16)

    @pl.when(p == 0)
    def _():
        sel = jnp.zeros((N_EXPERTS, ch), F32)
        for k in range(TOP_K):
            sel = sel + (idxc[k:k + 1, :] == eid).astype(F32)
        rank_s[c] = _dot(sel.astype(BF16), tri_s[...]) + carry_s[:, 0:1]
        carry_s[...] = carry_s[...] + jnp.sum(sel, axis=1, keepdims=True)

    @pl.when(jnp.logical_and(p == 0, c == nc - 1))
    def _():
        counts = carry_s[...]

        def div_bm(v):
            q = jnp.floor(v * (1.0 / bm))
            return q + jnp.where((q + 1.0) * bm <= v, 1.0, 0.0) - jnp.where(q * bm > v, 1.0, 0.0)

        padded = div_bm(counts + (bm - 1.0)) * bm
        e128 = lax.broadcasted_iota(jnp.int32, counts.shape, 0)
        pad_end = padded
        sh = 1
        while sh < N_EXPERTS:
            pad_end = pad_end + jnp.where(e128 >= sh, pltpu.roll(pad_end, sh, axis=0), 0.0)
            sh *= 2
        pad_start = pad_end - padded
        start_s[...] = pad_start
        total = pad_end[N_EXPERTS - 1:N_EXPERTS, 0:1]
        lane = lax.broadcasted_iota(jnp.int32, (1, nbl), 1)
        lane_f = lane.astype(F32)
        n_used = div_bm(total)
        bclamp = jnp.minimum(lane_f, n_used - 1.0)
        bstart = bclamp * bm
        pe, ps, cn, pd = pad_end[:, 0:1], pad_start[:, 0:1], counts[:, 0:1], padded[:, 0:1]
        be = jnp.minimum(jnp.sum((pe <= bstart).astype(F32), axis=0, keepdims=True), N_EXPERTS - 1.0)
        esub = lax.broadcasted_iota(jnp.int32, (N_EXPERTS, nbl), 0)
        onehot = esub.astype(F32) == be
        bv = jnp.sum(jnp.where(onehot, cn - (bstart - ps), 0.0), axis=0, keepdims=True)
        bv = jnp.where(lane_f * bm < total, jnp.clip(bv, 0.0, bm), 0.0)
        own = esub == lane
        fill_start = jnp.sum(jnp.where(own, ps + cn, 0.0), axis=0, keepdims=True)
        fill_n = jnp.sum(jnp.where(own, pd - cn, 0.0), axis=0, keepdims=True)
        later = jnp.logical_and(esub.astype(F32) > be, cn > 0.0)
        nxt = jnp.min(jnp.where(later, esub.astype(F32), float(N_EXPERTS)), axis=0, keepdims=True)
        nxt = jnp.where(nxt == float(N_EXPERTS), -1.0, nxt)
        zero = jnp.zeros((1, nbl), F32)
        meta_ref[...] = jnp.concatenate(
            [be, bv, bclamp, fill_start, fill_n, jnp.broadcast_to(n_used, (1, nbl)), nxt, zero],
            axis=0).astype(jnp.int32)

    @pl.when(p == 1)
    def _():
        base = start_s[:, 0:1] + rank_s[c]
        rows = [jnp.sum(jnp.where(idxc[k:k + 1, :] == eid, base, 0.0), axis=0, keepdims=True)
                for k in range(TOP_K)]
        dest_ref[...] = jnp.concatenate(rows, axis=0).astype(jnp.int32)


def _route_call(idx):
    k, t = idx.shape
    ch = ROUTE_CHUNK
    nc = t // ch
    nbl = -(-_num_blocks(t) // 128) * 128
    return pl.pallas_call(
        _route_kernel,
        grid=(2, nc),
        in_specs=[pl.BlockSpec((k, ch), lambda p, c: (0, c))],
        out_specs=[
            pl.BlockSpec((k, ch), lambda p, c: (0, c * p)),
            pl.BlockSpec((8, nbl), lambda p, c: (0, 0)),
        ],
        out_shape=[
            jax.ShapeDtypeStruct((k, t), jnp.int32),
            jax.ShapeDtypeStruct((8, nbl), jnp.int32),
        ],
        scratch_shapes=[
            pltpu.VMEM((nc, N_EXPERTS, ch), F32),
            pltpu.VMEM((N_EXPERTS, 128), F32),
            pltpu.VMEM((N_EXPERTS, 128), F32),
            pltpu.VMEM((ch, ch), BF16),
        ],
        compiler_params=pltpu.CompilerParams(dimension_semantics=("arbitrary", "arbitrary")),
        name="route_tables",
    )(idx)


def _sc_workers():
    info = plsc.get_sparse_core_info()
    return info.num_cores, info.num_cores * info.num_subcores


def _sc_dispatch_call(hn, dest, n_rows):
    t, w = hn.shape
    nc, nw = _sc_workers()
    per_w = t // nw
    ch = SC_ROWS
    mesh = plsc.VectorSubcoreMesh(core_axis_name="c", subcore_axis_name="s")

    @functools.partial(
        pl.kernel, mesh=mesh,
        out_type=jax.ShapeDtypeStruct((n_rows, w), hn.dtype),
        scratch_types=[pltpu.VMEM((ch,), jnp.int32), pltpu.VMEM((ch, w), hn.dtype)],
    )
    def body(hn_hbm, dest_hbm, xs_hbm, idx_v, rows_v):
        wid = lax.axis_index("s") * nc + lax.axis_index("c")

        @pl.loop(0, per_w // ch)
        def _(ci):
            base = pl.multiple_of(wid * per_w + ci * ch, ch)
            pltpu.sync_copy(hn_hbm.at[pl.ds(base, ch)], rows_v)
            for k in range(TOP_K):
                pltpu.sync_copy(dest_hbm.at[k, pl.ds(base, ch)], idx_v)
                pltpu.sync_copy(rows_v, xs_hbm.at[idx_v])

    return body(hn, dest)


def _sc_gather_call(ys, dest, part):
    k_, t = dest.shape
    tp = t // TOKEN_PARTS
    w = ys.shape[1]
    nc, nw = _sc_workers()
    per_w = tp // nw
    ch = SC_ROWS
    mesh = plsc.VectorSubcoreMesh(core_axis_name="c", subcore_axis_name="s")

    @functools.partial(
        pl.kernel, mesh=mesh,
        out_type=jax.ShapeDtypeStruct((k_, tp, w), ys.dtype),
        scratch_types=[pltpu.VMEM((ch,), jnp.int32), pltpu.VMEM((ch, w), ys.dtype)],
    )
    def body(ys_hbm, dest_hbm, y4_hbm, idx_v, rows_v):
        wid = lax.axis_index("s") * nc + lax.axis_index("c")

        @pl.loop(0, per_w // ch)
        def _(ci):
            base = pl.multiple_of(wid * per_w + ci * ch, ch)
            for k in range(k_):
                pltpu.sync_copy(dest_hbm.at[k, pl.ds(part * tp + base, ch)], idx_v)
                pltpu.sync_copy(ys_hbm.at[idx_v], rows_v)
                pltpu.sync_copy(rows_v, y4_hbm.at[k, pl.ds(base, ch)])

    return body(ys, dest)


def _pad_fill_kernel(fill_start_ref, fill_n_ref, xs_in_ref, xs_ref, zero_s, zsem):
    del xs_in_ref
    bm = EXPERT_ROWS
    zero_s[...] = jnp.zeros(zero_s.shape, zero_s.dtype)

    def go(cp, wait):
        if wait:
            cp.wait()
        else:
            cp.start()

    def fill(e, wait):
        n = fill_n_ref[e]
        st = fill_start_ref[e]
        head = n & (SUBLANES - 1)
        for j in range(SUBLANES - 1):
            cp = pltpu.make_async_copy(zero_s.at[pl.ds(0, 1), :], xs_ref.at[pl.ds(st + j, 1), :], zsem)
            pl.when(j < head)(functools.partial(go, cp, wait))
        st = pl.multiple_of(st + head, SUBLANES)
        bit = 1 << ((bm - 1).bit_length() - 1)
        while bit >= SUBLANES:
            cp = pltpu.make_async_copy(zero_s.at[pl.ds(0, bit), :], xs_ref.at[pl.ds(st, bit), :], zsem)
            pl.when((n & bit) != 0)(functools.partial(go, cp, wait))
            st = pl.multiple_of(st + (n & bit), SUBLANES)
            bit //= 2

    for wait in (False, True):
        lax.fori_loop(0, N_EXPERTS, lambda e, c, wait=wait: (fill(e, wait), c)[1], 0)


def _pad_fill_call(fill_start, fill_n, xs):
    bit_rows = 1 << ((EXPERT_ROWS - 1).bit_length() - 1)
    grid_spec = pltpu.PrefetchScalarGridSpec(
        num_scalar_prefetch=2,
        grid=(1,),
        in_specs=[pl.BlockSpec(memory_space=pl.ANY)],
        out_specs=pl.BlockSpec(memory_space=pl.ANY),
        scratch_shapes=[pltpu.VMEM((bit_rows, xs.shape[1]), xs.dtype), pltpu.SemaphoreType.DMA],
    )
    return pl.pallas_call(
        _pad_fill_kernel,
        grid_spec=grid_spec,
        out_shape=jax.ShapeDtypeStruct(xs.shape, xs.dtype),
        input_output_aliases={2: 0},
        compiler_params=pltpu.CompilerParams(dimension_semantics=("arbitrary",)),
        name="pad_fill",
    )(fill_start, fill_n, xs)


def _expert_kernel(be_ref, bv_ref, bi_ref, nx_ref, x_ref, wgu_hbm, bgu_ref, wd_hbm, bd_ref, y_ref,
                   wgu_st, wd_st, wgu_bf, wd_bf, wsem):
    del bi_ref
    i = pl.program_id(0)
    e = be_ref[i]
    e_prev = be_ref[jnp.maximum(i - 1, 0)]
    valid = bv_ref[i]

    def weight_copies(ex):
        return (pltpu.make_async_copy(wgu_hbm.at[ex], wgu_st, wsem.at[0]),
                pltpu.make_async_copy(wd_hbm.at[ex], wd_st, wsem.at[1]))

    @pl.when(i == 0)
    def _():
        for cp in weight_copies(e):
            cp.start()

    @pl.when(jnp.logical_and(valid > 0, jnp.logical_or(i == 0, e != e_prev)))
    def _():
        for cp in weight_copies(e):
            cp.wait()
        wgu_bf[...] = wgu_st[...].astype(BF16)
        wd_bf[...] = wd_st[...].astype(BF16)
        nxt = nx_ref[i]

        @pl.when(nxt >= 0)
        def _():
            for cp in weight_copies(nxt):
                cp.start()

    @pl.when(valid > 0)
    def _():
        x_lo, x_hi = _unpack_bf16_pairs(x_ref[...])
        xb = jnp.concatenate([x_lo.astype(BF16), x_hi.astype(BF16)], axis=1)
        hs = []
        for c in range(D_FF // (2 * MXU_DIM)):
            lo = c * 2 * MXU_DIM
            hi = lo + 2 * MXU_DIM
            g = _dot(xb, wgu_bf[:, lo:hi]) + bgu_ref[0, :, lo:hi]
            up = _dot(xb, wgu_bf[:, D_FF + lo:D_FF + hi]) + bgu_ref[0, :, D_FF + lo:D_FF + hi]
            g = jnp.minimum(g, SWIGLU_LIMIT)
            up = jnp.clip(up, -SWIGLU_LIMIT, SWIGLU_LIMIT)
            glu = g * jax.nn.sigmoid(g * SWIGLU_ALPHA)
            hs.append(((up + 1.0) * glu).astype(BF16))
        y_ref[...] = _pack_bf16_pairs(_dot(jnp.concatenate(hs, axis=1), wd_bf[...]) + bd_ref[0])


def _expert_call(block_e, block_valid, block_idx, block_next, xs, w_gu, b_gu, w_down, b_down):
    n_pad, dw = xs.shape
    d = 2 * dw
    bm = EXPERT_ROWS
    nb = n_pad // bm
    grid_spec = pltpu.PrefetchScalarGridSpec(
        num_scalar_prefetch=4,
        grid=(nb,),
        in_specs=[
            pl.BlockSpec((bm, dw), lambda i, be, bv, bi, nx: (bi[i], 0)),
            pl.BlockSpec(memory_space=pl.ANY),
            pl.BlockSpec((1, 1, 2 * D_FF), lambda i, be, bv, bi, nx: (be[i], 0, 0)),
            pl.BlockSpec(memory_space=pl.ANY),
            pl.BlockSpec((1, 1, d), lambda i, be, bv, bi, nx: (be[i], 0, 0)),
        ],
        out_specs=pl.BlockSpec((bm, dw), lambda i, be, bv, bi, nx: (bi[i], 0)),
        scratch_shapes=[
            pltpu.VMEM((d, 2 * D_FF), F32),
            pltpu.VMEM((D_FF, d), F32),
            pltpu.VMEM((d, 2 * D_FF), BF16),
            pltpu.VMEM((D_FF, d), BF16),
            pltpu.SemaphoreType.DMA((2,)),
        ],
    )
    return pl.pallas_call(
        _expert_kernel,
        grid_spec=grid_spec,
        out_shape=jax.ShapeDtypeStruct((n_pad, dw), jnp.uint32),
        compiler_params=pltpu.CompilerParams(
            dimension_semantics=("arbitrary",), vmem_limit_bytes=VMEM_LIMIT_BYTES),
        name="expert_ffn",
    )(block_e, block_valid, block_idx, block_next, xs, w_gu, b_gu, w_down, b_down)


def _combine_kernel(x1_ref, gate_ref, y4_ref, g_ref, out_ref):
    x1 = x1_ref[...]
    n = x1.shape[1] // 2
    gate = gate_ref[...]
    acc_lo, acc_hi = x1[:, :n], x1[:, n:]
    for k in range(TOP_K):
        y_lo, y_hi = _unpack_bf16_pairs(y4_ref[k])
        acc_lo = acc_lo + gate[:, k:k + 1] * y_lo
        acc_hi = acc_hi + gate[:, k:k + 1] * y_hi
    out_ref[...] = _rms(jnp.concatenate([acc_lo, acc_hi], axis=1), g_ref[...])


def _combine_into_kernel(x1_ref, gate_ref, y4_ref, g_ref, prev_ref, out_ref):
    del prev_ref
    _combine_kernel(x1_ref, gate_ref, y4_ref, g_ref, out_ref)


def _combine_call(x1, gate_t, y4_part, g_final, part, out_prev):
    t, d = x1.shape
    tt = COMBINE_ROWS
    nb = t // TOKEN_PARTS // tt
    off = part * nb
    in_specs = [
        pl.BlockSpec((tt, d), lambda i: (off + i, 0)),
        pl.BlockSpec((tt, TOP_K), lambda i: (off + i, 0)),
        pl.BlockSpec((TOP_K, tt, d // 2), lambda i: (0, i, 0)),
        pl.BlockSpec((1, d), lambda i: (0, 0)),
    ]
    args = [x1, gate_t, y4_part, g_final]
    body = _combine_kernel
    aliases = {}
    if out_prev is not None:
        in_specs.append(pl.BlockSpec(memory_space=pl.ANY))
        args.append(out_prev)
        aliases = {len(args) - 1: 0}
        body = _combine_into_kernel
    return pl.pallas_call(
        body,
        grid=(nb,),
        in_specs=in_specs,
        out_specs=pl.BlockSpec((tt, d), lambda i: (off + i, 0)),
        out_shape=jax.ShapeDtypeStruct((t, d), F32),
        input_output_aliases=aliases,
        compiler_params=pltpu.CompilerParams(dimension_semantics=("arbitrary",)),
        name="combine_norm",
    )(*args)


def _block_diag(w):
    hh, n, _ = w.shape
    eye = jnp.eye(hh, dtype=w.dtype)
    return jnp.einsum("hij,hg->higj", w, eye).reshape(hh * n, hh * n)


def kernel(x, mem, norm_mix_g, w_in, conv_w, conv_b, w_rg_a, b_rg_a, w_rg_i, b_rg_i, lru_lambda, w_pool,
           pool_scale, mem_norm_g, w_mem_kv, w_out, norm_ffn_g, w_router, b_router, w_gu, b_gu, w_down,
           b_down, final_norm_g):
    b, s, d = x.shape
    t = b * s
    l = 0
    row = lambda v: v.reshape(1, -1)

    kv = _kv_call(mem, row(mem_norm_g[l]), w_mem_kv[l].astype(BF16))
    kh = kv[..., :D_XATTN].reshape(b, N_MEM, XATTN_HEADS, XATTN_HEAD_DIM)
    vh = kv[..., D_XATTN:].reshape(b, N_MEM, XATTN_HEADS, XATTN_HEAD_DIM)
    eye_h = jnp.eye(XATTN_HEADS, dtype=F32)
    kbd = jnp.einsum("bmhd,hg->bhdgm", kh, eye_h).reshape(b, D_XATTN, XATTN_HEADS * N_MEM).astype(BF16)
    vbd = jnp.einsum("bmhd,hg->bgmhd", vh, eye_h).reshape(b, XATTN_HEADS * N_MEM, D_XATTN).astype(BF16)

    heads_per = MXU_DIM // RNN_HEAD_DIM
    wg = jnp.stack([
        jnp.concatenate([_block_diag(w_rg_a[l, c * heads_per:(c + 1) * heads_per]),
                         _block_diag(w_rg_i[l, c * heads_per:(c + 1) * heads_per])], axis=1)
        for c in range(D_RNN // MXU_DIM)]).astype(BF16)

    x1, hn, idx, gate = _mix_call(
        x, row(norm_mix_g[l]), w_in[l].astype(BF16), conv_w[l], row(conv_b[l]), wg, row(b_rg_a[l]),
        row(b_rg_i[l]), row(lru_lambda[l]), _block_diag(w_pool[l]).astype(BF16), row(pool_scale[l]), kbd, vbd,
        w_out[l].astype(BF16), row(norm_ffn_g[l]), w_router[l].T.astype(BF16), b_router[l].reshape(-1, 1))

    hn = hn.reshape(t, d // 2)
    idx = jnp.transpose(idx, (1, 0, 2)).reshape(TOP_K, t)
    gate = jnp.transpose(gate, (1, 0, 2)).reshape(TOP_K, t)
    n_blocks = _num_blocks(t)
    dest, meta = _route_call(idx)
    be, bv, bi, nx = meta[0, :n_blocks], meta[1, :n_blocks], meta[2, :n_blocks], meta[6, :n_blocks]
    fill_start, fill_n = meta[3, :N_EXPERTS], meta[4, :N_EXPERTS]

    xs = _sc_dispatch_call(hn, dest, n_blocks * EXPERT_ROWS)
    xs = _pad_fill_call(fill_start, fill_n, xs)
    ys = _expert_call(be, bv, bi, nx, xs, w_gu[l], b_gu[l].reshape(N_EXPERTS, 1, -1), w_down[l],
                      b_down[l].reshape(N_EXPERTS, 1, -1))
    x1 = x1.reshape(t, d)
    gate_t = gate.T
    out = None
    for part in range(TOKEN_PARTS):
        y4 = _sc_gather_call(ys, dest, part)
        out = _combine_call(x1, gate_t, y4, row(final_norm_g), part, out)
    return out.reshape(b, s, d)
```

```python
import functools
import math

import jax
import jax.numpy as jnp
from jax import lax
from jax.experimental import pallas as pl
from jax.experimental.pallas import tpu as pltpu
from jax.experimental.pallas import tpu_sc as plsc

D_MODEL = 1024
N_MEM = 256
D_RNN = 512
RNN_HEADS = 8
RNN_HEAD_DIM = D_RNN // RNN_HEADS
CONV_WIDTH = 4
LRU_C = 8.0
D_POOL = 256
POOL_WINDOWS = (2, 4, 8, 16)
POOL_GROUP_DIM = D_POOL // len(POOL_WINDOWS)
MAX_WINDOW = max(POOL_WINDOWS)
D_XATTN = 256
XATTN_HEADS = 4
XATTN_HEAD_DIM = D_XATTN // XATTN_HEADS
D_IN = 2 * D_RNN + D_POOL + D_XATTN
N_EXPERTS = 32
TOP_K = 4
D_FF = D_MODEL
SWIGLU_ALPHA = 1.702
SWIGLU_LIMIT = 7.0
RMS_EPS = 1e-6
SQRT_GUARD = 1e-30

SUBLANES = 8
MXU_DIM = 256
VMEM_LIMIT_BYTES = 56 * 1024 * 1024

MIX_ROWS = 512
HIST = 16
EXPERT_ROWS = 528
COMBINE_ROWS = 512
ROUTE_CHUNK = 1024
SC_ROWS = 64
TOKEN_PARTS = 1
MIX_PHASE_STAGES = (1 + D_IN // MXU_DIM, XATTN_HEADS + 10, D_MODEL // MXU_DIM + 2)

BF16 = jnp.bfloat16
F32 = jnp.float32


def _rms(xf, g):
    return xf * lax.rsqrt(jnp.mean(xf * xf, axis=-1, keepdims=True) + RMS_EPS) * g


def _dot(a, b):
    return jnp.dot(a, b, preferred_element_type=F32)


def _pack_bf16_pairs(v):
    n = v.shape[1] // 2
    bits = lax.bitcast_convert_type(v.astype(BF16).astype(F32), jnp.uint32)
    return (bits[:, :n] >> 16) | (bits[:, n:] & jnp.uint32(0xFFFF0000))


def _unpack_bf16_pairs(w):
    lo = lax.bitcast_convert_type(w << 16, F32)
    hi = lax.bitcast_convert_type(w & jnp.uint32(0xFFFF0000), F32)
    return lo, hi


def _kv_kernel(mem_ref, g_ref, w_ref, kv_ref):
    mn = _rms(mem_ref[0], g_ref[...])
    kv_ref[0] = _dot(mn.astype(BF16), w_ref[...])


def _kv_call(mem, g, w_kv_bf):
    b = mem.shape[0]
    return pl.pallas_call(
        _kv_kernel,
        grid=(b,),
        in_specs=[
            pl.BlockSpec((1, N_MEM, D_MODEL), lambda i: (i, 0, 0)),
            pl.BlockSpec((1, D_MODEL), lambda i: (0, 0)),
            pl.BlockSpec((D_MODEL, 2 * D_XATTN), lambda i: (0, 0)),
        ],
        out_specs=pl.BlockSpec((1, N_MEM, 2 * D_XATTN), lambda i: (i, 0, 0)),
        out_shape=jax.ShapeDtypeStruct((b, N_MEM, 2 * D_XATTN), F32),
        name="kv_proj",
    )(mem, g, w_kv_bf)


def _mix_kernel(x_ref, g_mix_ref, w_in_ref, conv_w_ref, conv_b_ref, wg_ref, b_a_ref, b_i_ref, lam_ref,
                w_pool_ref, pool_scale_ref, kbd_ref, vbd_ref, w_out_ref, g_ffn_ref, w_rt_ref, b_r_ref,
                x1_ref, hn_ref, idx_ref, gate_ref,
                urnn_buf, upool_buf, hcar, mixed_s):
    tiles = [
        _mix_tile(x_ref.at[b], g_mix_ref, w_in_ref, conv_w_ref, conv_b_ref, wg_ref, b_a_ref, b_i_ref, lam_ref,
                  w_pool_ref, pool_scale_ref, kbd_ref.at[b], vbd_ref.at[b], w_out_ref, g_ffn_ref, w_rt_ref,
                  b_r_ref, x1_ref.at[b], hn_ref.at[b], idx_ref.at[b], gate_ref.at[b],
                  urnn_buf.at[b], upool_buf.at[b], hcar.at[b], mixed_s.at[b])
        for b in range(x_ref.shape[0])]
    n_phase = len(MIX_PHASE_STAGES)
    for step in range(len(tiles) + n_phase - 1):
        active = [(tiles[j], MIX_PHASE_STAGES[step - j]) for j in range(len(tiles)) if 0 <= step - j < n_phase]
        ticks = max(n for _, n in active)
        done = [0] * len(active)
        for i in range(ticks):
            for a, (tile, n) in enumerate(active):
                while done[a] * ticks < (i + 1) * n:
                    next(tile, None)
                    done[a] += 1
    done_mark = object()
    assert all(next(tile, done_mark) is done_mark for tile in tiles), "MIX_PHASE_STAGES does not cover _mix_tile"


def _mix_tile(x_ref, g_mix_ref, w_in_ref, conv_w_ref, conv_b_ref, wg_ref, b_a_ref, b_i_ref, lam_ref,
              w_pool_ref, pool_scale_ref, kbd_ref, vbd_ref, w_out_ref, g_ffn_ref, w_rt_ref, b_r_ref,
              x1_ref, hn_ref, idx_ref, gate_ref,
              urnn_buf, upool_buf, hcar, mixed_s):
    tm = MIX_ROWS
    cw = MXU_DIM
    s = pl.program_id(0)

    @pl.when(s == 0)
    def _():
        urnn_buf[0:HIST, :] = jnp.zeros((HIST, D_RNN), F32)
        upool_buf[0:HIST, :] = jnp.zeros((HIST, D_POOL), F32)
        hcar[...] = jnp.zeros((1, D_RNN), F32)

    x = x_ref[...]
    hb = _rms(x, g_mix_ref[...]).astype(BF16)
    yield
    proj = []
    for c in range(D_IN // cw):
        proj.append(_dot(hb, w_in_ref[:, c * cw:(c + 1) * cw]))
        yield
    urnn_buf[HIST:, :] = jnp.concatenate(proj[:D_RNN // cw], axis=1)
    g_rnn = jnp.concatenate(proj[D_RNN // cw:2 * D_RNN // cw], axis=1)
    upool_buf[HIST:, :] = proj[2 * D_RNN // cw]
    q_mem = proj[2 * D_RNN // cw + 1]

    row = lax.broadcasted_iota(jnp.int32, (tm, 1), 0)
    grow = row + s * tm

    e = upool_buf[...]
    s2 = e + pltpu.roll(e, 1, axis=0)
    s4 = s2 + pltpu.roll(s2, 2, axis=0)
    s8 = s4 + pltpu.roll(s4, 4, axis=0)
    s16 = s8 + pltpu.roll(s8, 8, axis=0)
    grp = lax.broadcasted_iota(jnp.int32, (tm, D_POOL), 1) // POOL_GROUP_DIM
    wsum = jnp.where(grp == 0, s2[HIST:], jnp.where(grp == 1, s4[HIST:], jnp.where(grp == 2, s8[HIST:], s16[HIST:])))
    win = jnp.where(grp == 0, 2.0, jnp.where(grp == 1, 4.0, jnp.where(grp == 2, 8.0, 16.0)))
    pos = (grow + 1).astype(F32)
    dpool = wsum / jnp.minimum(pos, win) - e[HIST:]
    y_pool = _dot(dpool.astype(BF16), w_pool_ref[...]) * pool_scale_ref[...]
    mixed_s[:, D_RNN:D_RNN + D_POOL] = y_pool.astype(BF16)
    yield

    sc = _dot(q_mem.astype(BF16), kbd_ref[...]) * (XATTN_HEAD_DIM ** -0.5)
    ps = []
    for hh in range(XATTN_HEADS):
        sh = sc[:, hh * N_MEM:(hh + 1) * N_MEM]
        ph = jnp.exp(sh - jnp.max(sh, axis=-1, keepdims=True))
        ps.append((ph * (1.0 / jnp.sum(ph, axis=-1, keepdims=True))).astype(BF16))
        yield
    y_mem = _dot(jnp.concatenate(ps, axis=1), vbd_ref[...])
    mixed_s[:, D_RNN + D_POOL:] = y_mem.astype(BF16)
    yield

    uc = conv_b_ref[...] + conv_w_ref[CONV_WIDTH - 1:CONV_WIDTH, :] * urnn_buf[pl.ds(HIST, tm), :]
    for k in range(CONV_WIDTH - 1):
        off = HIST - (CONV_WIDTH - 1) + k
        uc = uc + conv_w_ref[k:k + 1, :] * urnn_buf[pl.ds(off, tm), :]
    yield

    ucb = uc.astype(BF16)
    pre = [_dot(ucb[:, c * cw:(c + 1) * cw], wg_ref[c]) for c in range(D_RNN // cw)]
    pre_a = jnp.concatenate([p[:, :cw] for p in pre], axis=1)
    pre_i = jnp.concatenate([p[:, cw:] for p in pre], axis=1)
    r = jax.nn.sigmoid(pre_a + b_a_ref[...])
    ig = jax.nn.sigmoid(pre_i + b_i_ref[...])
    yield
    lam = lam_ref[...]
    softplus_neg = jnp.maximum(-lam, 0.0) + jnp.log(1.0 + jnp.exp(-jnp.abs(lam)))
    log_a = (-LRU_C) * r * softplus_neg
    a = jnp.exp(log_a)
    z = jnp.maximum(1.0 - a * a, 0.0)
    mult = z * lax.rsqrt(jnp.maximum(z, SQRT_GUARD))
    mult = jnp.where(grow == 0, 1.0, mult)
    bx = mult * (ig * uc)
    yield

    ng = tm // SUBLANES
    a = a.reshape(ng, SUBLANES, D_RNN)
    bx = bx.reshape(ng, SUBLANES, D_RNN)
    row8 = lax.broadcasted_iota(jnp.int32, (1, SUBLANES, D_RNN), 1)
    d = 1
    while d < SUBLANES:
        a_sh = pltpu.roll(a, d, axis=1)
        b_sh = pltpu.roll(bx, d, axis=1)
        m = row8 >= d
        bx = jnp.where(m, a * b_sh + bx, bx)
        a = jnp.where(m, a * a_sh, a)
        d *= 2
        yield
    carry = jnp.broadcast_to(hcar[...], (SUBLANES, D_RNN))
    hs = []
    for g in range(ng):
        hv = a[g] * carry + bx[g]
        hs.append(hv)
        carry = jnp.broadcast_to(hv[SUBLANES - 1:SUBLANES, :], (SUBLANES, D_RNN))
    hcar[...] = carry[0:1, :]
    yield
    y_rnn = jnp.concatenate(hs, axis=0) * jax.nn.gelu(g_rnn)
    mixed_s[:, :D_RNN] = y_rnn.astype(BF16)
    urnn_buf[0:HIST, :] = urnn_buf[tm:tm + HIST, :]
    upool_buf[0:HIST, :] = upool_buf[tm:tm + HIST, :]
    yield

    mixed = mixed_s[...]
    x1s = []
    for c in range(D_MODEL // cw):
        x1s.append(x[:, c * cw:(c + 1) * cw] + _dot(mixed, w_out_ref[:, c * cw:(c + 1) * cw]))
        yield
    x1 = jnp.concatenate(x1s, axis=1)
    x1_ref[...] = x1

    hn = _rms(x1, g_ffn_ref[...]).astype(BF16)
    hn_ref[...] = _pack_bf16_pairs(hn)
    yield
    logits =lax.dot_general(w_rt_ref[...], hn, (((1,), (1,)), ((), ())),
                             preferred_element_type=F32) + b_r_ref[...]
    eid = lax.broadcasted_iota(jnp.int32, (N_EXPERTS, tm), 0)
    vals, idxs = [], []
    for _ in range(TOP_K):
        mx = jnp.max(logits, axis=0, keepdims=True)
        ix = jnp.min(jnp.where(logits == mx, eid, N_EXPERTS), axis=0, keepdims=True)
        vals.append(mx)
        idxs.append(ix)
        logits = jnp.where(eid == ix, -jnp.inf, logits)
    ex = [jnp.exp(v - vals[0]) for v in vals]
    den = ex[0] + ex[1] + ex[2] + ex[3]
    idx_ref[...] = jnp.concatenate(idxs, axis=0)
    gate_ref[...] = jnp.concatenate([e_ / den for e_ in ex], axis=0)


def _mix_call(x, g_mix, w_in, conv_w, conv_b, wg, b_a, b_i, lam, w_pool, pool_scale, kbd, vbd, w_out,
              g_ffn, w_rt, b_r):
    b, s, d = x.shape
    tm = MIX_ROWS
    ns = s // tm
    const2 = lambda shape: pl.BlockSpec(shape, lambda si: (0, 0))
    const3 = lambda shape: pl.BlockSpec(shape, lambda si: (0, 0, 0))
    return pl.pallas_call(
        _mix_kernel,
        grid=(ns,),
        in_specs=[
            pl.BlockSpec((b, tm, d), lambda si: (0, si, 0)),
            const2((1, d)),
            const2((d, D_IN)),
            const2((CONV_WIDTH, D_RNN)),
            const2((1, D_RNN)),
            const3((D_RNN // MXU_DIM, MXU_DIM, 2 * MXU_DIM)),
            const2((1, D_RNN)),
            const2((1, D_RNN)),
            const2((1, D_RNN)),
            const2((D_POOL, D_POOL)),
            const2((1, D_POOL)),
            const3((b, D_XATTN, XATTN_HEADS * N_MEM)),
            const3((b, XATTN_HEADS * N_MEM, D_XATTN)),
            const2((d, d)),
            const2((1, d)),
            const2((N_EXPERTS, d)),
            const2((N_EXPERTS, 1)),
        ],
        out_specs=[
            pl.BlockSpec((b, tm, d), lambda si: (0, si, 0)),
            pl.BlockSpec((b, tm, d // 2), lambda si: (0, si, 0)),
            pl.BlockSpec((b, TOP_K, tm), lambda si: (0, 0, si)),
            pl.BlockSpec((b, TOP_K, tm), lambda si: (0, 0, si)),
        ],
        out_shape=[
            jax.ShapeDtypeStruct((b, s, d), F32),
            jax.ShapeDtypeStruct((b, s, d // 2), jnp.uint32),
            jax.ShapeDtypeStruct((b, TOP_K, s), jnp.int32),
            jax.ShapeDtypeStruct((b, TOP_K, s), F32),
        ],
        scratch_shapes=[
            pltpu.VMEM((b, HIST + tm, D_RNN), F32),
            pltpu.VMEM((b, HIST + tm, D_POOL), F32),
            pltpu.VMEM((b, 1, D_RNN), F32),
            pltpu.VMEM((b, tm, d), BF16),
        ],
        compiler_params=pltpu.CompilerParams(
            dimension_semantics=("arbitrary",), vmem_limit_bytes=VMEM_LIMIT_BYTES),
        name="mixer_router",
    )(x, g_mix, w_in, conv_w, conv_b, wg, b_a, b_i, lam, w_pool, pool_scale, kbd, vbd, w_out, g_ffn, w_rt, b_r)


def _num_blocks(t):
    bm = EXPERT_ROWS
    return (t * TOP_K + N_EXPERTS * (bm - 1) + bm - 1) // bm


def _route_kernel(idx_ref, dest_ref, meta_ref, rank_s, carry_s, start_s, tri_s):
    p = pl.program_id(0)
    c = pl.program_id(1)
    nc = pl.num_programs(1)
    ch = ROUTE_CHUNK
    bm = float(EXPERT_ROWS)
    nbl = meta_ref.shape[1]
    eid = lax.broadcasted_iota(jnp.int32, (N_EXPERTS, ch), 0)
    idxc = idx_ref[...]

    @pl.when(jnp.logical_and(p == 0, c == 0))
    def _():
        carry_s[...] = jnp.zeros(carry_s.shape, F32)
        tri_s[...] = (lax.broadcasted_iota(jnp.int32, (ch, ch), 0)
                      < lax.broadcasted_iota(jnp.int32, (ch, ch), 1)).astype(BF16)

    @pl.when(p == 0)
    def _():
        sel = jnp.zeros((N_EXPERTS, ch), F32)
        for k in range(TOP_K):
            sel = sel + (idxc[k:k + 1, :] == eid).astype(F32)
        rank_s[c] = _dot(sel.astype(BF16), tri_s[...]) + carry_s[:, 0:1]
        carry_s[...] = carry_s[...] + jnp.sum(sel, axis=1, keepdims=True)

    @pl.when(jnp.logical_and(p == 0, c == nc - 1))
    def _():
        counts = carry_s[...]

        def div_bm(v):
            q = jnp.floor(v * (1.0 / bm))
            return q + jnp.where((q + 1.0) * bm <= v, 1.0, 0.0) - jnp.where(q * bm > v, 1.0, 0.0)

        padded = div_bm(counts + (bm - 1.0)) * bm
        e128 = lax.broadcasted_iota(jnp.int32, counts.shape, 0)
        pad_end = padded
        sh = 1
        while sh < N_EXPERTS:
            pad_end = pad_end + jnp.where(e128 >= sh, pltpu.roll(pad_end, sh, axis=0), 0.0)
            sh *= 2
        pad_start = pad_end - padded
        start_s[...] = pad_start
        total = pad_end[N_EXPERTS - 1:N_EXPERTS, 0:1]
        lane = lax.broadcasted_iota(jnp.int32, (1, nbl), 1)
        lane_f = lane.astype(F32)
        n_used = div_bm(total)
        bclamp = jnp.minimum(lane_f, n_used - 1.0)
        bstart = bclamp * bm
        pe, ps, cn, pd = pad_end[:, 0:1], pad_start[:, 0:1], counts[:, 0:1], padded[:, 0:1]
        be = jnp.minimum(jnp.sum((pe <= bstart).astype(F32), axis=0, keepdims=True), N_EXPERTS - 1.0)
        esub = lax.broadcasted_iota(jnp.int32, (N_EXPERTS, nbl), 0)
        onehot = esub.astype(F32) == be
        bv = jnp.sum(jnp.where(onehot, cn - (bstart - ps), 0.0), axis=0, keepdims=True)
        bv = jnp.where(lane_f * bm < total, jnp.clip(bv, 0.0, bm), 0.0)
        own = esub == lane
        fill_start = jnp.sum(jnp.where(own, ps + cn, 0.0), axis=0, keepdims=True)
        fill_n = jnp.sum(jnp.where(own, pd - cn, 0.0), axis=0, keepdims=True)
        later = jnp.logical_and(esub.astype(F32) > be, cn > 0.0)
        nxt = jnp.min(jnp.where(later, esub.astype(F32), float(N_EXPERTS)), axis=0, keepdims=True)
        nxt = jnp.where(nxt == float(N_EXPERTS), -1.0, nxt)
        zero = jnp.zeros((1, nbl), F32)
        meta_ref[...] = jnp.concatenate(
            [be, bv, bclamp, fill_start, fill_n, jnp.broadcast_to(n_used, (1, nbl)), nxt, zero],
            axis=0).astype(jnp.int32)

    @pl.when(p == 1)
    def _():
        base = start_s[:, 0:1] + rank_s[c]
        rows = [jnp.sum(jnp.where(idxc[k:k + 1, :] == eid, base, 0.0), axis=0, keepdims=True)
                for k in range(TOP_K)]
        dest_ref[...] = jnp.concatenate(rows, axis=0).astype(jnp.int32)


def _route_call(idx):
    k, t = idx.shape
    ch = ROUTE_CHUNK
    nc = t // ch
    nbl = -(-_num_blocks(t) // 128) * 128
    return pl.pallas_call(
        _route_kernel,
        grid=(2, nc),
        in_specs=[pl.BlockSpec((k, ch), lambda p, c: (0, c))],
        out_specs=[
            pl.BlockSpec((k, ch), lambda p, c: (0, c * p)),
            pl.BlockSpec((8, nbl), lambda p, c: (0, 0)),
        ],
        out_shape=[
            jax.ShapeDtypeStruct((k, t), jnp.int32),
            jax.ShapeDtypeStruct((8, nbl), jnp.int32),
        ],
        scratch_shapes=[
            pltpu.VMEM((nc, N_EXPERTS, ch), F32),
            pltpu.VMEM((N_EXPERTS, 128), F32),
            pltpu.VMEM((N_EXPERTS, 128), F32),
            pltpu.VMEM((ch, ch), BF16),
        ],
        compiler_params=pltpu.CompilerParams(dimension_semantics=("arbitrary", "arbitrary")),
        name="route_tables",
    )(idx)


def _sc_workers():
    info = plsc.get_sparse_core_info()
    return info.num_cores, info.num_cores * info.num_subcores


def _sc_dispatch_call(hn, dest, n_rows):
    t, w = hn.shape
    nc, nw = _sc_workers()
    per_w = t // nw
    ch = SC_ROWS
    mesh = plsc.VectorSubcoreMesh(core_axis_name="c", subcore_axis_name="s")

    n_chunks = per_w // ch

    @functools.partial(
        pl.kernel, mesh=mesh,
        out_type=jax.ShapeDtypeStruct((n_rows, w), hn.dtype),
        scratch_types=[pltpu.VMEM((ch,), jnp.int32)] * (2 * TOP_K) + [pltpu.VMEM((ch, w), hn.dtype)] * 2
        + [pltpu.SemaphoreType.DMA] * 4,
    )
    def body(hn_hbm, dest_hbm, xs_hbm, *scratch):
        idx = (scratch[:TOP_K], scratch[TOP_K:2 * TOP_K])
        rows = scratch[2 * TOP_K:2 * TOP_K + 2]
        read_sem = scratch[2 * TOP_K + 2:2 * TOP_K + 4]
        scatter_sem = scratch[2 * TOP_K + 4:]
        wid = lax.axis_index("s") * nc + lax.axis_index("c")
        base = pl.multiple_of(wid * per_w, per_w)

        def load(c):
            for k in range(TOP_K):
                pltpu.sync_copy(dest_hbm.at[k, pl.ds(base + c * ch, ch)], idx[c % 2][k])
            return pltpu.async_copy(hn_hbm.at[pl.ds(base + c * ch, ch)], rows[c % 2], read_sem[c % 2])

        reads = {0: load(0)}
        for c in range(n_chunks):
            reads[c].wait()
            if c + 1 < n_chunks:
                reads[c + 1] = load(c + 1)
            scatters = [pltpu.async_copy(rows[c % 2], xs_hbm.at[idx[c % 2][k]], scatter_sem[c % 2])
                        for k in range(TOP_K)]
            for cp in scatters:
                cp.wait()

    return body(hn, dest)


def _sc_gather_call(ys, dest, part):
    k_, t = dest.shape
    tp = t // TOKEN_PARTS
    w = ys.shape[1]
    nc, nw = _sc_workers()
    per_w = tp // nw
    ch = SC_ROWS
    mesh = plsc.VectorSubcoreMesh(core_axis_name="c", subcore_axis_name="s")

    units = [(k, c) for c in range(per_w // ch) for k in range(k_)]

    @functools.partial(
        pl.kernel, mesh=mesh,
        out_type=jax.ShapeDtypeStruct((k_, tp, w), ys.dtype),
        scratch_types=[pltpu.VMEM((per_w,), jnp.int32)] * k_ + [pltpu.VMEM((ch, w), ys.dtype)] * 2
        + [pltpu.SemaphoreType.DMA] * 4,
    )
    def body(ys_hbm, dest_hbm, y4_hbm, *scratch):
        idx = scratch[:k_]
        rows = scratch[k_:k_ + 2]
        gather_sem = scratch[k_ + 2:k_ + 4]
        write_sem = scratch[k_ + 4:]
        wid = lax.axis_index("s") * nc + lax.axis_index("c")
        base = pl.multiple_of(wid * per_w, per_w)
        for k in range(k_):
            pltpu.sync_copy(dest_hbm.at[k, pl.ds(part * tp + base, per_w)], idx[k])

        def gather(u):
            k, c = units[u]
            return pltpu.async_copy(ys_hbm.at[idx[k].at[pl.ds(c * ch, ch)]], rows[u % 2], gather_sem[u % 2])

        def write(u):
            k, c = units[u]
            return pltpu.async_copy(rows[u % 2], y4_hbm.at[k, pl.ds(base + c * ch, ch)], write_sem[u % 2])

        gathers = {0: gather(0)}
        writes = {}
        for u in range(len(units)):
            gathers[u].wait()
            if u >= 1:
                writes[u - 1].wait()
            if u + 1 < len(units):
                gathers[u + 1] = gather(u + 1)
            writes[u] = write(u)
        writes[len(units) - 1].wait()

    return body(ys, dest)


def _pad_fill_kernel(fill_start_ref, fill_n_ref, xs_in_ref, xs_ref, zero_s, zsem):
    del xs_in_ref
    bm = EXPERT_ROWS
    zero_s[...] = jnp.zeros(zero_s.shape, zero_s.dtype)

    def go(cp, wait):
        if wait:
            cp.wait()
        else:
            cp.start()

    def fill(e, wait):
        n = fill_n_ref[e]
        st = fill_start_ref[e]
        head = n & (SUBLANES - 1)
        for j in range(SUBLANES - 1):
            cp = pltpu.make_async_copy(zero_s.at[pl.ds(0, 1), :], xs_ref.at[pl.ds(st + j, 1), :], zsem)
            pl.when(j < head)(functools.partial(go, cp, wait))
        st = pl.multiple_of(st + head, SUBLANES)
        bit = 1 << ((bm - 1).bit_length() - 1)
        while bit >= SUBLANES:
            cp = pltpu.make_async_copy(zero_s.at[pl.ds(0, bit), :], xs_ref.at[pl.ds(st, bit), :], zsem)
            pl.when((n & bit) != 0)(functools.partial(go, cp, wait))
            st = pl.multiple_of(st + (n & bit), SUBLANES)
            bit //= 2

    for wait in (False, True):
        lax.fori_loop(0, N_EXPERTS, lambda e, c, wait=wait: (fill(e, wait), c)[1], 0)


def _pad_fill_call(fill_start, fill_n, xs):
    bit_rows = 1 << ((EXPERT_ROWS - 1).bit_length() - 1)
    grid_spec = pltpu.PrefetchScalarGridSpec(
        num_scalar_prefetch=2,
        grid=(1,),
        in_specs=[pl.BlockSpec(memory_space=pl.ANY)],
        out_specs=pl.BlockSpec(memory_space=pl.ANY),
        scratch_shapes=[pltpu.VMEM((bit_rows, xs.shape[1]), xs.dtype), pltpu.SemaphoreType.DMA],
    )
    return pl.pallas_call(
        _pad_fill_kernel,
        grid_spec=grid_spec,
        out_shape=jax.ShapeDtypeStruct(xs.shape, xs.dtype),
        input_output_aliases={2: 0},
        compiler_params=pltpu.CompilerParams(dimension_semantics=("arbitrary",)),
        name="pad_fill",
    )(fill_start, fill_n, xs)


def _expert_kernel(be_ref, bv_ref, bi_ref, nx_ref, x_ref, wgu_hbm, bgu_ref, wd_hbm, bd_ref, y_ref,
                   wgu_st, wd_st, wgu_bf, wd_bf, wsem):
    del bi_ref
    i = pl.program_id(0)
    e = be_ref[i]
    e_prev = be_ref[jnp.maximum(i - 1, 0)]
    valid = bv_ref[i]

    def weight_copies(ex):
        return (pltpu.make_async_copy(wgu_hbm.at[ex], wgu_st, wsem.at[0]),
                pltpu.make_async_copy(wd_hbm.at[ex], wd_st, wsem.at[1]))

    @pl.when(i == 0)
    def _():
        for cp in weight_copies(e):
            cp.start()

    @pl.when(jnp.logical_and(valid > 0, jnp.logical_or(i == 0, e != e_prev)))
    def _():
        for cp in weight_copies(e):
            cp.wait()
        wgu_bf[...] = wgu_st[...].astype(BF16)
        wd_bf[...] = wd_st[...].astype(BF16)
        nxt = nx_ref[i]

        @pl.when(nxt >= 0)
        def _():
            for cp in weight_copies(nxt):
                cp.start()

    @pl.when(valid > 0)
    def _():
        x_lo, x_hi = _unpack_bf16_pairs(x_ref[...])
        xb = jnp.concatenate([x_lo.astype(BF16), x_hi.astype(BF16)], axis=1)
        hs = []
        for c in range(D_FF // (2 * MXU_DIM)):
            lo = c * 2 * MXU_DIM
            hi = lo + 2 * MXU_DIM
            g = _dot(xb, wgu_bf[:, lo:hi]) + bgu_ref[0, :, lo:hi]
            up = _dot(xb, wgu_bf[:, D_FF + lo:D_FF + hi]) + bgu_ref[0, :, D_FF + lo:D_FF + hi]
            g = jnp.minimum(g, SWIGLU_LIMIT)
            up = jnp.clip(up, -SWIGLU_LIMIT, SWIGLU_LIMIT)
            glu = g * jax.nn.sigmoid(g * SWIGLU_ALPHA)
            hs.append(((up + 1.0) * glu).astype(BF16))
        y_ref[...] = _pack_bf16_pairs(_dot(jnp.concatenate(hs, axis=1), wd_bf[...]) + bd_ref[0])


def _expert_call(block_e, block_valid, block_idx, block_next, xs, w_gu, b_gu, w_down, b_down):
    n_pad, dw = xs.shape
    d = 2 * dw
    bm = EXPERT_ROWS
    nb = n_pad // bm
    grid_spec = pltpu.PrefetchScalarGridSpec(
        num_scalar_prefetch=4,
        grid=(nb,),
        in_specs=[
            pl.BlockSpec((bm, dw), lambda i, be, bv, bi, nx: (bi[i], 0)),
            pl.BlockSpec(memory_space=pl.ANY),
            pl.BlockSpec((1, 1, 2 * D_FF), lambda i, be, bv, bi, nx: (be[i], 0, 0)),
            pl.BlockSpec(memory_space=pl.ANY),
            pl.BlockSpec((1, 1, d), lambda i, be, bv, bi, nx: (be[i], 0, 0)),
        ],
        out_specs=pl.BlockSpec((bm, dw), lambda i, be, bv, bi, nx: (bi[i], 0)),
        scratch_shapes=[
            pltpu.VMEM((d, 2 * D_FF), F32),
            pltpu.VMEM((D_FF, d), F32),
            pltpu.VMEM((d, 2 * D_FF), BF16),
            pltpu.VMEM((D_FF, d), BF16),
            pltpu.SemaphoreType.DMA((2,)),
        ],
    )
    return pl.pallas_call(
        _expert_kernel,
        grid_spec=grid_spec,
        out_shape=jax.ShapeDtypeStruct((n_pad, dw), jnp.uint32),
        compiler_params=pltpu.CompilerParams(
            dimension_semantics=("arbitrary",), vmem_limit_bytes=VMEM_LIMIT_BYTES),
        name="expert_ffn",
    )(block_e, block_valid, block_idx, block_next, xs, w_gu, b_gu, w_down, b_down)


def _combine_kernel(x1_ref, gate_ref, y4_ref, g_ref, out_ref):
    x1 = x1_ref[...]
    n = x1.shape[1] // 2
    gate = gate_ref[...]
    acc_lo, acc_hi = x1[:, :n], x1[:, n:]
    for k in range(TOP_K):
        y_lo, y_hi = _unpack_bf16_pairs(y4_ref[k])
        acc_lo = acc_lo + gate[:, k:k + 1] * y_lo
        acc_hi = acc_hi + gate[:, k:k + 1] * y_hi
    out_ref[...] = _rms(jnp.concatenate([acc_lo, acc_hi], axis=1), g_ref[...])


def _combine_into_kernel(x1_ref, gate_ref, y4_ref, g_ref, prev_ref, out_ref):
    del prev_ref
    _combine_kernel(x1_ref, gate_ref, y4_ref, g_ref, out_ref)


def _combine_call(x1, gate_t, y4_part, g_final, part, out_prev):
    t, d = x1.shape
    tt = COMBINE_ROWS
    nb = t // TOKEN_PARTS // tt
    off = part * nb
    in_specs = [
        pl.BlockSpec((tt, d), lambda i: (off + i, 0)),
        pl.BlockSpec((tt, TOP_K), lambda i: (off + i, 0)),
        pl.BlockSpec((TOP_K, tt, d // 2), lambda i: (0, i, 0)),
        pl.BlockSpec((1, d), lambda i: (0, 0)),
    ]
    args = [x1, gate_t, y4_part, g_final]
    body = _combine_kernel
    aliases = {}
    if out_prev is not None:
        in_specs.append(pl.BlockSpec(memory_space=pl.ANY))
        args.append(out_prev)
        aliases = {len(args) - 1: 0}
        body = _combine_into_kernel
    return pl.pallas_call(
        body,
        grid=(nb,),
        in_specs=in_specs,
        out_specs=pl.BlockSpec((tt, d), lambda i: (off + i, 0)),
        out_shape=jax.ShapeDtypeStruct((t, d), F32),
        input_output_aliases=aliases,
        compiler_params=pltpu.CompilerParams(dimension_semantics=("arbitrary",)),
        name="combine_norm",
    )(*args)


def _block_diag(w):
    hh, n, _ = w.shape
    eye = jnp.eye(hh, dtype=w.dtype)
    return jnp.einsum("hij,hg->higj", w, eye).reshape(hh * n, hh * n)


def kernel(x, mem, norm_mix_g, w_in, conv_w, conv_b, w_rg_a, b_rg_a, w_rg_i, b_rg_i, lru_lambda, w_pool,
           pool_scale, mem_norm_g, w_mem_kv, w_out, norm_ffn_g, w_router, b_router, w_gu, b_gu, w_down,
           b_down, final_norm_g):
    b, s, d = x.shape
    t = b * s
    l = 0
    row = lambda v: v.reshape(1, -1)

    kv = _kv_call(mem, row(mem_norm_g[l]), w_mem_kv[l].astype(BF16))
    kh = kv[..., :D_XATTN].reshape(b, N_MEM, XATTN_HEADS, XATTN_HEAD_DIM)
    vh = kv[..., D_XATTN:].reshape(b, N_MEM, XATTN_HEADS, XATTN_HEAD_DIM)
    eye_h = jnp.eye(XATTN_HEADS, dtype=F32)
    kbd = jnp.einsum("bmhd,hg->bhdgm", kh, eye_h).reshape(b, D_XATTN, XATTN_HEADS * N_MEM).astype(BF16)
    vbd = jnp.einsum("bmhd,hg->bgmhd", vh, eye_h).reshape(b, XATTN_HEADS * N_MEM, D_XATTN).astype(BF16)

    heads_per = MXU_DIM // RNN_HEAD_DIM
    wg = jnp.stack([
        jnp.concatenate([_block_diag(w_rg_a[l, c * heads_per:(c + 1) * heads_per]),
                         _block_diag(w_rg_i[l, c * heads_per:(c + 1) * heads_per])], axis=1)
        for c in range(D_RNN // MXU_DIM)]).astype(BF16)

    x1, hn, idx, gate = _mix_call(
        x, row(norm_mix_g[l]), w_in[l].astype(BF16), conv_w[l], row(conv_b[l]), wg, row(b_rg_a[l]),
        row(b_rg_i[l]), row(lru_lambda[l]), _block_diag(w_pool[l]).astype(BF16), row(pool_scale[l]), kbd, vbd,
        w_out[l].astype(BF16), row(norm_ffn_g[l]), w_router[l].T.astype(BF16), b_router[l].reshape(-1, 1))

    hn = hn.reshape(t, d // 2)
    idx = jnp.transpose(idx, (1, 0, 2)).reshape(TOP_K, t)
    gate = jnp.transpose(gate, (1, 0, 2)).reshape(TOP_K, t)
    n_blocks = _num_blocks(t)
    dest, meta = _route_call(idx)
    be, bv, bi, nx = meta[0, :n_blocks], meta[1, :n_blocks], meta[2, :n_blocks], meta[6, :n_blocks]
    fill_start, fill_n = meta[3, :N_EXPERTS], meta[4, :N_EXPERTS]

    xs = _sc_dispatch_call(hn, dest, n_blocks * EXPERT_ROWS)
    xs = _pad_fill_call(fill_start, fill_n, xs)
    ys = _expert_call(be, bv, bi, nx, xs, w_gu[l], b_gu[l].reshape(N_EXPERTS, 1, -1), w_down[l],
                      b_down[l].reshape(N_EXPERTS, 1, -1))
    x1 = x1.reshape(t, d)
    gate_t = gate.T
    out = None
    for part in range(TOKEN_PARTS):
        y4 = _sc_gather_call(ys, dest, part)
        out = _combine_call(x1, gate_t, y4, row(final_norm_g), part, out)
    return out.reshape(b, s, d)
```

```python
import functools

import jax
import jax.numpy as jnp
from jax import lax
from jax.experimental import pallas as pl
from jax.experimental.pallas import tpu as pltpu
from jax.experimental.pallas import tpu_sc as plsc

D_MODEL = 1024
N_MEM = 256
D_RNN = 512
RNN_HEADS = 8
RNN_HEAD_DIM = D_RNN // RNN_HEADS
CONV_WIDTH = 4
LRU_C = 8.0
D_POOL = 256
POOL_WINDOWS = (2, 4, 8, 16)
POOL_GROUP_DIM = D_POOL // len(POOL_WINDOWS)
MAX_WINDOW = max(POOL_WINDOWS)
D_XATTN = 256
XATTN_HEADS = 4
XATTN_HEAD_DIM = D_XATTN // XATTN_HEADS
D_IN = 2 * D_RNN + D_POOL + D_XATTN
N_EXPERTS = 32
TOP_K = 4
D_FF = D_MODEL
SWIGLU_ALPHA = 1.702
SWIGLU_LIMIT = 7.0
RMS_EPS = 1e-6
SQRT_GUARD = 1e-30

SUBLANES = 8
MXU_DIM = 256
VMEM_LIMIT_BYTES = 56 * 1024 * 1024

MIX_ROWS = 512
HIST = 16
assert HIST % SUBLANES == 0 and HIST >= max(MAX_WINDOW, CONV_WIDTH) - 1
EXPERT_ROWS = 528
COMBINE_ROWS = 512
ROUTE_CHUNK = 1024
SC_ROWS = 128
MIX_PHASE_STAGES = (1 + D_IN // MXU_DIM, XATTN_HEADS + 10, D_MODEL // MXU_DIM + 2)

BF16 = jnp.bfloat16
F32 = jnp.float32


def _rms(xf, g):
    return xf * lax.rsqrt(jnp.mean(xf * xf, axis=-1, keepdims=True) + RMS_EPS) * g


def _dot(a, b):
    return jnp.dot(a, b, preferred_element_type=F32)


def _pack_bf16_pairs(v):
    n = v.shape[1] // 2
    bits = lax.bitcast_convert_type(v.astype(BF16).astype(F32), jnp.uint32)
    return (bits[:, :n] >> 16) | (bits[:, n:] & jnp.uint32(0xFFFF0000))


def _unpack_bf16_pairs(w):
    lo = lax.bitcast_convert_type(w << 16, F32)
    hi = lax.bitcast_convert_type(w & jnp.uint32(0xFFFF0000), F32)
    return lo, hi


def _kv_kernel(mem_ref, g_ref, w_ref, kv_ref):
    mn = _rms(mem_ref[0], g_ref[...])
    kv_ref[0] = _dot(mn.astype(BF16), w_ref[...])


def _kv_call(mem, g, w_kv_bf):
    b = mem.shape[0]
    return pl.pallas_call(
        _kv_kernel,
        grid=(b,),
        in_specs=[
            pl.BlockSpec((1, N_MEM, D_MODEL), lambda i: (i, 0, 0)),
            pl.BlockSpec((1, D_MODEL), lambda i: (0, 0)),
            pl.BlockSpec((D_MODEL, 2 * D_XATTN), lambda i: (0, 0)),
        ],
        out_specs=pl.BlockSpec((1, N_MEM, 2 * D_XATTN), lambda i: (i, 0, 0)),
        out_shape=jax.ShapeDtypeStruct((b, N_MEM, 2 * D_XATTN), F32),
        name="kv_proj",
    )(mem, g, w_kv_bf)


def _mix_kernel(x_ref, g_mix_ref, w_in_ref, conv_w_ref, conv_b_ref, wg_ref, b_a_ref, b_i_ref, lam_ref,
                w_pool_ref, pool_scale_ref, kbd_ref, vbd_ref, w_out_ref, g_ffn_ref, w_rt_ref, b_r_ref,
                x1_ref, hn_ref, idx_ref, gate_ref,
                urnn_buf, upool_buf, hcar, mixed_s):
    tiles = [
        _mix_tile(x_ref.at[b], g_mix_ref, w_in_ref, conv_w_ref, conv_b_ref, wg_ref, b_a_ref, b_i_ref, lam_ref,
                  w_pool_ref, pool_scale_ref, kbd_ref.at[b], vbd_ref.at[b], w_out_ref, g_ffn_ref, w_rt_ref,
                  b_r_ref, x1_ref.at[b], hn_ref.at[b], idx_ref.at[b], gate_ref.at[b],
                  urnn_buf.at[b], upool_buf.at[b], hcar.at[b], mixed_s.at[b])
        for b in range(x_ref.shape[0])]
    n_phase = len(MIX_PHASE_STAGES)
    for step in range(len(tiles) + n_phase - 1):
        active = [(tiles[j], MIX_PHASE_STAGES[step - j]) for j in range(len(tiles)) if 0 <= step - j < n_phase]
        ticks = max(n for _, n in active)
        done = [0] * len(active)
        for i in range(ticks):
            for a, (tile, n) in enumerate(active):
                while done[a] * ticks < (i + 1) * n:
                    next(tile, None)
                    done[a] += 1
    done_mark = object()
    assert all(next(tile, done_mark) is done_mark for tile in tiles), "MIX_PHASE_STAGES does not cover _mix_tile"


def _mix_tile(x_ref, g_mix_ref, w_in_ref, conv_w_ref, conv_b_ref, wg_ref, b_a_ref, b_i_ref, lam_ref,
              w_pool_ref, pool_scale_ref, kbd_ref, vbd_ref, w_out_ref, g_ffn_ref, w_rt_ref, b_r_ref,
              x1_ref, hn_ref, idx_ref, gate_ref,
              urnn_buf, upool_buf, hcar, mixed_s):
    tm = MIX_ROWS
    cw = MXU_DIM
    s = pl.program_id(0)

    @pl.when(s == 0)
    def _():
        urnn_buf[0:HIST, :] = jnp.zeros((HIST, D_RNN), F32)
        upool_buf[0:HIST, :] = jnp.zeros((HIST, D_POOL), F32)
        hcar[...] = jnp.zeros((1, D_RNN), F32)

    x = x_ref[...]
    hb = _rms(x, g_mix_ref[...]).astype(BF16)
    yield
    proj = []
    for c in range(D_IN // cw):
        proj.append(_dot(hb, w_in_ref[:, c * cw:(c + 1) * cw]))
        yield
    urnn_buf[HIST:, :] = jnp.concatenate(proj[:D_RNN // cw], axis=1)
    g_rnn = jnp.concatenate(proj[D_RNN // cw:2 * D_RNN // cw], axis=1)
    upool_buf[HIST:, :] = proj[2 * D_RNN // cw]
    q_mem = proj[2 * D_RNN // cw + 1]

    row = lax.broadcasted_iota(jnp.int32, (tm, 1), 0)
    grow = row + s * tm

    e = upool_buf[...]
    s2 = e + pltpu.roll(e, 1, axis=0)
    s4 = s2 + pltpu.roll(s2, 2, axis=0)
    s8 = s4 + pltpu.roll(s4, 4, axis=0)
    s16 = s8 + pltpu.roll(s8, 8, axis=0)
    grp = lax.broadcasted_iota(jnp.int32, (tm, D_POOL), 1) // POOL_GROUP_DIM
    wsum = jnp.where(grp == 0, s2[HIST:], jnp.where(grp == 1, s4[HIST:], jnp.where(grp == 2, s8[HIST:], s16[HIST:])))
    win = jnp.where(grp == 0, 2.0, jnp.where(grp == 1, 4.0, jnp.where(grp == 2, 8.0, 16.0)))
    pos = (grow + 1).astype(F32)
    dpool = wsum / jnp.minimum(pos, win) - e[HIST:]
    y_pool = _dot(dpool.astype(BF16), w_pool_ref[...]) * pool_scale_ref[...]
    mixed_s[:, D_RNN:D_RNN + D_POOL] = y_pool.astype(BF16)
    yield

    sc = _dot(q_mem.astype(BF16), kbd_ref[...]) * (XATTN_HEAD_DIM ** -0.5)
    ps = []
    for hh in range(XATTN_HEADS):
        sh = sc[:, hh * N_MEM:(hh + 1) * N_MEM]
        ph = jnp.exp(sh - jnp.max(sh, axis=-1, keepdims=True))
        ps.append((ph * (1.0 / jnp.sum(ph, axis=-1, keepdims=True))).astype(BF16))
        yield
    y_mem = _dot(jnp.concatenate(ps, axis=1), vbd_ref[...])
    mixed_s[:, D_RNN + D_POOL:] = y_mem.astype(BF16)
    yield

    uc = conv_b_ref[...] + conv_w_ref[CONV_WIDTH - 1:CONV_WIDTH, :] * urnn_buf[pl.ds(HIST, tm), :]
    for k in range(CONV_WIDTH - 1):
        off = HIST - (CONV_WIDTH - 1) + k
        uc = uc + conv_w_ref[k:k + 1, :] * urnn_buf[pl.ds(off, tm), :]
    yield

    ucb = uc.astype(BF16)
    pre = [_dot(ucb[:, c * cw:(c + 1) * cw], wg_ref[c]) for c in range(D_RNN // cw)]
    pre_a = jnp.concatenate([p[:, :cw] for p in pre], axis=1)
    pre_i = jnp.concatenate([p[:, cw:] for p in pre], axis=1)
    r = jax.nn.sigmoid(pre_a + b_a_ref[...])
    ig = jax.nn.sigmoid(pre_i + b_i_ref[...])
    yield
    lam = lam_ref[...]
    softplus_neg = jnp.maximum(-lam, 0.0) + jnp.log(1.0 + jnp.exp(-jnp.abs(lam)))
    log_a = (-LRU_C) * r * softplus_neg
    a = jnp.exp(log_a)
    z = jnp.maximum(1.0 - a * a, 0.0)
    mult = z * lax.rsqrt(jnp.maximum(z, SQRT_GUARD))
    mult = jnp.where(grow == 0, 1.0, mult)
    bx = mult * (ig * uc)
    yield

    ng = tm // SUBLANES
    a = a.reshape(ng, SUBLANES, D_RNN)
    bx = bx.reshape(ng, SUBLANES, D_RNN)
    row8 = lax.broadcasted_iota(jnp.int32, (1, SUBLANES, D_RNN), 1)
    d = 1
    while d < SUBLANES:
        a_sh = pltpu.roll(a, d, axis=1)
        b_sh = pltpu.roll(bx, d, axis=1)
        m = row8 >= d
        bx = jnp.where(m, a * b_sh + bx, bx)
        a = jnp.where(m, a * a_sh, a)
        d *= 2
        yield
    carry = jnp.broadcast_to(hcar[...], (SUBLANES, D_RNN))
    hs = []
    for g in range(ng):
        hv = a[g] * carry + bx[g]
        hs.append(hv)
        carry = jnp.broadcast_to(hv[SUBLANES - 1:SUBLANES, :], (SUBLANES, D_RNN))
    hcar[...] = carry[0:1, :]
    yield
    y_rnn = jnp.concatenate(hs, axis=0) * jax.nn.gelu(g_rnn)
    mixed_s[:, :D_RNN] = y_rnn.astype(BF16)
    urnn_buf[0:HIST, :] = urnn_buf[tm:tm + HIST, :]
    upool_buf[0:HIST, :] = upool_buf[tm:tm + HIST, :]
    yield

    mixed = mixed_s[...]
    x1s = []
    for c in range(D_MODEL // cw):
        x1s.append(x[:, c * cw:(c + 1) * cw] + _dot(mixed, w_out_ref[:, c * cw:(c + 1) * cw]))
        yield
    x1 = jnp.concatenate(x1s, axis=1)
    x1_ref[...] = x1

    hn = _rms(x1, g_ffn_ref[...]).astype(BF16)
    hn_ref[...] = _pack_bf16_pairs(hn)
    yield
    logits = lax.dot_general(w_rt_ref[...], hn, (((1,), (1,)), ((), ())),
                             preferred_element_type=F32) + b_r_ref[...]
    eid = lax.broadcasted_iota(jnp.int32, (N_EXPERTS, tm), 0)
    vals, idxs = [], []
    for _ in range(TOP_K):
        mx = jnp.max(logits, axis=0, keepdims=True)
        ix = jnp.min(jnp.where(logits == mx, eid, N_EXPERTS), axis=0, keepdims=True)
        vals.append(mx)
        idxs.append(ix)
        logits = jnp.where(eid == ix, -jnp.inf, logits)
    ex = [jnp.exp(v - vals[0]) for v in vals]
    den = ex[0] + ex[1] + ex[2] + ex[3]
    idx_ref[...] = jnp.concatenate(idxs, axis=0)
    gate_ref[...] = jnp.concatenate([e_ / den for e_ in ex], axis=0)


def _mix_call(x, g_mix, w_in, conv_w, conv_b, wg, b_a, b_i, lam, w_pool, pool_scale, kbd, vbd, w_out,
              g_ffn, w_rt, b_r):
    b, s, d = x.shape
    tm = MIX_ROWS
    ns = s // tm
    const2 = lambda shape: pl.BlockSpec(shape, lambda si: (0, 0))
    const3 = lambda shape: pl.BlockSpec(shape, lambda si: (0, 0, 0))
    return pl.pallas_call(
        _mix_kernel,
        grid=(ns,),
        in_specs=[
            pl.BlockSpec((b, tm, d), lambda si: (0, si, 0)),
            const2((1, d)),
            const2((d, D_IN)),
            const2((CONV_WIDTH, D_RNN)),
            const2((1, D_RNN)),
            const3((D_RNN // MXU_DIM, MXU_DIM, 2 * MXU_DIM)),
            const2((1, D_RNN)),
            const2((1, D_RNN)),
            const2((1, D_RNN)),
            const2((D_POOL, D_POOL)),
            const2((1, D_POOL)),
            const3((b, D_XATTN, XATTN_HEADS * N_MEM)),
            const3((b, XATTN_HEADS * N_MEM, D_XATTN)),
            const2((d, d)),
            const2((1, d)),
            const2((N_EXPERTS, d)),
            const2((N_EXPERTS, 1)),
        ],
        out_specs=[
            pl.BlockSpec((b, tm, d), lambda si: (0, si, 0)),
            pl.BlockSpec((b, tm, d // 2), lambda si: (0, si, 0)),
            pl.BlockSpec((b, TOP_K, tm), lambda si: (0, 0, si)),
            pl.BlockSpec((b, TOP_K, tm), lambda si: (0, 0, si)),
        ],
        out_shape=[
            jax.ShapeDtypeStruct((b, s, d), F32),
            jax.ShapeDtypeStruct((b, s, d // 2), jnp.uint32),
            jax.ShapeDtypeStruct((b, TOP_K, s), jnp.int32),
            jax.ShapeDtypeStruct((b, TOP_K, s), F32),
        ],
        scratch_shapes=[
            pltpu.VMEM((b, HIST + tm, D_RNN), F32),
            pltpu.VMEM((b, HIST + tm, D_POOL), F32),
            pltpu.VMEM((b, 1, D_RNN), F32),
            pltpu.VMEM((b, tm, d), BF16),
        ],
        compiler_params=pltpu.CompilerParams(
            dimension_semantics=("arbitrary",), vmem_limit_bytes=VMEM_LIMIT_BYTES),
        name="mixer_router",
    )(x, g_mix, w_in, conv_w, conv_b, wg, b_a, b_i, lam, w_pool, pool_scale, kbd, vbd, w_out, g_ffn, w_rt, b_r)


def _num_blocks(t):
    bm = EXPERT_ROWS
    return (t * TOP_K + N_EXPERTS * (bm - 1) + bm - 1) // bm


def _route_kernel(idx_ref, dest_ref, meta_ref, rank_s, carry_s, start_s, tri_s):
    p = pl.program_id(0)
    c = pl.program_id(1)
    nc = pl.num_programs(1)
    ch = ROUTE_CHUNK
    bm = float(EXPERT_ROWS)
    nbl = meta_ref.shape[1]
    eid = lax.broadcasted_iota(jnp.int32, (N_EXPERTS, ch), 0)
    idxc = idx_ref[...]

    @pl.when(jnp.logical_and(p == 0, c == 0))
    def _():
        carry_s[...] = jnp.zeros(carry_s.shape, F32)
        tri_s[...] = (lax.broadcasted_iota(jnp.int32, (ch, ch), 0)
                      < lax.broadcasted_iota(jnp.int32, (ch, ch), 1)).astype(BF16)

    @pl.when(p == 0)
    def _():
        sel = jnp.zeros((N_EXPERTS, ch), F32)
        for k in range(TOP_K):
            sel = sel + (idxc[k:k + 1, :] == eid).astype(F32)
        rank_s[c] = _dot(sel.astype(BF16), tri_s[...]) + carry_s[:, 0:1]
        carry_s[...] = carry_s[...] + jnp.sum(sel, axis=1, keepdims=True)

    @pl.when(jnp.logical_and(p == 0, c == nc - 1))
    def _():
        counts = carry_s[...]

        def div_bm(v):
            q = jnp.floor(v * (1.0 / bm))
            return q + jnp.where((q + 1.0) * bm <= v, 1.0, 0.0) - jnp.where(q * bm > v, 1.0, 0.0)

        padded = div_bm(counts + (bm - 1.0)) * bm
        e128 = lax.broadcasted_iota(jnp.int32, counts.shape, 0)
        pad_end = padded
        sh = 1
        while sh < N_EXPERTS:
            pad_end = pad_end + jnp.where(e128 >= sh, pltpu.roll(pad_end, sh, axis=0), 0.0)
            sh *= 2
        pad_start = pad_end - padded
        start_s[...] = pad_start
        total = pad_end[N_EXPERTS - 1:N_EXPERTS, 0:1]
        lane = lax.broadcasted_iota(jnp.int32, (1, nbl), 1)
        lane_f = lane.astype(F32)
        n_used = div_bm(total)
        bclamp = jnp.minimum(lane_f, n_used - 1.0)
        bstart = bclamp * bm
        pe, ps, cn, pd = pad_end[:, 0:1], pad_start[:, 0:1], counts[:, 0:1], padded[:, 0:1]
        be = jnp.minimum(jnp.sum((pe <= bstart).astype(F32), axis=0, keepdims=True), N_EXPERTS - 1.0)
        esub = lax.broadcasted_iota(jnp.int32, (N_EXPERTS, nbl), 0)
        onehot = esub.astype(F32) == be
        bv = jnp.sum(jnp.where(onehot, cn - (bstart - ps), 0.0), axis=0, keepdims=True)
        bv = jnp.where(lane_f * bm < total, jnp.clip(bv, 0.0, bm), 0.0)
        own = esub == lane
        fill_start = jnp.sum(jnp.where(own, ps + cn, 0.0), axis=0, keepdims=True)
        fill_n = jnp.sum(jnp.where(own, pd - cn, 0.0), axis=0, keepdims=True)
        later = jnp.logical_and(esub.astype(F32) > be, cn > 0.0)
        nxt = jnp.min(jnp.where(later, esub.astype(F32), float(N_EXPERTS)), axis=0, keepdims=True)
        nxt = jnp.where(nxt == float(N_EXPERTS), -1.0, nxt)
        zero = jnp.zeros((1, nbl), F32)
        meta_ref[...] = jnp.concatenate(
            [be, bv, bclamp, fill_start, fill_n, jnp.broadcast_to(n_used, (1, nbl)), nxt, zero],
            axis=0).astype(jnp.int32)

    @pl.when(p == 1)
    def _():
        base = start_s[:, 0:1] + rank_s[c]
        rows = [jnp.sum(jnp.where(idxc[k:k + 1, :] == eid, base, 0.0), axis=0, keepdims=True)
                for k in range(TOP_K)]
        dest_ref[...] = jnp.concatenate(rows, axis=0).astype(jnp.int32)


def _route_call(idx):
    k, t = idx.shape
    ch = ROUTE_CHUNK
    nc = t // ch
    nbl = -(-_num_blocks(t) // 128) * 128
    return pl.pallas_call(
        _route_kernel,
        grid=(2, nc),
        in_specs=[pl.BlockSpec((k, ch), lambda p, c: (0, c))],
        out_specs=[
            pl.BlockSpec((k, ch), lambda p, c: (0, c * p)),
            pl.BlockSpec((8, nbl), lambda p, c: (0, 0)),
        ],
        out_shape=[
            jax.ShapeDtypeStruct((k, t), jnp.int32),
            jax.ShapeDtypeStruct((8, nbl), jnp.int32),
        ],
        scratch_shapes=[
            pltpu.VMEM((nc, N_EXPERTS, ch), F32),
            pltpu.VMEM((N_EXPERTS, 128), F32),
            pltpu.VMEM((N_EXPERTS, 128), F32),
            pltpu.VMEM((ch, ch), BF16),
        ],
        compiler_params=pltpu.CompilerParams(dimension_semantics=("arbitrary", "arbitrary")),
        name="route_tables",
    )(idx)


def _sc_workers():
    info = plsc.get_sparse_core_info()
    return info.num_cores, info.num_cores * info.num_subcores


def _sc_dispatch_call(hn, dest, n_rows):
    t, w = hn.shape
    nc, nw = _sc_workers()
    per_w = t // nw
    ch = SC_ROWS
    mesh = plsc.VectorSubcoreMesh(core_axis_name="c", subcore_axis_name="s")

    @functools.partial(
        pl.kernel, mesh=mesh,
        out_type=jax.ShapeDtypeStruct((n_rows, w), hn.dtype),
        scratch_types=[pltpu.VMEM((ch,), jnp.int32), pltpu.VMEM((ch, w), hn.dtype)],
    )
    def body(hn_hbm, dest_hbm, xs_hbm, idx_v, rows_v):
        wid = lax.axis_index("s") * nc + lax.axis_index("c")

        @pl.loop(0, per_w // ch)
        def _(ci):
            base = pl.multiple_of(wid * per_w + ci * ch, ch)
            pltpu.sync_copy(hn_hbm.at[pl.ds(base, ch)], rows_v)
            for k in range(TOP_K):
                pltpu.sync_copy(dest_hbm.at[k, pl.ds(base, ch)], idx_v)
                pltpu.sync_copy(rows_v, xs_hbm.at[idx_v])

    return body(hn, dest)


def _sc_gather_call(ys, dest):
    k_, t = dest.shape
    w = ys.shape[1]
    nc, nw = _sc_workers()
    per_w = t // nw
    ch = SC_ROWS
    mesh = plsc.VectorSubcoreMesh(core_axis_name="c", subcore_axis_name="s")

    @functools.partial(
        pl.kernel, mesh=mesh,
        out_type=jax.ShapeDtypeStruct((k_, t, w), ys.dtype),
        scratch_types=[pltpu.VMEM((ch,), jnp.int32), pltpu.VMEM((ch, w), ys.dtype)],
    )
    def body(ys_hbm, dest_hbm, y4_hbm, idx_v, rows_v):
        wid = lax.axis_index("s") * nc + lax.axis_index("c")

        @pl.loop(0, per_w // ch)
        def _(ci):
            base = pl.multiple_of(wid * per_w + ci * ch, ch)
            for k in range(k_):
                pltpu.sync_copy(dest_hbm.at[k, pl.ds(base, ch)], idx_v)
                pltpu.sync_copy(ys_hbm.at[idx_v], rows_v)
                pltpu.sync_copy(rows_v, y4_hbm.at[k, pl.ds(base, ch)])

    return body(ys, dest)


def _pad_fill_kernel(fill_start_ref, fill_n_ref, xs_in_ref, xs_ref, zero_s, zsem):
    del xs_in_ref
    bm = EXPERT_ROWS
    zero_s[...] = jnp.zeros(zero_s.shape, zero_s.dtype)

    def go(cp, wait):
        if wait:
            cp.wait()
        else:
            cp.start()

    def fill(e, wait):
        n = fill_n_ref[e]
        st = fill_start_ref[e]
        head = n & (SUBLANES - 1)
        for j in range(SUBLANES - 1):
            cp = pltpu.make_async_copy(zero_s.at[pl.ds(0, 1), :], xs_ref.at[pl.ds(st + j, 1), :], zsem)
            pl.when(j < head)(functools.partial(go, cp, wait))
        st = pl.multiple_of(st + head, SUBLANES)
        bit = 1 << ((bm - 1).bit_length() - 1)
        while bit >= SUBLANES:
            cp = pltpu.make_async_copy(zero_s.at[pl.ds(0, bit), :], xs_ref.at[pl.ds(st, bit), :], zsem)
            pl.when((n & bit) != 0)(functools.partial(go, cp, wait))
            st = pl.multiple_of(st + (n & bit), SUBLANES)
            bit //= 2

    for wait in (False, True):
        lax.fori_loop(0, N_EXPERTS, lambda e, c, wait=wait: (fill(e, wait), c)[1], 0)


def _pad_fill_call(fill_start, fill_n, xs):
    bit_rows = 1 << ((EXPERT_ROWS - 1).bit_length() - 1)
    grid_spec = pltpu.PrefetchScalarGridSpec(
        num_scalar_prefetch=2,
        grid=(1,),
        in_specs=[pl.BlockSpec(memory_space=pl.ANY)],
        out_specs=pl.BlockSpec(memory_space=pl.ANY),
        scratch_shapes=[pltpu.VMEM((bit_rows, xs.shape[1]), xs.dtype), pltpu.SemaphoreType.DMA],
    )
    return pl.pallas_call(
        _pad_fill_kernel,
        grid_spec=grid_spec,
        out_shape=jax.ShapeDtypeStruct(xs.shape, xs.dtype),
        input_output_aliases={2: 0},
        compiler_params=pltpu.CompilerParams(dimension_semantics=("arbitrary",)),
        name="pad_fill",
    )(fill_start, fill_n, xs)


def _expert_kernel(be_ref, bv_ref, bi_ref, nx_ref, x_ref, wgu_hbm, bgu_ref, wd_hbm, bd_ref, y_ref,
                   wgu_st, wd_st, wgu_bf, wd_bf, wsem):
    del bi_ref
    i = pl.program_id(0)
    e = be_ref[i]
    e_prev = be_ref[jnp.maximum(i - 1, 0)]
    valid = bv_ref[i]

    def weight_copies(ex):
        return (pltpu.make_async_copy(wgu_hbm.at[ex], wgu_st, wsem.at[0]),
                pltpu.make_async_copy(wd_hbm.at[ex], wd_st, wsem.at[1]))

    @pl.when(i == 0)
    def _():
        for cp in weight_copies(e):
            cp.start()

    @pl.when(jnp.logical_and(valid > 0, jnp.logical_or(i == 0, e != e_prev)))
    def _():
        for cp in weight_copies(e):
            cp.wait()
        wgu_bf[...] = wgu_st[...].astype(BF16)
        wd_bf[...] = wd_st[...].astype(BF16)
        nxt = nx_ref[i]

        @pl.when(nxt >= 0)
        def _():
            for cp in weight_copies(nxt):
                cp.start()

    @pl.when(valid > 0)
    def _():
        x_lo, x_hi = _unpack_bf16_pairs(x_ref[...])
        xb = jnp.concatenate([x_lo.astype(BF16), x_hi.astype(BF16)], axis=1)
        hs = []
        for c in range(D_FF // (2 * MXU_DIM)):
            lo = c * 2 * MXU_DIM
            hi = lo + 2 * MXU_DIM
            g = _dot(xb, wgu_bf[:, lo:hi]) + bgu_ref[0, :, lo:hi]
            up = _dot(xb, wgu_bf[:, D_FF + lo:D_FF + hi]) + bgu_ref[0, :, D_FF + lo:D_FF + hi]
            g = jnp.minimum(g, SWIGLU_LIMIT)
            up = jnp.clip(up, -SWIGLU_LIMIT, SWIGLU_LIMIT)
            glu = g * jax.nn.sigmoid(g * SWIGLU_ALPHA)
            hs.append(((up + 1.0) * glu).astype(BF16))
        y_ref[...] = _pack_bf16_pairs(_dot(jnp.concatenate(hs, axis=1), wd_bf[...]) + bd_ref[0])


def _expert_call(block_e, block_valid, block_idx, block_next, xs, w_gu, b_gu, w_down, b_down):
    n_pad, dw = xs.shape
    d = 2 * dw
    bm = EXPERT_ROWS
    nb = n_pad // bm
    grid_spec = pltpu.PrefetchScalarGridSpec(
        num_scalar_prefetch=4,
        grid=(nb,),
        in_specs=[
            pl.BlockSpec((bm, dw), lambda i, be, bv, bi, nx: (bi[i], 0)),
            pl.BlockSpec(memory_space=pl.ANY),
            pl.BlockSpec((1, 1, 2 * D_FF), lambda i, be, bv, bi, nx: (be[i], 0, 0)),
            pl.BlockSpec(memory_space=pl.ANY),
            pl.BlockSpec((1, 1, d), lambda i, be, bv, bi, nx: (be[i], 0, 0)),
        ],
        out_specs=pl.BlockSpec((bm, dw), lambda i, be, bv, bi, nx: (bi[i], 0)),
        scratch_shapes=[
            pltpu.VMEM((d, 2 * D_FF), F32),
            pltpu.VMEM((D_FF, d), F32),
            pltpu.VMEM((d, 2 * D_FF), BF16),
            pltpu.VMEM((D_FF, d), BF16),
            pltpu.SemaphoreType.DMA((2,)),
        ],
    )
    return pl.pallas_call(
        _expert_kernel,
        grid_spec=grid_spec,
        out_shape=jax.ShapeDtypeStruct((n_pad, dw), jnp.uint32),
        compiler_params=pltpu.CompilerParams(
            dimension_semantics=("arbitrary",), vmem_limit_bytes=VMEM_LIMIT_BYTES),
        name="expert_ffn",
    )(block_e, block_valid, block_idx, block_next, xs, w_gu, b_gu, w_down, b_down)


def _combine_kernel(x1_ref, gate_ref, y4_ref, g_ref, out_ref):
    x1 = x1_ref[...]
    n = x1.shape[1] // 2
    gate = gate_ref[...]
    acc_lo, acc_hi = x1[:, :n], x1[:, n:]
    for k in range(TOP_K):
        y_lo, y_hi = _unpack_bf16_pairs(y4_ref[k])
        acc_lo = acc_lo + gate[:, k:k + 1] * y_lo
        acc_hi = acc_hi + gate[:, k:k + 1] * y_hi
    out_ref[...] = _rms(jnp.concatenate([acc_lo, acc_hi], axis=1), g_ref[...])


def _combine_call(x1, gate_t, y4, g_final):
    t, d = x1.shape
    tt = COMBINE_ROWS
    return pl.pallas_call(
        _combine_kernel,
        grid=(t // tt,),
        in_specs=[
            pl.BlockSpec((tt, d), lambda i: (i, 0)),
            pl.BlockSpec((tt, TOP_K), lambda i: (i, 0)),
            pl.BlockSpec((TOP_K, tt, d // 2), lambda i: (0, i, 0)),
            pl.BlockSpec((1, d), lambda i: (0, 0)),
        ],
        out_specs=pl.BlockSpec((tt, d), lambda i: (i, 0)),
        out_shape=jax.ShapeDtypeStruct((t, d), F32),
        compiler_params=pltpu.CompilerParams(dimension_semantics=("arbitrary",)),
        name="combine_norm",
    )(x1, gate_t, y4, g_final)


def _block_diag(w):
    hh, n, _ = w.shape
    eye = jnp.eye(hh, dtype=w.dtype)
    return jnp.einsum("hij,hg->higj", w, eye).reshape(hh * n, hh * n)


def kernel(x, mem, norm_mix_g, w_in, conv_w, conv_b, w_rg_a, b_rg_a, w_rg_i, b_rg_i, lru_lambda, w_pool,
           pool_scale, mem_norm_g, w_mem_kv, w_out, norm_ffn_g, w_router, b_router, w_gu, b_gu, w_down,
           b_down, final_norm_g):
    b, s, d = x.shape
    t = b * s
    l = 0
    row = lambda v: v.reshape(1, -1)

    kv = _kv_call(mem, row(mem_norm_g[l]), w_mem_kv[l].astype(BF16))
    kh = kv[..., :D_XATTN].reshape(b, N_MEM, XATTN_HEADS, XATTN_HEAD_DIM)
    vh = kv[..., D_XATTN:].reshape(b, N_MEM, XATTN_HEADS, XATTN_HEAD_DIM)
    eye_h = jnp.eye(XATTN_HEADS, dtype=F32)
    kbd = jnp.einsum("bmhd,hg->bhdgm", kh, eye_h).reshape(b, D_XATTN, XATTN_HEADS * N_MEM).astype(BF16)
    vbd = jnp.einsum("bmhd,hg->bgmhd", vh, eye_h).reshape(b, XATTN_HEADS * N_MEM, D_XATTN).astype(BF16)

    heads_per = MXU_DIM // RNN_HEAD_DIM
    wg = jnp.stack([
        jnp.concatenate([_block_diag(w_rg_a[l, c * heads_per:(c + 1) * heads_per]),
                         _block_diag(w_rg_i[l, c * heads_per:(c + 1) * heads_per])], axis=1)
        for c in range(D_RNN // MXU_DIM)]).astype(BF16)

    x1, hn, idx, gate = _mix_call(
        x, row(norm_mix_g[l]), w_in[l].astype(BF16), conv_w[l], row(conv_b[l]), wg, row(b_rg_a[l]),
        row(b_rg_i[l]), row(lru_lambda[l]), _block_diag(w_pool[l]).astype(BF16), row(pool_scale[l]), kbd, vbd,
        w_out[l].astype(BF16), row(norm_ffn_g[l]), w_router[l].T.astype(BF16), b_router[l].reshape(-1, 1))

    hn = hn.reshape(t, d // 2)
    idx = jnp.transpose(idx, (1, 0, 2)).reshape(TOP_K, t)
    gate_t = jnp.transpose(gate, (0, 2, 1)).reshape(t, TOP_K)
    n_blocks = _num_blocks(t)
    dest, meta = _route_call(idx)
    be, bv, bi, nx = (meta[r, :n_blocks] for r in (0, 1, 2, 6))
    fill_start, fill_n = meta[3, :N_EXPERTS], meta[4, :N_EXPERTS]

    xs = _sc_dispatch_call(hn, dest, n_blocks * EXPERT_ROWS)
    xs = _pad_fill_call(fill_start, fill_n, xs)
    ys = _expert_call(be, bv, bi, nx, xs, w_gu[l], b_gu[l].reshape(N_EXPERTS, 1, -1), w_down[l],
                      b_down[l].reshape(N_EXPERTS, 1, -1))
    y4 = _sc_gather_call(ys, dest)
    out = _combine_call(x1.reshape(t, d), gate_t, y4, row(final_norm_g))
    return out.reshape(b, s, d)
```

```python
import functools

import jax
import jax.numpy as jnp
from jax import lax
from jax.experimental import pallas as pl
from jax.experimental.pallas import tpu as pltpu
from jax.experimental.pallas import tpu_sc as plsc

D_MODEL = 1024
N_MEM = 256
D_RNN = 512
RNN_HEADS = 8
RNN_HEAD_DIM = D_RNN // RNN_HEADS
CONV_WIDTH = 4
LRU_C = 8.0
D_POOL = 256
POOL_WINDOWS = (2, 4, 8, 16)
POOL_GROUP_DIM = D_POOL // len(POOL_WINDOWS)
MAX_WINDOW = max(POOL_WINDOWS)
D_XATTN = 256
XATTN_HEADS = 4
XATTN_HEAD_DIM = D_XATTN // XATTN_HEADS
D_IN = 2 * D_RNN + D_POOL + D_XATTN
N_EXPERTS = 32
TOP_K = 4
D_FF = D_MODEL
SWIGLU_ALPHA = 1.702
SWIGLU_LIMIT = 7.0
RMS_EPS = 1e-6
SQRT_GUARD = 1e-30

SUBLANES = 8
MXU_DIM = 256
VMEM_LIMIT_BYTES = 56 * 1024 * 1024

MIX_ROWS = 512
HIST = 16
assert HIST % SUBLANES == 0 and HIST >= max(MAX_WINDOW, CONV_WIDTH) - 1
EXPERT_ROWS = 528
COMBINE_ROWS = 512
ROUTE_CHUNK = 1024
SC_ROWS = 128
MIX_PHASE_STAGES = (1 + D_IN // MXU_DIM, XATTN_HEADS + 10, D_MODEL // MXU_DIM + 2)

BF16 = jnp.bfloat16
F32 = jnp.float32


def _rms(xf, g):
    return xf * lax.rsqrt(jnp.mean(xf * xf, axis=-1, keepdims=True) + RMS_EPS) * g


def _dot(a, b):
    return jnp.dot(a, b, preferred_element_type=F32)


def _pack_bf16_pairs(v):
    n = v.shape[1] // 2
    bits = lax.bitcast_convert_type(v.astype(BF16).astype(F32), jnp.uint32)
    return (bits[:, :n] >> 16) | (bits[:, n:] & jnp.uint32(0xFFFF0000))


def _unpack_bf16_pairs(w):
    lo = lax.bitcast_convert_type(w << 16, F32)
    hi = lax.bitcast_convert_type(w & jnp.uint32(0xFFFF0000), F32)
    return lo, hi


def _kv_kernel(mem_ref, g_ref, w_ref, kv_ref):
    mn = _rms(mem_ref[0], g_ref[...])
    kv_ref[0] = _dot(mn.astype(BF16), w_ref[...])


def _kv_call(mem, g, w_kv_bf):
    b = mem.shape[0]
    return pl.pallas_call(
        _kv_kernel,
        grid=(b,),
        in_specs=[
            pl.BlockSpec((1, N_MEM, D_MODEL), lambda i: (i, 0, 0)),
            pl.BlockSpec((1, D_MODEL), lambda i: (0, 0)),
            pl.BlockSpec((D_MODEL, 2 * D_XATTN), lambda i: (0, 0)),
        ],
        out_specs=pl.BlockSpec((1, N_MEM, 2 * D_XATTN), lambda i: (i, 0, 0)),
        out_shape=jax.ShapeDtypeStruct((b, N_MEM, 2 * D_XATTN), F32),
        name="kv_proj",
    )(mem, g, w_kv_bf)


def _mix_kernel(x_ref, g_mix_ref, w_in_ref, conv_w_ref, conv_b_ref, wg_ref, b_a_ref, b_i_ref, lam_ref,
                w_pool_ref, pool_scale_ref, kbd_ref, vbd_ref, w_out_ref, g_ffn_ref, w_rt_ref, b_r_ref,
                x1_ref, hn_ref, idx_ref, gate_ref,
                urnn_buf, upool_buf, hcar, mixed_s):
    tiles = [
        _mix_tile(x_ref.at[b], g_mix_ref, w_in_ref, conv_w_ref, conv_b_ref, wg_ref, b_a_ref, b_i_ref, lam_ref,
                  w_pool_ref, pool_scale_ref, kbd_ref.at[b], vbd_ref.at[b], w_out_ref, g_ffn_ref, w_rt_ref,
                  b_r_ref, x1_ref.at[b], hn_ref.at[b], idx_ref.at[b], gate_ref.at[b],
                  urnn_buf.at[b], upool_buf.at[b], hcar.at[b], mixed_s.at[b])
        for b in range(x_ref.shape[0])]
    n_phase = len(MIX_PHASE_STAGES)
    for step in range(len(tiles) + n_phase - 1):
        active = [(tiles[j], MIX_PHASE_STAGES[step - j]) for j in range(len(tiles)) if 0 <= step - j < n_phase]
        ticks = max(n for _, n in active)
        done = [0] * len(active)
        for i in range(ticks):
            for a, (tile, n) in enumerate(active):
                while done[a] * ticks < (i + 1) * n:
                    next(tile, None)
                    done[a] += 1
    done_mark = object()
    assert all(next(tile, done_mark) is done_mark for tile in tiles), "MIX_PHASE_STAGES does not cover _mix_tile"


def _mix_tile(x_ref, g_mix_ref, w_in_ref, conv_w_ref, conv_b_ref, wg_ref, b_a_ref, b_i_ref, lam_ref,
              w_pool_ref, pool_scale_ref, kbd_ref, vbd_ref, w_out_ref, g_ffn_ref, w_rt_ref, b_r_ref,
              x1_ref, hn_ref, idx_ref, gate_ref,
              urnn_buf, upool_buf, hcar, mixed_s):
    tm = MIX_ROWS
    cw = MXU_DIM
    s = pl.program_id(0)

    @pl.when(s == 0)
    def _():
        urnn_buf[0:HIST, :] = jnp.zeros((HIST, D_RNN), F32)
        upool_buf[0:HIST, :] = jnp.zeros((HIST, D_POOL), F32)
        hcar[...] = jnp.zeros((1, D_RNN), F32)

    x = x_ref[...]
    hb = _rms(x, g_mix_ref[...]).astype(BF16)
    yield
    proj = []
    for c in range(D_IN // cw):
        proj.append(_dot(hb, w_in_ref[:, c * cw:(c + 1) * cw]))
        yield
    urnn_buf[HIST:, :] = jnp.concatenate(proj[:D_RNN // cw], axis=1)
    g_rnn = jnp.concatenate(proj[D_RNN // cw:2 * D_RNN // cw], axis=1)
    upool_buf[HIST:, :] = proj[2 * D_RNN // cw]
    q_mem = proj[2 * D_RNN // cw + 1]

    row = lax.broadcasted_iota(jnp.int32, (tm, 1), 0)
    grow = row + s * tm

    e = upool_buf[...]
    s2 = e + pltpu.roll(e, 1, axis=0)
    s4 = s2 + pltpu.roll(s2, 2, axis=0)
    s8 = s4 + pltpu.roll(s4, 4, axis=0)
    s16 = s8 + pltpu.roll(s8, 8, axis=0)
    grp = lax.broadcasted_iota(jnp.int32, (tm, D_POOL), 1) // POOL_GROUP_DIM
    wsum = jnp.where(grp == 0, s2[HIST:], jnp.where(grp == 1, s4[HIST:], jnp.where(grp == 2, s8[HIST:], s16[HIST:])))
    win = jnp.where(grp == 0, 2.0, jnp.where(grp == 1, 4.0, jnp.where(grp == 2, 8.0, 16.0)))
    pos = (grow + 1).astype(F32)
    dpool = wsum / jnp.minimum(pos, win) - e[HIST:]
    y_pool = _dot(dpool.astype(BF16), w_pool_ref[...]) * pool_scale_ref[...]
    mixed_s[:, D_RNN:D_RNN + D_POOL] = y_pool.astype(BF16)
    yield

    sc = _dot(q_mem.astype(BF16), kbd_ref[...]) * (XATTN_HEAD_DIM ** -0.5)
    ps = []
    for hh in range(XATTN_HEADS):
        sh = sc[:, hh * N_MEM:(hh + 1) * N_MEM]
        ph = jnp.exp(sh - jnp.max(sh, axis=-1, keepdims=True))
        ps.append((ph * (1.0 / jnp.sum(ph, axis=-1, keepdims=True))).astype(BF16))
        yield
    y_mem = _dot(jnp.concatenate(ps, axis=1), vbd_ref[...])
    mixed_s[:, D_RNN + D_POOL:] = y_mem.astype(BF16)
    yield

    uc = conv_b_ref[...] + conv_w_ref[CONV_WIDTH - 1:CONV_WIDTH, :] * urnn_buf[pl.ds(HIST, tm), :]
    for k in range(CONV_WIDTH - 1):
        off = HIST - (CONV_WIDTH - 1) + k
        uc = uc + conv_w_ref[k:k + 1, :] * urnn_buf[pl.ds(off, tm), :]
    yield

    ucb = uc.astype(BF16)
    pre = [_dot(ucb[:, c * cw:(c + 1) * cw], wg_ref[c]) for c in range(D_RNN // cw)]
    pre_a = jnp.concatenate([p[:, :cw] for p in pre], axis=1)
    pre_i = jnp.concatenate([p[:, cw:] for p in pre], axis=1)
    r = jax.nn.sigmoid(pre_a + b_a_ref[...])
    ig = jax.nn.sigmoid(pre_i + b_i_ref[...])
    yield
    lam = lam_ref[...]
    softplus_neg = jnp.maximum(-lam, 0.0) + jnp.log(1.0 + jnp.exp(-jnp.abs(lam)))
    log_a = (-LRU_C) * r * softplus_neg
    a = jnp.exp(log_a)
    z = jnp.maximum(1.0 - a * a, 0.0)
    mult = z * lax.rsqrt(jnp.maximum(z, SQRT_GUARD))
    mult = jnp.where(grow == 0, 1.0, mult)
    bx = mult * (ig * uc)
    yield

    ng = tm // SUBLANES
    a = a.reshape(ng, SUBLANES, D_RNN)
    bx = bx.reshape(ng, SUBLANES, D_RNN)
    row8 = lax.broadcasted_iota(jnp.int32, (1, SUBLANES, D_RNN), 1)
    d = 1
    while d < SUBLANES:
        a_sh = pltpu.roll(a, d, axis=1)
        b_sh = pltpu.roll(bx, d, axis=1)
        m = row8 >= d
        bx = jnp.where(m, a * b_sh + bx, bx)
        a = jnp.where(m, a * a_sh, a)
        d *= 2
        yield
    carry = jnp.broadcast_to(hcar[...], (SUBLANES, D_RNN))
    hs = []
    for g in range(ng):
        hv = a[g] * carry + bx[g]
        hs.append(hv)
        carry = jnp.broadcast_to(hv[SUBLANES - 1:SUBLANES, :], (SUBLANES, D_RNN))
    hcar[...] = carry[0:1, :]
    yield
    y_rnn = jnp.concatenate(hs, axis=0) * jax.nn.gelu(g_rnn)
    mixed_s[:, :D_RNN] = y_rnn.astype(BF16)
    urnn_buf[0:HIST, :] = urnn_buf[tm:tm + HIST, :]
    upool_buf[0:HIST, :] = upool_buf[tm:tm + HIST, :]
    yield

    mixed = mixed_s[...]
    x1s = []
    for c in range(D_MODEL // cw):
        x1s.append(x[:, c * cw:(c + 1) * cw] + _dot(mixed, w_out_ref[:, c * cw:(c + 1) * cw]))
        yield
    x1 = jnp.concatenate(x1s, axis=1)
    x1_ref[...] = x1

    hn = _rms(x1, g_ffn_ref[...]).astype(BF16)
    hn_ref[...] = _pack_bf16_pairs(hn)
    yield
    logits = lax.dot_general(w_rt_ref[...], hn, (((1,), (1,)), ((), ())),
                             preferred_element_type=F32) + b_r_ref[...]
    eid = lax.broadcasted_iota(jnp.int32, (N_EXPERTS, tm), 0)
    vals, idxs = [], []
    for _ in range(TOP_K):
        mx = jnp.max(logits, axis=0, keepdims=True)
        ix = jnp.min(jnp.where(logits == mx, eid, N_EXPERTS), axis=0, keepdims=True)
        vals.append(mx)
        idxs.append(ix)
        logits = jnp.where(eid == ix, -jnp.inf, logits)
    ex = [jnp.exp(v - vals[0]) for v in vals]
    den = ex[0] + ex[1] + ex[2] + ex[3]
    idx_ref[...] = jnp.concatenate(idxs, axis=0)
    gate_ref[...] = jnp.concatenate([e_ / den for e_ in ex], axis=0)


def _mix_call(x, g_mix, w_in, conv_w, conv_b, wg, b_a, b_i, lam, w_pool, pool_scale, kbd, vbd, w_out,
              g_ffn, w_rt, b_r):
    b, s, d = x.shape
    tm = MIX_ROWS
    ns = s // tm
    const2 = lambda shape: pl.BlockSpec(shape, lambda si: (0, 0))
    const3 = lambda shape: pl.BlockSpec(shape, lambda si: (0, 0, 0))
    return pl.pallas_call(
        _mix_kernel,
        grid=(ns,),
        in_specs=[
            pl.BlockSpec((b, tm, d), lambda si: (0, si, 0)),
            const2((1, d)),
            const2((d, D_IN)),
            const2((CONV_WIDTH, D_RNN)),
            const2((1, D_RNN)),
            const3((D_RNN // MXU_DIM, MXU_DIM, 2 * MXU_DIM)),
            const2((1, D_RNN)),
            const2((1, D_RNN)),
            const2((1, D_RNN)),
            const2((D_POOL, D_POOL)),
            const2((1, D_POOL)),
            const3((b, D_XATTN, XATTN_HEADS * N_MEM)),
            const3((b, XATTN_HEADS * N_MEM, D_XATTN)),
            const2((d, d)),
            const2((1, d)),
            const2((N_EXPERTS, d)),
            const2((N_EXPERTS, 1)),
        ],
        out_specs=[
            pl.BlockSpec((b, tm, d), lambda si: (0, si, 0)),
            pl.BlockSpec((b, tm, d // 2), lambda si: (0, si, 0)),
            pl.BlockSpec((b, TOP_K, tm), lambda si: (0, 0, si)),
            pl.BlockSpec((b, TOP_K, tm), lambda si: (0, 0, si)),
        ],
        out_shape=[
            jax.ShapeDtypeStruct((b, s, d), F32),
            jax.ShapeDtypeStruct((b, s, d // 2), jnp.uint32),
            jax.ShapeDtypeStruct((b, TOP_K, s), jnp.int32),
            jax.ShapeDtypeStruct((b, TOP_K, s), F32),
        ],
        scratch_shapes=[
            pltpu.VMEM((b, HIST + tm, D_RNN), F32),
            pltpu.VMEM((b, HIST + tm, D_POOL), F32),
            pltpu.VMEM((b, 1, D_RNN), F32),
            pltpu.VMEM((b, tm, d), BF16),
        ],
        compiler_params=pltpu.CompilerParams(
            dimension_semantics=("arbitrary",), vmem_limit_bytes=VMEM_LIMIT_BYTES),
        name="mixer_router",
    )(x, g_mix, w_in, conv_w, conv_b, wg, b_a, b_i, lam, w_pool, pool_scale, kbd, vbd, w_out, g_ffn, w_rt, b_r)


def _num_blocks(t):
    bm = EXPERT_ROWS
    return (t * TOP_K + N_EXPERTS * (bm - 1) + bm - 1) // bm


def _route_kernel(idx_ref, dest_ref, meta_ref, rank_s, carry_s, start_s, tri_s):
    p = pl.program_id(0)
    c = pl.program_id(1)
    nc = pl.num_programs(1)
    ch = ROUTE_CHUNK
    bm = float(EXPERT_ROWS)
    nbl = meta_ref.shape[1]
    eid = lax.broadcasted_iota(jnp.int32, (N_EXPERTS, ch), 0)
    idxc = idx_ref[...]

    @pl.when(jnp.logical_and(p == 0, c == 0))
    def _():
        carry_s[...] = jnp.zeros(carry_s.shape, F32)
        tri_s[...] = (lax.broadcasted_iota(jnp.int32, (ch, ch), 0)
                      < lax.broadcasted_iota(jnp.int32, (ch, ch), 1)).astype(BF16)

    @pl.when(p == 0)
    def _():
        sel = jnp.zeros((N_EXPERTS, ch), F32)
        for k in range(TOP_K):
            sel = sel + (idxc[k:k + 1, :] == eid).astype(F32)
        rank_s[c] = _dot(sel.astype(BF16), tri_s[...]) + carry_s[:, 0:1]
        carry_s[...] = carry_s[...] + jnp.sum(sel, axis=1, keepdims=True)

    @pl.when(jnp.logical_and(p == 0, c == nc - 1))
    def _():
        counts = carry_s[...]

        def div_bm(v):
            q = jnp.floor(v * (1.0 / bm))
            return q + jnp.where((q + 1.0) * bm <= v, 1.0, 0.0) - jnp.where(q * bm > v, 1.0, 0.0)

        padded = div_bm(counts + (bm - 1.0)) * bm
        e128 = lax.broadcasted_iota(jnp.int32, counts.shape, 0)
        pad_end = padded
        sh = 1
        while sh < N_EXPERTS:
            pad_end = pad_end + jnp.where(e128 >= sh, pltpu.roll(pad_end, sh, axis=0), 0.0)
            sh *= 2
        pad_start = pad_end - padded
        start_s[...] = pad_start
        total = pad_end[N_EXPERTS - 1:N_EXPERTS, 0:1]
        lane = lax.broadcasted_iota(jnp.int32, (1, nbl), 1)
        lane_f = lane.astype(F32)
        n_used = div_bm(total)
        bclamp = jnp.minimum(lane_f, n_used - 1.0)
        bstart = bclamp * bm
        pe, ps, cn, pd = pad_end[:, 0:1], pad_start[:, 0:1], counts[:, 0:1], padded[:, 0:1]
        be = jnp.minimum(jnp.sum((pe <= bstart).astype(F32), axis=0, keepdims=True), N_EXPERTS - 1.0)
        esub = lax.broadcasted_iota(jnp.int32, (N_EXPERTS, nbl), 0)
        onehot = esub.astype(F32) == be
        bv = jnp.sum(jnp.where(onehot, cn - (bstart - ps), 0.0), axis=0, keepdims=True)
        bv = jnp.where(lane_f * bm < total, jnp.clip(bv, 0.0, bm), 0.0)
        own = esub == lane
        fill_start = jnp.sum(jnp.where(own, ps + cn, 0.0), axis=0, keepdims=True)
        fill_n = jnp.sum(jnp.where(own, pd - cn, 0.0), axis=0, keepdims=True)
        later = jnp.logical_and(esub.astype(F32) > be, cn > 0.0)
        nxt = jnp.min(jnp.where(later, esub.astype(F32), float(N_EXPERTS)), axis=0, keepdims=True)
        nxt = jnp.where(nxt == float(N_EXPERTS), -1.0, nxt)
        zero = jnp.zeros((1, nbl), F32)
        meta_ref[...] = jnp.concatenate(
            [be, bv, bclamp, fill_start, fill_n, jnp.broadcast_to(n_used, (1, nbl)), nxt, zero],
            axis=0).astype(jnp.int32)

    @pl.when(p == 1)
    def _():
        base = start_s[:, 0:1] + rank_s[c]
        rows = [jnp.sum(jnp.where(idxc[k:k + 1, :] == eid, base, 0.0), axis=0, keepdims=True)
                for k in range(TOP_K)]
        dest_ref[...] = jnp.concatenate(rows, axis=0).astype(jnp.int32)


def _route_call(idx):
    k, t = idx.shape
    ch = ROUTE_CHUNK
    nc = t // ch
    nbl = -(-_num_blocks(t) // 128) * 128
    return pl.pallas_call(
        _route_kernel,
        grid=(2, nc),
        in_specs=[pl.BlockSpec((k, ch), lambda p, c: (0, c))],
        out_specs=[
            pl.BlockSpec((k, ch), lambda p, c: (0, c * p)),
            pl.BlockSpec((8, nbl), lambda p, c: (0, 0)),
        ],
        out_shape=[
            jax.ShapeDtypeStruct((k, t), jnp.int32),
            jax.ShapeDtypeStruct((8, nbl), jnp.int32),
        ],
        scratch_shapes=[
            pltpu.VMEM((nc, N_EXPERTS, ch), F32),
            pltpu.VMEM((N_EXPERTS, 128), F32),
            pltpu.VMEM((N_EXPERTS, 128), F32),
            pltpu.VMEM((ch, ch), BF16),
        ],
        compiler_params=pltpu.CompilerParams(dimension_semantics=("arbitrary", "arbitrary")),
        name="route_tables",
    )(idx)


def _sc_workers():
    info = plsc.get_sparse_core_info()
    return info.num_cores, info.num_cores * info.num_subcores


def _sc_dispatch_call(hn, dest, n_rows):
    t, w = hn.shape
    nc, nw = _sc_workers()
    per_w = t // nw
    ch = SC_ROWS
    mesh = plsc.VectorSubcoreMesh(core_axis_name="c", subcore_axis_name="s")

    @functools.partial(
        pl.kernel, mesh=mesh,
        out_type=jax.ShapeDtypeStruct((n_rows, w), hn.dtype),
        scratch_types=[pltpu.VMEM((ch,), jnp.int32), pltpu.VMEM((ch, w), hn.dtype)],
    )
    def body(hn_hbm, dest_hbm, xs_hbm, idx_v, rows_v):
        wid = lax.axis_index("s") * nc + lax.axis_index("c")

        @pl.loop(0, per_w // ch)
        def _(ci):
            base = pl.multiple_of(wid * per_w + ci * ch, ch)
            pltpu.sync_copy(hn_hbm.at[pl.ds(base, ch)], rows_v)
            for k in range(TOP_K):
                pltpu.sync_copy(dest_hbm.at[k, pl.ds(base, ch)], idx_v)
                pltpu.sync_copy(rows_v, xs_hbm.at[idx_v])

    return body(hn, dest)


def _sc_gather_call(ys, dest):
    k_, t = dest.shape
    w = ys.shape[1]
    nc, nw = _sc_workers()
    per_w = t // nw
    ch = SC_ROWS
    mesh = plsc.VectorSubcoreMesh(core_axis_name="c", subcore_axis_name="s")

    @functools.partial(
        pl.kernel, mesh=mesh,
        out_type=jax.ShapeDtypeStruct((k_, t, w), ys.dtype),
        scratch_types=[pltpu.VMEM((ch,), jnp.int32), pltpu.VMEM((ch, w), ys.dtype)],
    )
    def body(ys_hbm, dest_hbm, y4_hbm, idx_v, rows_v):
        wid = lax.axis_index("s") * nc + lax.axis_index("c")

        @pl.loop(0, per_w // ch)
        def _(ci):
            base = pl.multiple_of(wid * per_w + ci * ch, ch)
            for k in range(k_):
                pltpu.sync_copy(dest_hbm.at[k, pl.ds(base, ch)], idx_v)
                pltpu.sync_copy(ys_hbm.at[idx_v], rows_v)
                pltpu.sync_copy(rows_v, y4_hbm.at[k, pl.ds(base, ch)])

    return body(ys, dest)


def _pad_fill_kernel(fill_start_ref, fill_n_ref, xs_in_ref, xs_ref, zero_s, zsem):
    del xs_in_ref
    bm = EXPERT_ROWS
    zero_s[...] = jnp.zeros(zero_s.shape, zero_s.dtype)

    def go(cp, wait):
        if wait:
            cp.wait()
        else:
            cp.start()

    def fill(e, wait):
        n = fill_n_ref[e]
        st = fill_start_ref[e]
        head = n & (SUBLANES - 1)
        for j in range(SUBLANES - 1):
            cp = pltpu.make_async_copy(zero_s.at[pl.ds(0, 1), :], xs_ref.at[pl.ds(st + j, 1), :], zsem)
            pl.when(j < head)(functools.partial(go, cp, wait))
        st = pl.multiple_of(st + head, SUBLANES)
        bit = 1 << ((bm - 1).bit_length() - 1)
        while bit >= SUBLANES:
            cp = pltpu.make_async_copy(zero_s.at[pl.ds(0, bit), :], xs_ref.at[pl.ds(st, bit), :], zsem)
            pl.when((n & bit) != 0)(functools.partial(go, cp, wait))
            st = pl.multiple_of(st + (n & bit), SUBLANES)
            bit //= 2

    for wait in (False, True):
        lax.fori_loop(0, N_EXPERTS, lambda e, c, wait=wait: (fill(e, wait), c)[1], 0)


def _pad_fill_call(fill_start, fill_n, xs):
    bit_rows = 1 << ((EXPERT_ROWS - 1).bit_length() - 1)
    grid_spec = pltpu.PrefetchScalarGridSpec(
        num_scalar_prefetch=2,
        grid=(1,),
        in_specs=[pl.BlockSpec(memory_space=pl.ANY)],
        out_specs=pl.BlockSpec(memory_space=pl.ANY),
        scratch_shapes=[pltpu.VMEM((bit_rows, xs.shape[1]), xs.dtype), pltpu.SemaphoreType.DMA],
    )
    return pl.pallas_call(
        _pad_fill_kernel,
        grid_spec=grid_spec,
        out_shape=jax.ShapeDtypeStruct(xs.shape, xs.dtype),
        input_output_aliases={2: 0},
        compiler_params=pltpu.CompilerParams(dimension_semantics=("arbitrary",)),
        name="pad_fill",
    )(fill_start, fill_n, xs)


def _expert_kernel(be_ref, bv_ref, bi_ref, nx_ref, x_ref, wgu_hbm, bgu_ref, wd_hbm, bd_ref, y_ref,
                   wgu_st, wd_st, wgu_bf, wd_bf, wsem):
    del bi_ref
    i = pl.program_id(0)
    e = be_ref[i]
    e_prev = be_ref[jnp.maximum(i - 1, 0)]
    valid = bv_ref[i]

    def weight_copies(ex):
        return (pltpu.make_async_copy(wgu_hbm.at[ex], wgu_st, wsem.at[0]),
                pltpu.make_async_copy(wd_hbm.at[ex], wd_st, wsem.at[1]))

    @pl.when(i == 0)
    def _():
        for cp in weight_copies(e):
            cp.start()

    @pl.when(jnp.logical_and(valid > 0, jnp.logical_or(i == 0, e != e_prev)))
    def _():
        for cp in weight_copies(e):
            cp.wait()
        wgu_bf[...] = wgu_st[...].astype(BF16)
        wd_bf[...] = wd_st[...].astype(BF16)
        nxt = nx_ref[i]

        @pl.when(nxt >= 0)
        def _():
            for cp in weight_copies(nxt):
                cp.start()

    @pl.when(valid > 0)
    def _():
        x_lo, x_hi = _unpack_bf16_pairs(x_ref[...])
        xb = jnp.concatenate([x_lo.astype(BF16), x_hi.astype(BF16)], axis=1)
        hs = []
        for c in range(D_FF // MXU_DIM):
            lo = c * MXU_DIM
            hi = lo + MXU_DIM
            g = _dot(xb, wgu_bf[:, lo:hi]) + bgu_ref[0, :, lo:hi]
            up = _dot(xb, wgu_bf[:, D_FF + lo:D_FF + hi]) + bgu_ref[0, :, D_FF + lo:D_FF + hi]
            g = jnp.minimum(g, SWIGLU_LIMIT)
            up = jnp.clip(up, -SWIGLU_LIMIT, SWIGLU_LIMIT)
            glu = g * jax.nn.sigmoid(g * SWIGLU_ALPHA)
            hs.append(((up + 1.0) * glu).astype(BF16))
        y_ref[...] = _pack_bf16_pairs(_dot(jnp.concatenate(hs, axis=1), wd_bf[...]) + bd_ref[0])


def _expert_call(block_e, block_valid, block_idx, block_next, xs, w_gu, b_gu, w_down, b_down):
    n_pad, dw = xs.shape
    d = 2 * dw
    bm = EXPERT_ROWS
    nb = n_pad // bm
    grid_spec = pltpu.PrefetchScalarGridSpec(
        num_scalar_prefetch=4,
        grid=(nb,),
        in_specs=[
            pl.BlockSpec((bm, dw), lambda i, be, bv, bi, nx: (bi[i], 0)),
            pl.BlockSpec(memory_space=pl.ANY),
            pl.BlockSpec((1, 1, 2 * D_FF), lambda i, be, bv, bi, nx: (be[i], 0, 0)),
            pl.BlockSpec(memory_space=pl.ANY),
            pl.BlockSpec((1, 1, d), lambda i, be, bv, bi, nx: (be[i], 0, 0)),
        ],
        out_specs=pl.BlockSpec((bm, dw), lambda i, be, bv, bi, nx: (bi[i], 0)),
        scratch_shapes=[
            pltpu.VMEM((d, 2 * D_FF), F32),
            pltpu.VMEM((D_FF, d), F32),
            pltpu.VMEM((d, 2 * D_FF), BF16),
            pltpu.VMEM((D_FF, d), BF16),
            pltpu.SemaphoreType.DMA((2,)),
        ],
    )
    return pl.pallas_call(
        _expert_kernel,
        grid_spec=grid_spec,
        out_shape=jax.ShapeDtypeStruct((n_pad, dw), jnp.uint32),
        compiler_params=pltpu.CompilerParams(
            dimension_semantics=("arbitrary",), vmem_limit_bytes=VMEM_LIMIT_BYTES),
        name="expert_ffn",
    )(block_e, block_valid, block_idx, block_next, xs, w_gu, b_gu, w_down, b_down)


def _combine_kernel(x1_ref, gate_ref, y4_ref, g_ref, out_ref):
    x1 = x1_ref[...]
    n = x1.shape[1] // 2
    gate = gate_ref[...]
    acc_lo, acc_hi = x1[:, :n], x1[:, n:]
    for k in range(TOP_K):
        y_lo, y_hi = _unpack_bf16_pairs(y4_ref[k])
        acc_lo = acc_lo + gate[:, k:k + 1] * y_lo
        acc_hi = acc_hi + gate[:, k:k + 1] * y_hi
    out_ref[...] = _rms(jnp.concatenate([acc_lo, acc_hi], axis=1), g_ref[...])


def _combine_call(x1, gate_t, y4, g_final):
    t, d = x1.shape
    tt = COMBINE_ROWS
    return pl.pallas_call(
        _combine_kernel,
        grid=(t // tt,),
        in_specs=[
            pl.BlockSpec((tt, d), lambda i: (i, 0)),
            pl.BlockSpec((tt, TOP_K), lambda i: (i, 0)),
            pl.BlockSpec((TOP_K, tt, d // 2), lambda i: (0, i, 0)),
            pl.BlockSpec((1, d), lambda i: (0, 0)),
        ],
        out_specs=pl.BlockSpec((tt, d), lambda i: (i, 0)),
        out_shape=jax.ShapeDtypeStruct((t, d), F32),
        compiler_params=pltpu.CompilerParams(dimension_semantics=("arbitrary",)),
        name="combine_norm",
    )(x1, gate_t, y4, g_final)


def _block_diag(w):
    hh, n, _ = w.shape
    eye = jnp.eye(hh, dtype=w.dtype)
    return jnp.einsum("hij,hg->higj", w, eye).reshape(hh * n, hh * n)


def kernel(x, mem, norm_mix_g, w_in, conv_w, conv_b, w_rg_a, b_rg_a, w_rg_i, b_rg_i, lru_lambda, w_pool,
           pool_scale, mem_norm_g, w_mem_kv, w_out, norm_ffn_g, w_router, b_router, w_gu, b_gu, w_down,
           b_down, final_norm_g):
    b, s, d = x.shape
    t = b * s
    l = 0
    row = lambda v: v.reshape(1, -1)

    kv = _kv_call(mem, row(mem_norm_g[l]), w_mem_kv[l].astype(BF16))
    kh = kv[..., :D_XATTN].reshape(b, N_MEM, XATTN_HEADS, XATTN_HEAD_DIM)
    vh = kv[..., D_XATTN:].reshape(b, N_MEM, XATTN_HEADS, XATTN_HEAD_DIM)
    eye_h = jnp.eye(XATTN_HEADS, dtype=F32)
    kbd = jnp.einsum("bmhd,hg->bhdgm", kh, eye_h).reshape(b, D_XATTN, XATTN_HEADS * N_MEM).astype(BF16)
    vbd = jnp.einsum("bmhd,hg->bgmhd", vh, eye_h).reshape(b, XATTN_HEADS * N_MEM, D_XATTN).astype(BF16)

    heads_per = MXU_DIM // RNN_HEAD_DIM
    wg = jnp.stack([
        jnp.concatenate([_block_diag(w_rg_a[l, c * heads_per:(c + 1) * heads_per]),
                         _block_diag(w_rg_i[l, c * heads_per:(c + 1) * heads_per])], axis=1)
        for c in range(D_RNN // MXU_DIM)]).astype(BF16)

    x1, hn, idx, gate = _mix_call(
        x, row(norm_mix_g[l]), w_in[l].astype(BF16), conv_w[l], row(conv_b[l]), wg, row(b_rg_a[l]),
        row(b_rg_i[l]), row(lru_lambda[l]), _block_diag(w_pool[l]).astype(BF16), row(pool_scale[l]), kbd, vbd,
        w_out[l].astype(BF16), row(norm_ffn_g[l]), w_router[l].T.astype(BF16), b_router[l].reshape(-1, 1))

    hn = hn.reshape(t, d // 2)
    idx = jnp.transpose(idx, (1, 0, 2)).reshape(TOP_K, t)
    gate_t = jnp.transpose(gate, (0, 2, 1)).reshape(t, TOP_K)
    n_blocks = _num_blocks(t)
    dest, meta = _route_call(idx)
    be, bv, bi, nx = (meta[r, :n_blocks] for r in (0, 1, 2, 6))
    fill_start, fill_n = meta[3, :N_EXPERTS], meta[4, :N_EXPERTS]

    xs = _sc_dispatch_call(hn, dest, n_blocks * EXPERT_ROWS)
    xs = _pad_fill_call(fill_start, fill_n, xs)
    ys = _expert_call(be, bv, bi, nx, xs, w_gu[l], b_gu[l].reshape(N_EXPERTS, 1, -1), w_down[l],
                      b_down[l].reshape(N_EXPERTS, 1, -1))
    y4 = _sc_gather_call(ys, dest)
    out = _combine_call(x1.reshape(t, d), gate_t, y4, row(final_norm_g))
    return out.reshape(b, s, d)
```

```python
import functools

import jax
import jax.numpy as jnp
from jax import lax
from jax.experimental import pallas as pl
from jax.experimental.pallas import tpu as pltpu
from jax.experimental.pallas import tpu_sc as plsc

D_MODEL = 1024
N_MEM = 256
D_RNN = 512
RNN_HEADS = 8
RNN_HEAD_DIM = D_RNN // RNN_HEADS
CONV_WIDTH = 4
LRU_C = 8.0
D_POOL = 256
POOL_WINDOWS = (2, 4, 8, 16)
POOL_GROUP_DIM = D_POOL // len(POOL_WINDOWS)
MAX_WINDOW = max(POOL_WINDOWS)
D_XATTN = 256
XATTN_HEADS = 4
XATTN_HEAD_DIM = D_XATTN // XATTN_HEADS
D_IN = 2 * D_RNN + D_POOL + D_XATTN
N_EXPERTS = 32
TOP_K = 4
D_FF = D_MODEL
SWIGLU_ALPHA = 1.702
SWIGLU_LIMIT = 7.0
RMS_EPS = 1e-6
SQRT_GUARD = 1e-30

SUBLANES = 8
MXU_DIM = 256
VMEM_LIMIT_BYTES = 56 * 1024 * 1024

MIX_ROWS = 512
HIST = 16
assert HIST % SUBLANES == 0 and HIST >= max(MAX_WINDOW, CONV_WIDTH) - 1
EXPERT_ROWS = 528
COMBINE_ROWS = 512
ROUTE_CHUNK = 1024
SC_ROWS = 128
MIX_PHASE_STAGES = (1 + D_IN // MXU_DIM, XATTN_HEADS + 10, D_MODEL // MXU_DIM + 2)

BF16 = jnp.bfloat16
F32 = jnp.float32


def _rms(xf, g):
    return xf * lax.rsqrt(jnp.mean(xf * xf, axis=-1, keepdims=True) + RMS_EPS) * g


def _dot(a, b):
    return jnp.dot(a, b, preferred_element_type=F32)


def _pack_bf16_pairs(v):
    n = v.shape[1] // 2
    bits = lax.bitcast_convert_type(v.astype(BF16).astype(F32), jnp.uint32)
    return (bits[:, :n] >> 16) | (bits[:, n:] & jnp.uint32(0xFFFF0000))


def _unpack_bf16_pairs(w):
    lo = lax.bitcast_convert_type(w << 16, F32)
    hi = lax.bitcast_convert_type(w & jnp.uint32(0xFFFF0000), F32)
    return lo, hi


def _kv_kernel(mem_ref, g_ref, w_ref, kv_ref):
    mn = _rms(mem_ref[0], g_ref[...])
    kv_ref[0] = _dot(mn.astype(BF16), w_ref[...])


def _kv_call(mem, g, w_kv_bf):
    b = mem.shape[0]
    return pl.pallas_call(
        _kv_kernel,
        grid=(b,),
        in_specs=[
            pl.BlockSpec((1, N_MEM, D_MODEL), lambda i: (i, 0, 0)),
            pl.BlockSpec((1, D_MODEL), lambda i: (0, 0)),
            pl.BlockSpec((D_MODEL, 2 * D_XATTN), lambda i: (0, 0)),
        ],
        out_specs=pl.BlockSpec((1, N_MEM, 2 * D_XATTN), lambda i: (i, 0, 0)),
        out_shape=jax.ShapeDtypeStruct((b, N_MEM, 2 * D_XATTN), F32),
        name="kv_proj",
    )(mem, g, w_kv_bf)


def _mix_kernel(x_ref, g_mix_ref, w_in_ref, conv_w_ref, conv_b_ref, wg_ref, b_a_ref, b_i_ref, lam_ref,
                w_pool_ref, pool_scale_ref, kbd_ref, vbd_ref, w_out_ref, g_ffn_ref, w_rt_ref, b_r_ref,
                x1_ref, hn_ref, idx_ref, gate_ref,
                urnn_buf, upool_buf, hcar, mixed_s):
    tiles = [
        _mix_tile(x_ref.at[b], g_mix_ref, w_in_ref, conv_w_ref, conv_b_ref, wg_ref, b_a_ref, b_i_ref, lam_ref,
                  w_pool_ref, pool_scale_ref, kbd_ref.at[b], vbd_ref.at[b], w_out_ref, g_ffn_ref, w_rt_ref,
                  b_r_ref, x1_ref.at[b], hn_ref.at[b], idx_ref.at[b], gate_ref.at[b],
                  urnn_buf.at[b], upool_buf.at[b], hcar.at[b], mixed_s.at[b])
        for b in range(x_ref.shape[0])]
    n_phase = len(MIX_PHASE_STAGES)
    for step in range(len(tiles) + n_phase - 1):
        active = [(tiles[j], MIX_PHASE_STAGES[step - j]) for j in range(len(tiles)) if 0 <= step - j < n_phase]
        ticks = max(n for _, n in active)
        done = [0] * len(active)
        for i in range(ticks):
            for a, (tile, n) in enumerate(active):
                while done[a] * ticks < (i + 1) * n:
                    next(tile, None)
                    done[a] += 1
    done_mark = object()
    assert all(next(tile, done_mark) is done_mark for tile in tiles), "MIX_PHASE_STAGES does not cover _mix_tile"


def _mix_tile(x_ref, g_mix_ref, w_in_ref, conv_w_ref, conv_b_ref, wg_ref, b_a_ref, b_i_ref, lam_ref,
              w_pool_ref, pool_scale_ref, kbd_ref, vbd_ref, w_out_ref, g_ffn_ref, w_rt_ref, b_r_ref,
              x1_ref, hn_ref, idx_ref, gate_ref,
              urnn_buf, upool_buf, hcar, mixed_s):
    tm = MIX_ROWS
    cw = MXU_DIM
    s = pl.program_id(0)

    @pl.when(s == 0)
    def _():
        urnn_buf[0:HIST, :] = jnp.zeros((HIST, D_RNN), F32)
        upool_buf[0:HIST, :] = jnp.zeros((HIST, D_POOL), F32)
        hcar[...] = jnp.zeros((1, D_RNN), F32)

    x = x_ref[...]
    hb = _rms(x, g_mix_ref[...]).astype(BF16)
    yield
    proj = []
    for c in range(D_IN // cw):
        proj.append(_dot(hb, w_in_ref[:, c * cw:(c + 1) * cw]))
        yield
    urnn_buf[HIST:, :] = jnp.concatenate(proj[:D_RNN // cw], axis=1)
    g_rnn = jnp.concatenate(proj[D_RNN // cw:2 * D_RNN // cw], axis=1)
    upool_buf[HIST:, :] = proj[2 * D_RNN // cw]
    q_mem = proj[2 * D_RNN // cw + 1]

    row = lax.broadcasted_iota(jnp.int32, (tm, 1), 0)
    grow = row + s * tm

    e = upool_buf[...]
    s2 = e + pltpu.roll(e, 1, axis=0)
    s4 = s2 + pltpu.roll(s2, 2, axis=0)
    s8 = s4 + pltpu.roll(s4, 4, axis=0)
    s16 = s8 + pltpu.roll(s8, 8, axis=0)
    grp = lax.broadcasted_iota(jnp.int32, (tm, D_POOL), 1) // POOL_GROUP_DIM
    wsum = jnp.where(grp == 0, s2[HIST:], jnp.where(grp == 1, s4[HIST:], jnp.where(grp == 2, s8[HIST:], s16[HIST:])))
    win = jnp.where(grp == 0, 2.0, jnp.where(grp == 1, 4.0, jnp.where(grp == 2, 8.0, 16.0)))
    pos = (grow + 1).astype(F32)
    dpool = wsum / jnp.minimum(pos, win) - e[HIST:]
    y_pool = _dot(dpool.astype(BF16), w_pool_ref[...]) * pool_scale_ref[...]
    mixed_s[:, D_RNN:D_RNN + D_POOL] = y_pool.astype(BF16)
    yield

    sc = _dot(q_mem.astype(BF16), kbd_ref[...]) * (XATTN_HEAD_DIM ** -0.5)
    ps = []
    for hh in range(XATTN_HEADS):
        sh = sc[:, hh * N_MEM:(hh + 1) * N_MEM]
        ph = jnp.exp(sh - jnp.max(sh, axis=-1, keepdims=True))
        ps.append((ph * (1.0 / jnp.sum(ph, axis=-1, keepdims=True))).astype(BF16))
        yield
    y_mem = _dot(jnp.concatenate(ps, axis=1), vbd_ref[...])
    mixed_s[:, D_RNN + D_POOL:] = y_mem.astype(BF16)
    yield

    uc = conv_b_ref[...] + conv_w_ref[CONV_WIDTH - 1:CONV_WIDTH, :] * urnn_buf[pl.ds(HIST, tm), :]
    for k in range(CONV_WIDTH - 1):
        off = HIST - (CONV_WIDTH - 1) + k
        uc = uc + conv_w_ref[k:k + 1, :] * urnn_buf[pl.ds(off, tm), :]
    yield

    ucb = uc.astype(BF16)
    pre = [_dot(ucb[:, c * cw:(c + 1) * cw], wg_ref[c]) for c in range(D_RNN // cw)]
    pre_a = jnp.concatenate([p[:, :cw] for p in pre], axis=1)
    pre_i = jnp.concatenate([p[:, cw:] for p in pre], axis=1)
    r = jax.nn.sigmoid(pre_a + b_a_ref[...])
    ig = jax.nn.sigmoid(pre_i + b_i_ref[...])
    yield
    lam = lam_ref[...]
    softplus_neg = jnp.maximum(-lam, 0.0) + jnp.log(1.0 + jnp.exp(-jnp.abs(lam)))
    log_a = (-LRU_C) * r * softplus_neg
    a = jnp.exp(log_a)
    z = jnp.maximum(1.0 - a * a, 0.0)
    mult = z * lax.rsqrt(jnp.maximum(z, SQRT_GUARD))
    mult = jnp.where(grow == 0, 1.0, mult)
    bx = mult * (ig * uc)
    yield

    ng = tm // SUBLANES
    a = a.reshape(ng, SUBLANES, D_RNN)
    bx = bx.reshape(ng, SUBLANES, D_RNN)
    row8 = lax.broadcasted_iota(jnp.int32, (1, SUBLANES, D_RNN), 1)
    d = 1
    while d < SUBLANES:
        a_sh = pltpu.roll(a, d, axis=1)
        b_sh = pltpu.roll(bx, d, axis=1)
        m = row8 >= d
        bx = jnp.where(m, a * b_sh + bx, bx)
        a = jnp.where(m, a * a_sh, a)
        d *= 2
        yield
    carry = jnp.broadcast_to(hcar[...], (SUBLANES, D_RNN))
    hs = []
    for g in range(ng):
        hv = a[g] * carry + bx[g]
        hs.append(hv)
        carry = jnp.broadcast_to(hv[SUBLANES - 1:SUBLANES, :], (SUBLANES, D_RNN))
    hcar[...] = carry[0:1, :]
    yield
    y_rnn = jnp.concatenate(hs, axis=0) * jax.nn.gelu(g_rnn)
    mixed_s[:, :D_RNN] = y_rnn.astype(BF16)
    urnn_buf[0:HIST, :] = urnn_buf[tm:tm + HIST, :]
    upool_buf[0:HIST, :] = upool_buf[tm:tm + HIST, :]
    yield

    mixed = mixed_s[...]
    x1s = []
    for c in range(D_MODEL // cw):
        x1s.append(x[:, c * cw:(c + 1) * cw] + _dot(mixed, w_out_ref[:, c * cw:(c + 1) * cw]))
        yield
    x1 = jnp.concatenate(x1s, axis=1)
    x1_ref[...] = x1

    hn = _rms(x1, g_ffn_ref[...]).astype(BF16)
    hn_ref[...] = _pack_bf16_pairs(hn)
    yield
    logits = lax.dot_general(w_rt_ref[...], hn, (((1,), (1,)), ((), ())),
                             preferred_element_type=F32) + b_r_ref[...]
    eid = lax.broadcasted_iota(jnp.int32, (N_EXPERTS, tm), 0)
    vals, idxs = [], []
    for _ in range(TOP_K):
        mx = jnp.max(logits, axis=0, keepdims=True)
        ix = jnp.min(jnp.where(logits == mx, eid, N_EXPERTS), axis=0, keepdims=True)
        vals.append(mx)
        idxs.append(ix)
        logits = jnp.where(eid == ix, -jnp.inf, logits)
    ex = [jnp.exp(v - vals[0]) for v in vals]
    den = ex[0] + ex[1] + ex[2] + ex[3]
    idx_ref[...] = jnp.concatenate(idxs, axis=0)
    gate_ref[...] = jnp.concatenate([e_ / den for e_ in ex], axis=0)


def _mix_call(x, g_mix, w_in, conv_w, conv_b, wg, b_a, b_i, lam, w_pool, pool_scale, kbd, vbd, w_out,
              g_ffn, w_rt, b_r):
    b, s, d = x.shape
    tm = MIX_ROWS
    ns = s // tm
    const2 = lambda shape: pl.BlockSpec(shape, lambda si: (0, 0))
    const3 = lambda shape: pl.BlockSpec(shape, lambda si: (0, 0, 0))
    return pl.pallas_call(
        _mix_kernel,
        grid=(ns,),
        in_specs=[
            pl.BlockSpec((b, tm, d), lambda si: (0, si, 0)),
            const2((1, d)),
            const2((d, D_IN)),
            const2((CONV_WIDTH, D_RNN)),
            const2((1, D_RNN)),
            const3((D_RNN // MXU_DIM, MXU_DIM, 2 * MXU_DIM)),
            const2((1, D_RNN)),
            const2((1, D_RNN)),
            const2((1, D_RNN)),
            const2((D_POOL, D_POOL)),
            const2((1, D_POOL)),
            const3((b, D_XATTN, XATTN_HEADS * N_MEM)),
            const3((b, XATTN_HEADS * N_MEM, D_XATTN)),
            const2((d, d)),
            const2((1, d)),
            const2((N_EXPERTS, d)),
            const2((N_EXPERTS, 1)),
        ],
        out_specs=[
            pl.BlockSpec((b, tm, d), lambda si: (0, si, 0)),
            pl.BlockSpec((b, tm, d // 2), lambda si: (0, si, 0)),
            pl.BlockSpec((b, TOP_K, tm), lambda si: (0, 0, si)),
            pl.BlockSpec((b, TOP_K, tm), lambda si: (0, 0, si)),
        ],
        out_shape=[
            jax.ShapeDtypeStruct((b, s, d), F32),
            jax.ShapeDtypeStruct((b, s, d // 2), jnp.uint32),
            jax.ShapeDtypeStruct((b, TOP_K, s), jnp.int32),
            jax.ShapeDtypeStruct((b, TOP_K, s), F32),
        ],
        scratch_shapes=[
            pltpu.VMEM((b, HIST + tm, D_RNN), F32),
            pltpu.VMEM((b, HIST + tm, D_POOL), F32),
            pltpu.VMEM((b, 1, D_RNN), F32),
            pltpu.VMEM((b, tm, d), BF16),
        ],
        compiler_params=pltpu.CompilerParams(
            dimension_semantics=("arbitrary",), vmem_limit_bytes=VMEM_LIMIT_BYTES),
        name="mixer_router",
    )(x, g_mix, w_in, conv_w, conv_b, wg, b_a, b_i, lam, w_pool, pool_scale, kbd, vbd, w_out, g_ffn, w_rt, b_r)


def _num_blocks(t):
    bm = EXPERT_ROWS
    return (t * TOP_K + N_EXPERTS * (bm - 1) + bm - 1) // bm


def _route_kernel(idx_ref, dest_ref, meta_ref, rank_s, carry_s, start_s, tri_s):
    p = pl.program_id(0)
    c = pl.program_id(1)
    nc = pl.num_programs(1)
    ch = ROUTE_CHUNK
    bm = float(EXPERT_ROWS)
    nbl = meta_ref.shape[1]
    eid = lax.broadcasted_iota(jnp.int32, (N_EXPERTS, ch), 0)
    idxc = idx_ref[...]

    @pl.when(jnp.logical_and(p == 0, c == 0))
    def _():
        carry_s[...] = jnp.zeros(carry_s.shape, F32)
        tri_s[...] = (lax.broadcasted_iota(jnp.int32, (ch, ch), 0)
                      < lax.broadcasted_iota(jnp.int32, (ch, ch), 1)).astype(BF16)

    @pl.when(p == 0)
    def _():
        sel = jnp.zeros((N_EXPERTS, ch), F32)
        for k in range(TOP_K):
            sel = sel + (idxc[k:k + 1, :] == eid).astype(F32)
        rank_s[c] = _dot(sel.astype(BF16), tri_s[...]) + carry_s[:, 0:1]
        carry_s[...] = carry_s[...] + jnp.sum(sel, axis=1, keepdims=True)

    @pl.when(jnp.logical_and(p == 0, c == nc - 1))
    def _():
        counts = carry_s[...]

        def div_bm(v):
            q = jnp.floor(v * (1.0 / bm))
            return q + jnp.where((q + 1.0) * bm <= v, 1.0, 0.0) - jnp.where(q * bm > v, 1.0, 0.0)

        padded = div_bm(counts + (bm - 1.0)) * bm
        e128 = lax.broadcasted_iota(jnp.int32, counts.shape, 0)
        pad_end = padded
        sh = 1
        while sh < N_EXPERTS:
            pad_end = pad_end + jnp.where(e128 >= sh, pltpu.roll(pad_end, sh, axis=0), 0.0)
            sh *= 2
        pad_start = pad_end - padded
        start_s[...] = pad_start
        total = pad_end[N_EXPERTS - 1:N_EXPERTS, 0:1]
        lane = lax.broadcasted_iota(jnp.int32, (1, nbl), 1)
        lane_f = lane.astype(F32)
        n_used = div_bm(total)
        bclamp = jnp.minimum(lane_f, n_used - 1.0)
        bstart = bclamp * bm
        pe, ps, cn, pd = pad_end[:, 0:1], pad_start[:, 0:1], counts[:, 0:1], padded[:, 0:1]
        be = jnp.minimum(jnp.sum((pe <= bstart).astype(F32), axis=0, keepdims=True), N_EXPERTS - 1.0)
        esub = lax.broadcasted_iota(jnp.int32, (N_EXPERTS, nbl), 0)
        onehot = esub.astype(F32) == be
        bv = jnp.sum(jnp.where(onehot, cn - (bstart - ps), 0.0), axis=0, keepdims=True)
        bv = jnp.where(lane_f * bm < total, jnp.clip(bv, 0.0, bm), 0.0)
        own = esub == lane
        fill_start = jnp.sum(jnp.where(own, ps + cn, 0.0), axis=0, keepdims=True)
        fill_n = jnp.sum(jnp.where(own, pd - cn, 0.0), axis=0, keepdims=True)
        later = jnp.logical_and(esub.astype(F32) > be, cn > 0.0)
        nxt = jnp.min(jnp.where(later, esub.astype(F32), float(N_EXPERTS)), axis=0, keepdims=True)
        nxt = jnp.where(nxt == float(N_EXPERTS), -1.0, nxt)
        zero = jnp.zeros((1, nbl), F32)
        meta_ref[...] = jnp.concatenate(
            [be, bv, bclamp, fill_start, fill_n, jnp.broadcast_to(n_used, (1, nbl)), nxt, zero],
            axis=0).astype(jnp.int32)

    @pl.when(p == 1)
    def _():
        base = start_s[:, 0:1] + rank_s[c]
        rows = [jnp.sum(jnp.where(idxc[k:k + 1, :] == eid, base, 0.0), axis=0, keepdims=True)
                for k in range(TOP_K)]
        dest_ref[...] = jnp.concatenate(rows, axis=0).astype(jnp.int32)


def _route_call(idx):
    k, t = idx.shape
    ch = ROUTE_CHUNK
    nc = t // ch
    nbl = -(-_num_blocks(t) // 128) * 128
    return pl.pallas_call(
        _route_kernel,
        grid=(2, nc),
        in_specs=[pl.BlockSpec((k, ch), lambda p, c: (0, c))],
        out_specs=[
            pl.BlockSpec((k, ch), lambda p, c: (0, c * p)),
            pl.BlockSpec((8, nbl), lambda p, c: (0, 0)),
        ],
        out_shape=[
            jax.ShapeDtypeStruct((k, t), jnp.int32),
            jax.ShapeDtypeStruct((8, nbl), jnp.int32),
        ],
        scratch_shapes=[
            pltpu.VMEM((nc, N_EXPERTS, ch), F32),
            pltpu.VMEM((N_EXPERTS, 128), F32),
            pltpu.VMEM((N_EXPERTS, 128), F32),
            pltpu.VMEM((ch, ch), BF16),
        ],
        compiler_params=pltpu.CompilerParams(dimension_semantics=("arbitrary", "arbitrary")),
        name="route_tables",
    )(idx)


def _sc_workers():
    info = plsc.get_sparse_core_info()
    return info.num_cores, info.num_cores * info.num_subcores


def _sc_dispatch_call(hn, dest, n_rows):
    t, w = hn.shape
    nc, nw = _sc_workers()
    per_w = t // nw
    ch = SC_ROWS
    mesh = plsc.VectorSubcoreMesh(core_axis_name="c", subcore_axis_name="s")

    @functools.partial(
        pl.kernel, mesh=mesh,
        out_type=jax.ShapeDtypeStruct((n_rows, w), hn.dtype),
        scratch_types=[pltpu.VMEM((ch,), jnp.int32), pltpu.VMEM((ch, w), hn.dtype)],
    )
    def body(hn_hbm, dest_hbm, xs_hbm, idx_v, rows_v):
        wid = lax.axis_index("s") * nc + lax.axis_index("c")

        @pl.loop(0, per_w // ch)
        def _(ci):
            base = pl.multiple_of(wid * per_w + ci * ch, ch)
            pltpu.sync_copy(hn_hbm.at[pl.ds(base, ch)], rows_v)
            for k in range(TOP_K):
                pltpu.sync_copy(dest_hbm.at[k, pl.ds(base, ch)], idx_v)
                pltpu.sync_copy(rows_v, xs_hbm.at[idx_v])

    return body(hn, dest)


def _sc_gather_call(ys, dest):
    k_, t = dest.shape
    w = ys.shape[1]
    nc, nw = _sc_workers()
    per_w = t // nw
    ch = SC_ROWS
    mesh = plsc.VectorSubcoreMesh(core_axis_name="c", subcore_axis_name="s")

    @functools.partial(
        pl.kernel, mesh=mesh,
        out_type=jax.ShapeDtypeStruct((k_, t, w), ys.dtype),
        scratch_types=[pltpu.VMEM((ch,), jnp.int32), pltpu.VMEM((ch, w), ys.dtype)],
    )
    def body(ys_hbm, dest_hbm, y4_hbm, idx_v, rows_v):
        wid = lax.axis_index("s") * nc + lax.axis_index("c")

        @pl.loop(0, per_w // ch)
        def _(ci):
            base = pl.multiple_of(wid * per_w + ci * ch, ch)
            for k in range(k_):
                pltpu.sync_copy(dest_hbm.at[k, pl.ds(base, ch)], idx_v)
                pltpu.sync_copy(ys_hbm.at[idx_v], rows_v)
                pltpu.sync_copy(rows_v, y4_hbm.at[k, pl.ds(base, ch)])

    return body(ys, dest)


def _pad_fill_kernel(fill_start_ref, fill_n_ref, xs_in_ref, xs_ref, zero_s, zsem):
    del xs_in_ref
    bm = EXPERT_ROWS
    zero_s[...] = jnp.zeros(zero_s.shape, zero_s.dtype)

    def go(cp, wait):
        if wait:
            cp.wait()
        else:
            cp.start()

    def fill(e, wait):
        n = fill_n_ref[e]
        st = fill_start_ref[e]
        head = n & (SUBLANES - 1)
        for j in range(SUBLANES - 1):
            cp = pltpu.make_async_copy(zero_s.at[pl.ds(0, 1), :], xs_ref.at[pl.ds(st + j, 1), :], zsem)
            pl.when(j < head)(functools.partial(go, cp, wait))
        st = pl.multiple_of(st + head, SUBLANES)
        bit = 1 << ((bm - 1).bit_length() - 1)
        while bit >= SUBLANES:
            cp = pltpu.make_async_copy(zero_s.at[pl.ds(0, bit), :], xs_ref.at[pl.ds(st, bit), :], zsem)
            pl.when((n & bit) != 0)(functools.partial(go, cp, wait))
            st = pl.multiple_of(st + (n & bit), SUBLANES)
            bit //= 2

    for wait in (False, True):
        lax.fori_loop(0, N_EXPERTS, lambda e, c, wait=wait: (fill(e, wait), c)[1], 0)


def _pad_fill_call(fill_start, fill_n, xs):
    bit_rows = 1 << ((EXPERT_ROWS - 1).bit_length() - 1)
    grid_spec = pltpu.PrefetchScalarGridSpec(
        num_scalar_prefetch=2,
        grid=(1,),
        in_specs=[pl.BlockSpec(memory_space=pl.ANY)],
        out_specs=pl.BlockSpec(memory_space=pl.ANY),
        scratch_shapes=[pltpu.VMEM((bit_rows, xs.shape[1]), xs.dtype), pltpu.SemaphoreType.DMA],
    )
    return pl.pallas_call(
        _pad_fill_kernel,
        grid_spec=grid_spec,
        out_shape=jax.ShapeDtypeStruct(xs.shape, xs.dtype),
        input_output_aliases={2: 0},
        compiler_params=pltpu.CompilerParams(dimension_semantics=("arbitrary",)),
        name="pad_fill",
    )(fill_start, fill_n, xs)


def _expert_kernel(be_ref, bv_ref, bi_ref, nx_ref, x_ref, wgu_hbm, bgu_ref, wd_hbm, bd_ref, y_ref,
                   wgu_st, wd_st, wgu_bf, wd_bf, wsem):
    del bi_ref
    i = pl.program_id(0)
    e = be_ref[i]
    e_prev = be_ref[jnp.maximum(i - 1, 0)]
    valid = bv_ref[i]

    def weight_copies(ex):
        return (pltpu.make_async_copy(wgu_hbm.at[ex], wgu_st, wsem.at[0]),
                pltpu.make_async_copy(wd_hbm.at[ex], wd_st, wsem.at[1]))

    @pl.when(i == 0)
    def _():
        for cp in weight_copies(e):
            cp.start()

    @pl.when(jnp.logical_and(valid > 0, jnp.logical_or(i == 0, e != e_prev)))
    def _():
        for cp in weight_copies(e):
            cp.wait()
        wgu_bf[...] = wgu_st[...].astype(BF16)
        wd_bf[...] = wd_st[...].astype(BF16)
        nxt = nx_ref[i]

        @pl.when(nxt >= 0)
        def _():
            for cp in weight_copies(nxt):
                cp.start()

    @pl.when(valid > 0)
    def _():
        x_lo, x_hi = _unpack_bf16_pairs(x_ref[...])
        xb = jnp.concatenate([x_lo.astype(BF16), x_hi.astype(BF16)], axis=1)
        gu = _dot(xb, wgu_bf[...]) + bgu_ref[0]
        g = jnp.minimum(gu[:, :D_FF], SWIGLU_LIMIT)
        up = jnp.clip(gu[:, D_FF:], -SWIGLU_LIMIT, SWIGLU_LIMIT)
        glu = g * jax.nn.sigmoid(g * SWIGLU_ALPHA)
        h = ((up + 1.0) * glu).astype(BF16)
        y_ref[...] = _pack_bf16_pairs(_dot(h, wd_bf[...]) + bd_ref[0])


def _expert_call(block_e, block_valid, block_idx, block_next, xs, w_gu, b_gu, w_down, b_down):
    n_pad, dw = xs.shape
    d = 2 * dw
    bm = EXPERT_ROWS
    nb = n_pad // bm
    grid_spec = pltpu.PrefetchScalarGridSpec(
        num_scalar_prefetch=4,
        grid=(nb,),
        in_specs=[
            pl.BlockSpec((bm, dw), lambda i, be, bv, bi, nx: (bi[i], 0)),
            pl.BlockSpec(memory_space=pl.ANY),
            pl.BlockSpec((1, 1, 2 * D_FF), lambda i, be, bv, bi, nx: (be[i], 0, 0)),
            pl.BlockSpec(memory_space=pl.ANY),
            pl.BlockSpec((1, 1, d), lambda i, be, bv, bi, nx: (be[i], 0, 0)),
        ],
        out_specs=pl.BlockSpec((bm, dw), lambda i, be, bv, bi, nx: (bi[i], 0)),
        scratch_shapes=[
            pltpu.VMEM((d, 2 * D_FF), F32),
            pltpu.VMEM((D_FF, d), F32),
            pltpu.VMEM((d, 2 * D_FF), BF16),
            pltpu.VMEM((D_FF, d), BF16),
            pltpu.SemaphoreType.DMA((2,)),
        ],
    )
    return pl.pallas_call(
        _expert_kernel,
        grid_spec=grid_spec,
        out_shape=jax.ShapeDtypeStruct((n_pad, dw), jnp.uint32),
        compiler_params=pltpu.CompilerParams(
            dimension_semantics=("arbitrary",), vmem_limit_bytes=VMEM_LIMIT_BYTES),
        name="expert_ffn",
    )(block_e, block_valid, block_idx, block_next, xs, w_gu, b_gu, w_down, b_down)


def _combine_kernel(x1_ref, gate_ref, y4_ref, g_ref, out_ref):
    x1 = x1_ref[...]
    n = x1.shape[1] // 2
    gate = gate_ref[...]
    acc_lo, acc_hi = x1[:, :n], x1[:, n:]
    for k in range(TOP_K):
        y_lo, y_hi = _unpack_bf16_pairs(y4_ref[k])
        acc_lo = acc_lo + gate[:, k:k + 1] * y_lo
        acc_hi = acc_hi + gate[:, k:k + 1] * y_hi
    out_ref[...] = _rms(jnp.concatenate([acc_lo, acc_hi], axis=1), g_ref[...])


def _combine_call(x1, gate_t, y4, g_final):
    t, d = x1.shape
    tt = COMBINE_ROWS
    return pl.pallas_call(
        _combine_kernel,
        grid=(t // tt,),
        in_specs=[
            pl.BlockSpec((tt, d), lambda i: (i, 0)),
            pl.BlockSpec((tt, TOP_K), lambda i: (i, 0)),
            pl.BlockSpec((TOP_K, tt, d // 2), lambda i: (0, i, 0)),
            pl.BlockSpec((1, d), lambda i: (0, 0)),
        ],
        out_specs=pl.BlockSpec((tt, d), lambda i: (i, 0)),
        out_shape=jax.ShapeDtypeStruct((t, d), F32),
        compiler_params=pltpu.CompilerParams(dimension_semantics=("arbitrary",)),
        name="combine_norm",
    )(x1, gate_t, y4, g_final)


def _block_diag(w):
    hh, n, _ = w.shape
    eye = jnp.eye(hh, dtype=w.dtype)
    return jnp.einsum("hij,hg->higj", w, eye).reshape(hh * n, hh * n)


def kernel(x, mem, norm_mix_g, w_in, conv_w, conv_b, w_rg_a, b_rg_a, w_rg_i, b_rg_i, lru_lambda, w_pool,
           pool_scale, mem_norm_g, w_mem_kv, w_out, norm_ffn_g, w_router, b_router, w_gu, b_gu, w_down,
           b_down, final_norm_g):
    b, s, d = x.shape
    t = b * s
    l = 0
    row = lambda v: v.reshape(1, -1)

    kv = _kv_call(mem, row(mem_norm_g[l]), w_mem_kv[l].astype(BF16))
    kh = kv[..., :D_XATTN].reshape(b, N_MEM, XATTN_HEADS, XATTN_HEAD_DIM)
    vh = kv[..., D_XATTN:].reshape(b, N_MEM, XATTN_HEADS, XATTN_HEAD_DIM)
    eye_h = jnp.eye(XATTN_HEADS, dtype=F32)
    kbd = jnp.einsum("bmhd,hg->bhdgm", kh, eye_h).reshape(b, D_XATTN, XATTN_HEADS * N_MEM).astype(BF16)
    vbd = jnp.einsum("bmhd,hg->bgmhd", vh, eye_h).reshape(b, XATTN_HEADS * N_MEM, D_XATTN).astype(BF16)

    heads_per = MXU_DIM // RNN_HEAD_DIM
    wg = jnp.stack([
        jnp.concatenate([_block_diag(w_rg_a[l, c * heads_per:(c + 1) * heads_per]),
                         _block_diag(w_rg_i[l, c * heads_per:(c + 1) * heads_per])], axis=1)
        for c in range(D_RNN // MXU_DIM)]).astype(BF16)

    x1, hn, idx, gate = _mix_call(
        x, row(norm_mix_g[l]), w_in[l].astype(BF16), conv_w[l], row(conv_b[l]), wg, row(b_rg_a[l]),
        row(b_rg_i[l]), row(lru_lambda[l]), _block_diag(w_pool[l]).astype(BF16), row(pool_scale[l]), kbd, vbd,
        w_out[l].astype(BF16), row(norm_ffn_g[l]), w_router[l].T.astype(BF16), b_router[l].reshape(-1, 1))

    hn = hn.reshape(t, d // 2)
    idx = jnp.transpose(idx, (1, 0, 2)).reshape(TOP_K, t)
    gate_t = jnp.transpose(gate, (0, 2, 1)).reshape(t, TOP_K)
    n_blocks = _num_blocks(t)
    dest, meta = _route_call(idx)
    be, bv, bi, nx = (meta[r, :n_blocks] for r in (0, 1, 2, 6))
    fill_start, fill_n = meta[3, :N_EXPERTS], meta[4, :N_EXPERTS]

    xs = _sc_dispatch_call(hn, dest, n_blocks * EXPERT_ROWS)
    xs = _pad_fill_call(fill_start, fill_n, xs)
    ys = _expert_call(be, bv, bi, nx, xs, w_gu[l], b_gu[l].reshape(N_EXPERTS, 1, -1), w_down[l],
                      b_down[l].reshape(N_EXPERTS, 1, -1))
    y4 = _sc_gather_call(ys, dest)
    out = _combine_call(x1.reshape(t, d), gate_t, y4, row(final_norm_g))
    return out.reshape(b, s, d)
```

```python
import functools

import jax
import jax.numpy as jnp
from jax import lax
from jax.experimental import pallas as pl
from jax.experimental.pallas import tpu as pltpu
from jax.experimental.pallas import tpu_sc as plsc

D_MODEL = 1024
N_MEM = 256
D_RNN = 512
RNN_HEADS = 8
RNN_HEAD_DIM = D_RNN // RNN_HEADS
CONV_WIDTH = 4
LRU_C = 8.0
D_POOL = 256
POOL_WINDOWS = (2, 4, 8, 16)
POOL_GROUP_DIM = D_POOL // len(POOL_WINDOWS)
MAX_WINDOW = max(POOL_WINDOWS)
D_XATTN = 256
XATTN_HEADS = 4
XATTN_HEAD_DIM = D_XATTN // XATTN_HEADS
D_IN = 2 * D_RNN + D_POOL + D_XATTN
N_EXPERTS = 32
TOP_K = 4
D_FF = D_MODEL
SWIGLU_ALPHA = 1.702
SWIGLU_LIMIT = 7.0
RMS_EPS = 1e-6
SQRT_GUARD = 1e-30

SUBLANES = 8
MXU_DIM = 256
VMEM_LIMIT_BYTES = 56 * 1024 * 1024

MIX_ROWS = 512
HIST = 16
assert HIST % SUBLANES == 0 and HIST >= max(MAX_WINDOW, CONV_WIDTH) - 1
EXPERT_ROWS = 1072
COMBINE_ROWS = 512
ROUTE_CHUNK = 1024
SC_ROWS = 128
MIX_PHASE_STAGES = (1 + D_IN // MXU_DIM, XATTN_HEADS + 10, D_MODEL // MXU_DIM + 2)

BF16 = jnp.bfloat16
F32 = jnp.float32


def _rms(xf, g):
    return xf * lax.rsqrt(jnp.mean(xf * xf, axis=-1, keepdims=True) + RMS_EPS) * g


def _dot(a, b):
    return jnp.dot(a, b, preferred_element_type=F32)


def _pack_bf16_pairs(v):
    n = v.shape[1] // 2
    bits = lax.bitcast_convert_type(v.astype(BF16).astype(F32), jnp.uint32)
    return (bits[:, :n] >> 16) | (bits[:, n:] & jnp.uint32(0xFFFF0000))


def _unpack_bf16_pairs(w):
    lo = lax.bitcast_convert_type(w << 16, F32)
    hi = lax.bitcast_convert_type(w & jnp.uint32(0xFFFF0000), F32)
    return lo, hi


def _kv_kernel(mem_ref, g_ref, w_ref, kv_ref):
    mn = _rms(mem_ref[0], g_ref[...])
    kv_ref[0] = _dot(mn.astype(BF16), w_ref[...])


def _kv_call(mem, g, w_kv_bf):
    b = mem.shape[0]
    return pl.pallas_call(
        _kv_kernel,
        grid=(b,),
        in_specs=[
            pl.BlockSpec((1, N_MEM, D_MODEL), lambda i: (i, 0, 0)),
            pl.BlockSpec((1, D_MODEL), lambda i: (0, 0)),
            pl.BlockSpec((D_MODEL, 2 * D_XATTN), lambda i: (0, 0)),
        ],
        out_specs=pl.BlockSpec((1, N_MEM, 2 * D_XATTN), lambda i: (i, 0, 0)),
        out_shape=jax.ShapeDtypeStruct((b, N_MEM, 2 * D_XATTN), F32),
        name="kv_proj",
    )(mem, g, w_kv_bf)


def _mix_kernel(x_ref, g_mix_ref, w_in_ref, conv_w_ref, conv_b_ref, wg_ref, b_a_ref, b_i_ref, lam_ref,
                w_pool_ref, pool_scale_ref, kbd_ref, vbd_ref, w_out_ref, g_ffn_ref, w_rt_ref, b_r_ref,
                x1_ref, hn_ref, idx_ref, gate_ref,
                urnn_buf, upool_buf, hcar, mixed_s):
    tiles = [
        _mix_tile(x_ref.at[b], g_mix_ref, w_in_ref, conv_w_ref, conv_b_ref, wg_ref, b_a_ref, b_i_ref, lam_ref,
                  w_pool_ref, pool_scale_ref, kbd_ref.at[b], vbd_ref.at[b], w_out_ref, g_ffn_ref, w_rt_ref,
                  b_r_ref, x1_ref.at[b], hn_ref.at[b], idx_ref.at[b], gate_ref.at[b],
                  urnn_buf.at[b], upool_buf.at[b], hcar.at[b], mixed_s.at[b])
        for b in range(x_ref.shape[0])]
    n_phase = len(MIX_PHASE_STAGES)
    for step in range(len(tiles) + n_phase - 1):
        active = [(tiles[j], MIX_PHASE_STAGES[step - j]) for j in range(len(tiles)) if 0 <= step - j < n_phase]
        ticks = max(n for _, n in active)
        done = [0] * len(active)
        for i in range(ticks):
            for a, (tile, n) in enumerate(active):
                while done[a] * ticks < (i + 1) * n:
                    next(tile, None)
                    done[a] += 1
    done_mark = object()
    assert all(next(tile, done_mark) is done_mark for tile in tiles), "MIX_PHASE_STAGES does not cover _mix_tile"


def _mix_tile(x_ref, g_mix_ref, w_in_ref, conv_w_ref, conv_b_ref, wg_ref, b_a_ref, b_i_ref, lam_ref,
              w_pool_ref, pool_scale_ref, kbd_ref, vbd_ref, w_out_ref, g_ffn_ref, w_rt_ref, b_r_ref,
              x1_ref, hn_ref, idx_ref, gate_ref,
              urnn_buf, upool_buf, hcar, mixed_s):
    tm = MIX_ROWS
    cw = MXU_DIM
    s = pl.program_id(0)

    @pl.when(s == 0)
    def _():
        urnn_buf[0:HIST, :] = jnp.zeros((HIST, D_RNN), F32)
        upool_buf[0:HIST, :] = jnp.zeros((HIST, D_POOL), F32)
        hcar[...] = jnp.zeros((1, D_RNN), F32)

    x = x_ref[...]
    hb = _rms(x, g_mix_ref[...]).astype(BF16)
    yield
    proj = []
    for c in range(D_IN // cw):
        proj.append(_dot(hb, w_in_ref[:, c * cw:(c + 1) * cw]))
        yield
    urnn_buf[HIST:, :] = jnp.concatenate(proj[:D_RNN // cw], axis=1)
    g_rnn = jnp.concatenate(proj[D_RNN // cw:2 * D_RNN // cw], axis=1)
    upool_buf[HIST:, :] = proj[2 * D_RNN // cw]
    q_mem = proj[2 * D_RNN // cw + 1]

    row = lax.broadcasted_iota(jnp.int32, (tm, 1), 0)
    grow = row + s * tm

    e = upool_buf[...]
    s2 = e + pltpu.roll(e, 1, axis=0)
    s4 = s2 + pltpu.roll(s2, 2, axis=0)
    s8 = s4 + pltpu.roll(s4, 4, axis=0)
    s16 = s8 + pltpu.roll(s8, 8, axis=0)
    grp = lax.broadcasted_iota(jnp.int32, (tm, D_POOL), 1) // POOL_GROUP_DIM
    wsum = jnp.where(grp == 0, s2[HIST:], jnp.where(grp == 1, s4[HIST:], jnp.where(grp == 2, s8[HIST:], s16[HIST:])))
    win = jnp.where(grp == 0, 2.0, jnp.where(grp == 1, 4.0, jnp.where(grp == 2, 8.0, 16.0)))
    pos = (grow + 1).astype(F32)
    dpool = wsum / jnp.minimum(pos, win) - e[HIST:]
    y_pool = _dot(dpool.astype(BF16), w_pool_ref[...]) * pool_scale_ref[...]
    mixed_s[:, D_RNN:D_RNN + D_POOL] = y_pool.astype(BF16)
    yield

    sc = _dot(q_mem.astype(BF16), kbd_ref[...]) * (XATTN_HEAD_DIM ** -0.5)
    ps = []
    for hh in range(XATTN_HEADS):
        sh = sc[:, hh * N_MEM:(hh + 1) * N_MEM]
        ph = jnp.exp(sh - jnp.max(sh, axis=-1, keepdims=True))
        ps.append((ph * (1.0 / jnp.sum(ph, axis=-1, keepdims=True))).astype(BF16))
        yield
    y_mem = _dot(jnp.concatenate(ps, axis=1), vbd_ref[...])
    mixed_s[:, D_RNN + D_POOL:] = y_mem.astype(BF16)
    yield

    uc = conv_b_ref[...] + conv_w_ref[CONV_WIDTH - 1:CONV_WIDTH, :] * urnn_buf[pl.ds(HIST, tm), :]
    for k in range(CONV_WIDTH - 1):
        off = HIST - (CONV_WIDTH - 1) + k
        uc = uc + conv_w_ref[k:k + 1, :] * urnn_buf[pl.ds(off, tm), :]
    yield

    ucb = uc.astype(BF16)
    pre = [_dot(ucb[:, c * cw:(c + 1) * cw], wg_ref[c]) for c in range(D_RNN // cw)]
    pre_a = jnp.concatenate([p[:, :cw] for p in pre], axis=1)
    pre_i = jnp.concatenate([p[:, cw:] for p in pre], axis=1)
    r = jax.nn.sigmoid(pre_a + b_a_ref[...])
    ig = jax.nn.sigmoid(pre_i + b_i_ref[...])
    yield
    lam = lam_ref[...]
    softplus_neg = jnp.maximum(-lam, 0.0) + jnp.log(1.0 + jnp.exp(-jnp.abs(lam)))
    log_a = (-LRU_C) * r * softplus_neg
    a = jnp.exp(log_a)
    z = jnp.maximum(1.0 - a * a, 0.0)
    mult = z * lax.rsqrt(jnp.maximum(z, SQRT_GUARD))
    mult = jnp.where(grow == 0, 1.0, mult)
    bx = mult * (ig * uc)
    yield

    ng = tm // SUBLANES
    a = a.reshape(ng, SUBLANES, D_RNN)
    bx = bx.reshape(ng, SUBLANES, D_RNN)
    row8 = lax.broadcasted_iota(jnp.int32, (1, SUBLANES, D_RNN), 1)
    d = 1
    while d < SUBLANES:
        a_sh = pltpu.roll(a, d, axis=1)
        b_sh = pltpu.roll(bx, d, axis=1)
        m = row8 >= d
        bx = jnp.where(m, a * b_sh + bx, bx)
        a = jnp.where(m, a * a_sh, a)
        d *= 2
        yield
    carry = jnp.broadcast_to(hcar[...], (SUBLANES, D_RNN))
    hs = []
    for g in range(ng):
        hv = a[g] * carry + bx[g]
        hs.append(hv)
        carry = jnp.broadcast_to(hv[SUBLANES - 1:SUBLANES, :], (SUBLANES, D_RNN))
    hcar[...] = carry[0:1, :]
    yield
    y_rnn = jnp.concatenate(hs, axis=0) * jax.nn.gelu(g_rnn)
    mixed_s[:, :D_RNN] = y_rnn.astype(BF16)
    urnn_buf[0:HIST, :] = urnn_buf[tm:tm + HIST, :]
    upool_buf[0:HIST, :] = upool_buf[tm:tm + HIST, :]
    yield

    mixed = mixed_s[...]
    x1s = []
    for c in range(D_MODEL // cw):
        x1s.append(x[:, c * cw:(c + 1) * cw] + _dot(mixed, w_out_ref[:, c * cw:(c + 1) * cw]))
        yield
    x1 = jnp.concatenate(x1s, axis=1)
    x1_ref[...] = x1

    hn = _rms(x1, g_ffn_ref[...]).astype(BF16)
    hn_ref[...] = _pack_bf16_pairs(hn)
    yield
    logits = lax.dot_general(w_rt_ref[...], hn, (((1,), (1,)), ((), ())),
                             preferred_element_type=F32) + b_r_ref[...]
    eid = lax.broadcasted_iota(jnp.int32, (N_EXPERTS, tm), 0)
    vals, idxs = [], []
    for _ in range(TOP_K):
        mx = jnp.max(logits, axis=0, keepdims=True)
        ix = jnp.min(jnp.where(logits == mx, eid, N_EXPERTS), axis=0, keepdims=True)
        vals.append(mx)
        idxs.append(ix)
        logits = jnp.where(eid == ix, -jnp.inf, logits)
    ex = [jnp.exp(v - vals[0]) for v in vals]
    den = ex[0] + ex[1] + ex[2] + ex[3]
    idx_ref[...] = jnp.concatenate(idxs, axis=0)
    gate_ref[...] = jnp.concatenate([e_ / den for e_ in ex], axis=0)


def _mix_call(x, g_mix, w_in, conv_w, conv_b, wg, b_a, b_i, lam, w_pool, pool_scale, kbd, vbd, w_out,
              g_ffn, w_rt, b_r):
    b, s, d = x.shape
    tm = MIX_ROWS
    ns = s // tm
    const2 = lambda shape: pl.BlockSpec(shape, lambda si: (0, 0))
    const3 = lambda shape: pl.BlockSpec(shape, lambda si: (0, 0, 0))
    return pl.pallas_call(
        _mix_kernel,
        grid=(ns,),
        in_specs=[
            pl.BlockSpec((b, tm, d), lambda si: (0, si, 0)),
            const2((1, d)),
            const2((d, D_IN)),
            const2((CONV_WIDTH, D_RNN)),
            const2((1, D_RNN)),
            const3((D_RNN // MXU_DIM, MXU_DIM, 2 * MXU_DIM)),
            const2((1, D_RNN)),
            const2((1, D_RNN)),
            const2((1, D_RNN)),
            const2((D_POOL, D_POOL)),
            const2((1, D_POOL)),
            const3((b, D_XATTN, XATTN_HEADS * N_MEM)),
            const3((b, XATTN_HEADS * N_MEM, D_XATTN)),
            const2((d, d)),
            const2((1, d)),
            const2((N_EXPERTS, d)),
            const2((N_EXPERTS, 1)),
        ],
        out_specs=[
            pl.BlockSpec((b, tm, d), lambda si: (0, si, 0)),
            pl.BlockSpec((b, tm, d // 2), lambda si: (0, si, 0)),
            pl.BlockSpec((b, TOP_K, tm), lambda si: (0, 0, si)),
            pl.BlockSpec((b, TOP_K, tm), lambda si: (0, 0, si)),
        ],
        out_shape=[
            jax.ShapeDtypeStruct((b, s, d), F32),
            jax.ShapeDtypeStruct((b, s, d // 2), jnp.uint32),
            jax.ShapeDtypeStruct((b, TOP_K, s), jnp.int32),
            jax.ShapeDtypeStruct((b, TOP_K, s), F32),
        ],
        scratch_shapes=[
            pltpu.VMEM((b, HIST + tm, D_RNN), F32),
            pltpu.VMEM((b, HIST + tm, D_POOL), F32),
            pltpu.VMEM((b, 1, D_RNN), F32),
            pltpu.VMEM((b, tm, d), BF16),
        ],
        compiler_params=pltpu.CompilerParams(
            dimension_semantics=("arbitrary",), vmem_limit_bytes=VMEM_LIMIT_BYTES),
        name="mixer_router",
    )(x, g_mix, w_in, conv_w, conv_b, wg, b_a, b_i, lam, w_pool, pool_scale, kbd, vbd, w_out, g_ffn, w_rt, b_r)


def _num_blocks(t):
    bm = EXPERT_ROWS
    return (t * TOP_K + N_EXPERTS * (bm - 1) + bm - 1) // bm


def _route_kernel(idx_ref, dest_ref, meta_ref, rank_s, carry_s, start_s, tri_s):
    p = pl.program_id(0)
    c = pl.program_id(1)
    nc = pl.num_programs(1)
    ch = ROUTE_CHUNK
    bm = float(EXPERT_ROWS)
    nbl = meta_ref.shape[1]
    eid = lax.broadcasted_iota(jnp.int32, (N_EXPERTS, ch), 0)
    idxc = idx_ref[...]

    @pl.when(jnp.logical_and(p == 0, c == 0))
    def _():
        carry_s[...] = jnp.zeros(carry_s.shape, F32)
        tri_s[...] = (lax.broadcasted_iota(jnp.int32, (ch, ch), 0)
                      < lax.broadcasted_iota(jnp.int32, (ch, ch), 1)).astype(BF16)

    @pl.when(p == 0)
    def _():
        sel = jnp.zeros((N_EXPERTS, ch), F32)
        for k in range(TOP_K):
            sel = sel + (idxc[k:k + 1, :] == eid).astype(F32)
        rank_s[c] = _dot(sel.astype(BF16), tri_s[...]) + carry_s[:, 0:1]
        carry_s[...] = carry_s[...] + jnp.sum(sel, axis=1, keepdims=True)

    @pl.when(jnp.logical_and(p == 0, c == nc - 1))
    def _():
        counts = carry_s[...]

        def div_bm(v):
            q = jnp.floor(v * (1.0 / bm))
            return q + jnp.where((q + 1.0) * bm <= v, 1.0, 0.0) - jnp.where(q * bm > v, 1.0, 0.0)

        padded = div_bm(counts + (bm - 1.0)) * bm
        e128 = lax.broadcasted_iota(jnp.int32, counts.shape, 0)
        pad_end = padded
        sh = 1
        while sh < N_EXPERTS:
            pad_end = pad_end + jnp.where(e128 >= sh, pltpu.roll(pad_end, sh, axis=0), 0.0)
            sh *= 2
        pad_start = pad_end - padded
        start_s[...] = pad_start
        total = pad_end[N_EXPERTS - 1:N_EXPERTS, 0:1]
        lane = lax.broadcasted_iota(jnp.int32, (1, nbl), 1)
        lane_f = lane.astype(F32)
        n_used = div_bm(total)
        pe, ps, cn, pd = pad_end[:, 0:1], pad_start[:, 0:1], counts[:, 0:1], padded[:, 0:1]
        be = jnp.minimum(jnp.sum((pe <= lane_f * bm).astype(F32), axis=0, keepdims=True), N_EXPERTS - 1.0)
        esub = lax.broadcasted_iota(jnp.int32, (N_EXPERTS, nbl), 0)
        own = esub == lane
        fill_start = jnp.sum(jnp.where(own, ps + cn, 0.0), axis=0, keepdims=True)
        fill_n = jnp.sum(jnp.where(own, pd - cn, 0.0), axis=0, keepdims=True)
        later = jnp.logical_and(esub.astype(F32) > be, cn > 0.0)
        nxt = jnp.min(jnp.where(later, esub.astype(F32), float(N_EXPERTS)), axis=0, keepdims=True)
        nxt = jnp.where(nxt == float(N_EXPERTS), -1.0, nxt)
        zero = jnp.zeros((1, nbl), F32)
        meta_ref[...] = jnp.concatenate(
            [be, zero, zero, fill_start, fill_n, jnp.broadcast_to(n_used, (1, nbl)), nxt, zero],
            axis=0).astype(jnp.int32)

    @pl.when(p == 1)
    def _():
        base = start_s[:, 0:1] + rank_s[c]
        rows = [jnp.sum(jnp.where(idxc[k:k + 1, :] == eid, base, 0.0), axis=0, keepdims=True)
                for k in range(TOP_K)]
        dest_ref[...] = jnp.concatenate(rows, axis=0).astype(jnp.int32)


def _route_call(idx):
    k, t = idx.shape
    ch = ROUTE_CHUNK
    nc = t // ch
    nbl = -(-_num_blocks(t) // 128) * 128
    return pl.pallas_call(
        _route_kernel,
        grid=(2, nc),
        in_specs=[pl.BlockSpec((k, ch), lambda p, c: (0, c))],
        out_specs=[
            pl.BlockSpec((k, ch), lambda p, c: (0, c * p)),
            pl.BlockSpec((8, nbl), lambda p, c: (0, 0)),
        ],
        out_shape=[
            jax.ShapeDtypeStruct((k, t), jnp.int32),
            jax.ShapeDtypeStruct((8, nbl), jnp.int32),
        ],
        scratch_shapes=[
            pltpu.VMEM((nc, N_EXPERTS, ch), F32),
            pltpu.VMEM((N_EXPERTS, 128), F32),
            pltpu.VMEM((N_EXPERTS, 128), F32),
            pltpu.VMEM((ch, ch), BF16),
        ],
        compiler_params=pltpu.CompilerParams(dimension_semantics=("arbitrary", "arbitrary")),
        name="route_tables",
    )(idx)


def _sc_workers():
    info = plsc.get_sparse_core_info()
    return info.num_cores, info.num_cores * info.num_subcores


def _sc_dispatch_call(hn, dest, n_rows):
    t, w = hn.shape
    nc, nw = _sc_workers()
    per_w = t // nw
    ch = SC_ROWS
    mesh = plsc.VectorSubcoreMesh(core_axis_name="c", subcore_axis_name="s")

    @functools.partial(
        pl.kernel, mesh=mesh,
        out_type=jax.ShapeDtypeStruct((n_rows, w), hn.dtype),
        scratch_types=[pltpu.VMEM((ch,), jnp.int32), pltpu.VMEM((ch, w), hn.dtype)],
    )
    def body(hn_hbm, dest_hbm, xs_hbm, idx_v, rows_v):
        wid = lax.axis_index("s") * nc + lax.axis_index("c")

        @pl.loop(0, per_w // ch)
        def _(ci):
            base = pl.multiple_of(wid * per_w + ci * ch, ch)
            pltpu.sync_copy(hn_hbm.at[pl.ds(base, ch)], rows_v)
            for k in range(TOP_K):
                pltpu.sync_copy(dest_hbm.at[k, pl.ds(base, ch)], idx_v)
                pltpu.sync_copy(rows_v, xs_hbm.at[idx_v])

    return body(hn, dest)


def _sc_gather_call(ys, dest):
    k_, t = dest.shape
    w = ys.shape[1]
    nc, nw = _sc_workers()
    per_w = t // nw
    ch = SC_ROWS
    mesh = plsc.VectorSubcoreMesh(core_axis_name="c", subcore_axis_name="s")

    @functools.partial(
        pl.kernel, mesh=mesh,
        out_type=jax.ShapeDtypeStruct((k_, t, w), ys.dtype),
        scratch_types=[pltpu.VMEM((ch,), jnp.int32), pltpu.VMEM((ch, w), ys.dtype)],
    )
    def body(ys_hbm, dest_hbm, y4_hbm, idx_v, rows_v):
        wid = lax.axis_index("s") * nc + lax.axis_index("c")

        @pl.loop(0, per_w // ch)
        def _(ci):
            base = pl.multiple_of(wid * per_w + ci * ch, ch)
            for k in range(k_):
                pltpu.sync_copy(dest_hbm.at[k, pl.ds(base, ch)], idx_v)
                pltpu.sync_copy(ys_hbm.at[idx_v], rows_v)
                pltpu.sync_copy(rows_v, y4_hbm.at[k, pl.ds(base, ch)])

    return body(ys, dest)


def _pad_fill_kernel(fill_start_ref, fill_n_ref, xs_in_ref, xs_ref, zero_s, zsem):
    del xs_in_ref
    bm = EXPERT_ROWS
    zero_s[...] = jnp.zeros(zero_s.shape, zero_s.dtype)

    def go(cp, wait):
        if wait:
            cp.wait()
        else:
            cp.start()

    def fill(e, wait):
        n = fill_n_ref[e]
        st = fill_start_ref[e]
        head = n & (SUBLANES - 1)
        for j in range(SUBLANES - 1):
            cp = pltpu.make_async_copy(zero_s.at[pl.ds(0, 1), :], xs_ref.at[pl.ds(st + j, 1), :], zsem)
            pl.when(j < head)(functools.partial(go, cp, wait))
        st = pl.multiple_of(st + head, SUBLANES)
        bit = 1 << ((bm - 1).bit_length() - 1)
        while bit >= SUBLANES:
            cp = pltpu.make_async_copy(zero_s.at[pl.ds(0, bit), :], xs_ref.at[pl.ds(st, bit), :], zsem)
            pl.when((n & bit) != 0)(functools.partial(go, cp, wait))
            st = pl.multiple_of(st + (n & bit), SUBLANES)
            bit //= 2

    for wait in (False, True):
        lax.fori_loop(0, N_EXPERTS, lambda e, c, wait=wait: (fill(e, wait), c)[1], 0)


def _pad_fill_call(fill_start, fill_n, xs):
    bit_rows = 1 << ((EXPERT_ROWS - 1).bit_length() - 1)
    grid_spec = pltpu.PrefetchScalarGridSpec(
        num_scalar_prefetch=2,
        grid=(1,),
        in_specs=[pl.BlockSpec(memory_space=pl.ANY)],
        out_specs=pl.BlockSpec(memory_space=pl.ANY),
        scratch_shapes=[pltpu.VMEM((bit_rows, xs.shape[1]), xs.dtype), pltpu.SemaphoreType.DMA],
    )
    return pl.pallas_call(
        _pad_fill_kernel,
        grid_spec=grid_spec,
        out_shape=jax.ShapeDtypeStruct(xs.shape, xs.dtype),
        input_output_aliases={2: 0},
        compiler_params=pltpu.CompilerParams(dimension_semantics=("arbitrary",)),
        name="pad_fill",
    )(fill_start, fill_n, xs)


def _expert_kernel(be_ref, nx_ref, x_ref, wgu_hbm, bgu_ref, wd_hbm, bd_ref, y_ref,
                   wgu_st, wd_st, wgu_bf, wd_bf, wsem):
    i = pl.program_id(0)
    e = be_ref[i]
    e_prev = be_ref[jnp.maximum(i - 1, 0)]

    def weight_copies(ex):
        return (pltpu.make_async_copy(wgu_hbm.at[ex], wgu_st, wsem.at[0]),
                pltpu.make_async_copy(wd_hbm.at[ex], wd_st, wsem.at[1]))

    @pl.when(i == 0)
    def _():
        for cp in weight_copies(e):
            cp.start()

    @pl.when(jnp.logical_or(i == 0, e != e_prev))
    def _():
        for cp in weight_copies(e):
            cp.wait()
        wgu_bf[...] = wgu_st[...].astype(BF16)
        wd_bf[...] = wd_st[...].astype(BF16)
        nxt = nx_ref[i]

        @pl.when(nxt >= 0)
        def _():
            for cp in weight_copies(nxt):
                cp.start()

    x_lo, x_hi = _unpack_bf16_pairs(x_ref[...])
    xb = jnp.concatenate([x_lo.astype(BF16), x_hi.astype(BF16)], axis=1)
    hs = []
    for c in range(D_FF // (2 * MXU_DIM)):
        lo = c * 2 * MXU_DIM
        hi = lo + 2 * MXU_DIM
        g = _dot(xb, wgu_bf[:, lo:hi]) + bgu_ref[0, :, lo:hi]
        up = _dot(xb, wgu_bf[:, D_FF + lo:D_FF + hi]) + bgu_ref[0, :, D_FF + lo:D_FF + hi]
        g = jnp.minimum(g, SWIGLU_LIMIT)
        up = jnp.clip(up, -SWIGLU_LIMIT, SWIGLU_LIMIT)
        glu = g * jax.nn.sigmoid(g * SWIGLU_ALPHA)
        hs.append(((up + 1.0) * glu).astype(BF16))
    y_ref[...] = _pack_bf16_pairs(_dot(jnp.concatenate(hs, axis=1), wd_bf[...]) + bd_ref[0])


def _expert_call(n_used, block_e, block_next, xs, w_gu, b_gu, w_down, b_down):
    n_pad, dw = xs.shape
    d = 2 * dw
    bm = EXPERT_ROWS
    grid_spec = pltpu.PrefetchScalarGridSpec(
        num_scalar_prefetch=2,
        grid=(n_used,),
        in_specs=[
            pl.BlockSpec((bm, dw), lambda i, be, nx: (i, 0)),
            pl.BlockSpec(memory_space=pl.ANY),
            pl.BlockSpec((1, 1, 2 * D_FF), lambda i, be, nx: (be[i], 0, 0)),
            pl.BlockSpec(memory_space=pl.ANY),
            pl.BlockSpec((1, 1, d), lambda i, be, nx: (be[i], 0, 0)),
        ],
        out_specs=pl.BlockSpec((bm, dw), lambda i, be, nx: (i, 0)),
        scratch_shapes=[
            pltpu.VMEM((d, 2 * D_FF), F32),
            pltpu.VMEM((D_FF, d), F32),
            pltpu.VMEM((d, 2 * D_FF), BF16),
            pltpu.VMEM((D_FF, d), BF16),
            pltpu.SemaphoreType.DMA((2,)),
        ],
    )
    return pl.pallas_call(
        _expert_kernel,
        grid_spec=grid_spec,
        out_shape=jax.ShapeDtypeStruct((n_pad, dw), jnp.uint32),
        compiler_params=pltpu.CompilerParams(
            dimension_semantics=("arbitrary",), vmem_limit_bytes=VMEM_LIMIT_BYTES),
        name="expert_ffn",
    )(block_e, block_next, xs, w_gu, b_gu, w_down, b_down)


def _combine_kernel(x1_ref, gate_ref, y4_ref, g_ref, out_ref):
    x1 = x1_ref[...]
    n = x1.shape[1] // 2
    gate = gate_ref[...]
    acc_lo, acc_hi = x1[:, :n], x1[:, n:]
    for k in range(TOP_K):
        y_lo, y_hi = _unpack_bf16_pairs(y4_ref[k])
        acc_lo = acc_lo + gate[:, k:k + 1] * y_lo
        acc_hi = acc_hi + gate[:, k:k + 1] * y_hi
    out_ref[...] = _rms(jnp.concatenate([acc_lo, acc_hi], axis=1), g_ref[...])


def _combine_call(x1, gate_t, y4, g_final):
    t, d = x1.shape
    tt = COMBINE_ROWS
    return pl.pallas_call(
        _combine_kernel,
        grid=(t // tt,),
        in_specs=[
            pl.BlockSpec((tt, d), lambda i: (i, 0)),
            pl.BlockSpec((tt, TOP_K), lambda i: (i, 0)),
            pl.BlockSpec((TOP_K, tt, d // 2), lambda i: (0, i, 0)),
            pl.BlockSpec((1, d), lambda i: (0, 0)),
        ],
        out_specs=pl.BlockSpec((tt, d), lambda i: (i, 0)),
        out_shape=jax.ShapeDtypeStruct((t, d), F32),
        compiler_params=pltpu.CompilerParams(dimension_semantics=("arbitrary",)),
        name="combine_norm",
    )(x1, gate_t, y4, g_final)


def _block_diag(w):
    hh, n, _ = w.shape
    eye = jnp.eye(hh, dtype=w.dtype)
    return jnp.einsum("hij,hg->higj", w, eye).reshape(hh * n, hh * n)


def kernel(x, mem, norm_mix_g, w_in, conv_w, conv_b, w_rg_a, b_rg_a, w_rg_i, b_rg_i, lru_lambda, w_pool,
           pool_scale, mem_norm_g, w_mem_kv, w_out, norm_ffn_g, w_router, b_router, w_gu, b_gu, w_down,
           b_down, final_norm_g):
    b, s, d = x.shape
    t = b * s
    l = 0
    row = lambda v: v.reshape(1, -1)

    kv = _kv_call(mem, row(mem_norm_g[l]), w_mem_kv[l].astype(BF16))
    kh = kv[..., :D_XATTN].reshape(b, N_MEM, XATTN_HEADS, XATTN_HEAD_DIM)
    vh = kv[..., D_XATTN:].reshape(b, N_MEM, XATTN_HEADS, XATTN_HEAD_DIM)
    eye_h = jnp.eye(XATTN_HEADS, dtype=F32)
    kbd = jnp.einsum("bmhd,hg->bhdgm", kh, eye_h).reshape(b, D_XATTN, XATTN_HEADS * N_MEM).astype(BF16)
    vbd = jnp.einsum("bmhd,hg->bgmhd", vh, eye_h).reshape(b, XATTN_HEADS * N_MEM, D_XATTN).astype(BF16)

    heads_per = MXU_DIM // RNN_HEAD_DIM
    wg = jnp.stack([
        jnp.concatenate([_block_diag(w_rg_a[l, c * heads_per:(c + 1) * heads_per]),
                         _block_diag(w_rg_i[l, c * heads_per:(c + 1) * heads_per])], axis=1)
        for c in range(D_RNN // MXU_DIM)]).astype(BF16)

    x1, hn, idx, gate = _mix_call(
        x, row(norm_mix_g[l]), w_in[l].astype(BF16), conv_w[l], row(conv_b[l]), wg, row(b_rg_a[l]),
        row(b_rg_i[l]), row(lru_lambda[l]), _block_diag(w_pool[l]).astype(BF16), row(pool_scale[l]), kbd, vbd,
        w_out[l].astype(BF16), row(norm_ffn_g[l]), w_router[l].T.astype(BF16), b_router[l].reshape(-1, 1))

    hn = hn.reshape(t, d // 2)
    idx = jnp.transpose(idx, (1, 0, 2)).reshape(TOP_K, t)
    gate_t = jnp.transpose(gate, (0, 2, 1)).reshape(t, TOP_K)
    n_blocks = _num_blocks(t)
    dest, meta = _route_call(idx)
    be, nx, n_used = meta[0, :n_blocks], meta[6, :n_blocks], meta[5, 0]
    fill_start, fill_n = meta[3, :N_EXPERTS], meta[4, :N_EXPERTS]

    xs = _sc_dispatch_call(hn, dest, n_blocks * EXPERT_ROWS)
    xs = _pad_fill_call(fill_start, fill_n, xs)
    ys = _expert_call(n_used, be, nx, xs, w_gu[l], b_gu[l].reshape(N_EXPERTS, 1, -1), w_down[l],
                      b_down[l].reshape(N_EXPERTS, 1, -1))
    y4 = _sc_gather_call(ys, dest)
    out = _combine_call(x1.reshape(t, d), gate_t, y4, row(final_norm_g))
    return out.reshape(b, s, d)
```

```python
import functools

import jax
import jax.numpy as jnp
from jax import lax
from jax.experimental import pallas as pl
from jax.experimental.pallas import tpu as pltpu
from jax.experimental.pallas import tpu_sc as plsc

D_MODEL = 1024
N_MEM = 256
D_RNN = 512
RNN_HEADS = 8
RNN_HEAD_DIM = D_RNN // RNN_HEADS
CONV_WIDTH = 4
LRU_C = 8.0
D_POOL = 256
POOL_WINDOWS = (2, 4, 8, 16)
POOL_GROUP_DIM = D_POOL // len(POOL_WINDOWS)
MAX_WINDOW = max(POOL_WINDOWS)
D_XATTN = 256
XATTN_HEADS = 4
XATTN_HEAD_DIM = D_XATTN // XATTN_HEADS
D_IN = 2 * D_RNN + D_POOL + D_XATTN
N_EXPERTS = 32
TOP_K = 4
D_FF = D_MODEL
SWIGLU_ALPHA = 1.702
SWIGLU_LIMIT = 7.0
RMS_EPS = 1e-6
SQRT_GUARD = 1e-30

SUBLANES = 8
MXU_DIM = 256
VMEM_LIMIT_BYTES = 56 * 1024 * 1024

MIX_ROWS = 512
HIST = 16
assert HIST % SUBLANES == 0 and HIST >= max(MAX_WINDOW, CONV_WIDTH) - 1
EXPERT_ROWS = 352
COMBINE_ROWS = 512
ROUTE_CHUNK = 1024
SC_ROWS = 128
MIX_PHASE_STAGES = (1 + D_IN // MXU_DIM, XATTN_HEADS + 10, D_MODEL // MXU_DIM + 2)

BF16 = jnp.bfloat16
F32 = jnp.float32


def _rms(xf, g):
    return xf * lax.rsqrt(jnp.mean(xf * xf, axis=-1, keepdims=True) + RMS_EPS) * g


def _dot(a, b):
    return jnp.dot(a, b, preferred_element_type=F32)


def _pack_bf16_pairs(v):
    n = v.shape[1] // 2
    bits = lax.bitcast_convert_type(v.astype(BF16).astype(F32), jnp.uint32)
    return (bits[:, :n] >> 16) | (bits[:, n:] & jnp.uint32(0xFFFF0000))


def _unpack_bf16_pairs(w):
    lo = lax.bitcast_convert_type(w << 16, F32)
    hi = lax.bitcast_convert_type(w & jnp.uint32(0xFFFF0000), F32)
    return lo, hi


def _kv_kernel(mem_ref, g_ref, w_ref, kv_ref):
    mn = _rms(mem_ref[0], g_ref[...])
    kv_ref[0] = _dot(mn.astype(BF16), w_ref[...])


def _kv_call(mem, g, w_kv_bf):
    b = mem.shape[0]
    return pl.pallas_call(
        _kv_kernel,
        grid=(b,),
        in_specs=[
            pl.BlockSpec((1, N_MEM, D_MODEL), lambda i: (i, 0, 0)),
            pl.BlockSpec((1, D_MODEL), lambda i: (0, 0)),
            pl.BlockSpec((D_MODEL, 2 * D_XATTN), lambda i: (0, 0)),
        ],
        out_specs=pl.BlockSpec((1, N_MEM, 2 * D_XATTN), lambda i: (i, 0, 0)),
        out_shape=jax.ShapeDtypeStruct((b, N_MEM, 2 * D_XATTN), F32),
        name="kv_proj",
    )(mem, g, w_kv_bf)


def _mix_kernel(x_ref, g_mix_ref, w_in_ref, conv_w_ref, conv_b_ref, wg_ref, b_a_ref, b_i_ref, lam_ref,
                w_pool_ref, pool_scale_ref, kbd_ref, vbd_ref, w_out_ref, g_ffn_ref, w_rt_ref, b_r_ref,
                x1_ref, hn_ref, idx_ref, gate_ref,
                urnn_buf, upool_buf, hcar, mixed_s):
    tiles = [
        _mix_tile(x_ref.at[b], g_mix_ref, w_in_ref, conv_w_ref, conv_b_ref, wg_ref, b_a_ref, b_i_ref, lam_ref,
                  w_pool_ref, pool_scale_ref, kbd_ref.at[b], vbd_ref.at[b], w_out_ref, g_ffn_ref, w_rt_ref,
                  b_r_ref, x1_ref.at[b], hn_ref.at[b], idx_ref.at[b], gate_ref.at[b],
                  urnn_buf.at[b], upool_buf.at[b], hcar.at[b], mixed_s.at[b])
        for b in range(x_ref.shape[0])]
    n_phase = len(MIX_PHASE_STAGES)
    for step in range(len(tiles) + n_phase - 1):
        active = [(tiles[j], MIX_PHASE_STAGES[step - j]) for j in range(len(tiles)) if 0 <= step - j < n_phase]
        ticks = max(n for _, n in active)
        done = [0] * len(active)
        for i in range(ticks):
            for a, (tile, n) in enumerate(active):
                while done[a] * ticks < (i + 1) * n:
                    next(tile, None)
                    done[a] += 1
    done_mark = object()
    assert all(next(tile, done_mark) is done_mark for tile in tiles), "MIX_PHASE_STAGES does not cover _mix_tile"


def _mix_tile(x_ref, g_mix_ref, w_in_ref, conv_w_ref, conv_b_ref, wg_ref, b_a_ref, b_i_ref, lam_ref,
              w_pool_ref, pool_scale_ref, kbd_ref, vbd_ref, w_out_ref, g_ffn_ref, w_rt_ref, b_r_ref,
              x1_ref, hn_ref, idx_ref, gate_ref,
              urnn_buf, upool_buf, hcar, mixed_s):
    tm = MIX_ROWS
    cw = MXU_DIM
    s = pl.program_id(0)

    @pl.when(s == 0)
    def _():
        urnn_buf[0:HIST, :] = jnp.zeros((HIST, D_RNN), F32)
        upool_buf[0:HIST, :] = jnp.zeros((HIST, D_POOL), F32)
        hcar[...] = jnp.zeros((1, D_RNN), F32)

    x = x_ref[...]
    hb = _rms(x, g_mix_ref[...]).astype(BF16)
    yield
    proj = []
    for c in range(D_IN // cw):
        proj.append(_dot(hb, w_in_ref[:, c * cw:(c + 1) * cw]))
        yield
    urnn_buf[HIST:, :] = jnp.concatenate(proj[:D_RNN // cw], axis=1)
    g_rnn = jnp.concatenate(proj[D_RNN // cw:2 * D_RNN // cw], axis=1)
    upool_buf[HIST:, :] = proj[2 * D_RNN // cw]
    q_mem = proj[2 * D_RNN // cw + 1]

    row = lax.broadcasted_iota(jnp.int32, (tm, 1), 0)
    grow = row + s * tm

    e = upool_buf[...]
    s2 = e + pltpu.roll(e, 1, axis=0)
    s4 = s2 + pltpu.roll(s2, 2, axis=0)
    s8 = s4 + pltpu.roll(s4, 4, axis=0)
    s16 = s8 + pltpu.roll(s8, 8, axis=0)
    grp = lax.broadcasted_iota(jnp.int32, (tm, D_POOL), 1) // POOL_GROUP_DIM
    wsum = jnp.where(grp == 0, s2[HIST:], jnp.where(grp == 1, s4[HIST:], jnp.where(grp == 2, s8[HIST:], s16[HIST:])))
    win = jnp.where(grp == 0, 2.0, jnp.where(grp == 1, 4.0, jnp.where(grp == 2, 8.0, 16.0)))
    pos = (grow + 1).astype(F32)
    dpool = wsum / jnp.minimum(pos, win) - e[HIST:]
    y_pool = _dot(dpool.astype(BF16), w_pool_ref[...]) * pool_scale_ref[...]
    mixed_s[:, D_RNN:D_RNN + D_POOL] = y_pool.astype(BF16)
    yield

    sc = _dot(q_mem.astype(BF16), kbd_ref[...]) * (XATTN_HEAD_DIM ** -0.5)
    ps = []
    for hh in range(XATTN_HEADS):
        sh = sc[:, hh * N_MEM:(hh + 1) * N_MEM]
        ph = jnp.exp(sh - jnp.max(sh, axis=-1, keepdims=True))
        ps.append((ph * (1.0 / jnp.sum(ph, axis=-1, keepdims=True))).astype(BF16))
        yield
    y_mem = _dot(jnp.concatenate(ps, axis=1), vbd_ref[...])
    mixed_s[:, D_RNN + D_POOL:] = y_mem.astype(BF16)
    yield

    uc = conv_b_ref[...] + conv_w_ref[CONV_WIDTH - 1:CONV_WIDTH, :] * urnn_buf[pl.ds(HIST, tm), :]
    for k in range(CONV_WIDTH - 1):
        off = HIST - (CONV_WIDTH - 1) + k
        uc = uc + conv_w_ref[k:k + 1, :] * urnn_buf[pl.ds(off, tm), :]
    yield

    ucb = uc.astype(BF16)
    pre = [_dot(ucb[:, c * cw:(c + 1) * cw], wg_ref[c]) for c in range(D_RNN // cw)]
    pre_a = jnp.concatenate([p[:, :cw] for p in pre], axis=1)
    pre_i = jnp.concatenate([p[:, cw:] for p in pre], axis=1)
    r = jax.nn.sigmoid(pre_a + b_a_ref[...])
    ig = jax.nn.sigmoid(pre_i + b_i_ref[...])
    yield
    lam = lam_ref[...]
    softplus_neg = jnp.maximum(-lam, 0.0) + jnp.log(1.0 + jnp.exp(-jnp.abs(lam)))
    log_a = (-LRU_C) * r * softplus_neg
    a = jnp.exp(log_a)
    z = jnp.maximum(1.0 - a * a, 0.0)
    mult = z * lax.rsqrt(jnp.maximum(z, SQRT_GUARD))
    mult = jnp.where(grow == 0, 1.0, mult)
    bx = mult * (ig * uc)
    yield

    ng = tm // SUBLANES
    a = a.reshape(ng, SUBLANES, D_RNN)
    bx = bx.reshape(ng, SUBLANES, D_RNN)
    row8 = lax.broadcasted_iota(jnp.int32, (1, SUBLANES, D_RNN), 1)
    d = 1
    while d < SUBLANES:
        a_sh = pltpu.roll(a, d, axis=1)
        b_sh = pltpu.roll(bx, d, axis=1)
        m = row8 >= d
        bx = jnp.where(m, a * b_sh + bx, bx)
        a = jnp.where(m, a * a_sh, a)
        d *= 2
        yield
    carry = jnp.broadcast_to(hcar[...], (SUBLANES, D_RNN))
    hs = []
    for g in range(ng):
        hv = a[g] * carry + bx[g]
        hs.append(hv)
        carry = jnp.broadcast_to(hv[SUBLANES - 1:SUBLANES, :], (SUBLANES, D_RNN))
    hcar[...] = carry[0:1, :]
    yield
    y_rnn = jnp.concatenate(hs, axis=0) * jax.nn.gelu(g_rnn)
    mixed_s[:, :D_RNN] = y_rnn.astype(BF16)
    urnn_buf[0:HIST, :] = urnn_buf[tm:tm + HIST, :]
    upool_buf[0:HIST, :] = upool_buf[tm:tm + HIST, :]
    yield

    mixed = mixed_s[...]
    x1s = []
    for c in range(D_MODEL // cw):
        x1s.append(x[:, c * cw:(c + 1) * cw] + _dot(mixed, w_out_ref[:, c * cw:(c + 1) * cw]))
        yield
    x1 = jnp.concatenate(x1s, axis=1)
    x1_ref[...] = x1

    hn = _rms(x1, g_ffn_ref[...]).astype(BF16)
    hn_ref[...] = _pack_bf16_pairs(hn)
    yield
    logits = lax.dot_general(w_rt_ref[...], hn, (((1,), (1,)), ((), ())),
                             preferred_element_type=F32) + b_r_ref[...]
    eid = lax.broadcasted_iota(jnp.int32, (N_EXPERTS, tm), 0)
    vals, idxs = [], []
    for _ in range(TOP_K):
        mx = jnp.max(logits, axis=0, keepdims=True)
        ix = jnp.min(jnp.where(logits == mx, eid, N_EXPERTS), axis=0, keepdims=True)
        vals.append(mx)
        idxs.append(ix)
        logits = jnp.where(eid == ix, -jnp.inf, logits)
    ex = [jnp.exp(v - vals[0]) for v in vals]
    den = ex[0] + ex[1] + ex[2] + ex[3]
    idx_ref[...] = jnp.concatenate(idxs, axis=0)
    gate_ref[...] = jnp.concatenate([e_ / den for e_ in ex], axis=0)


def _mix_call(x, g_mix, w_in, conv_w, conv_b, wg, b_a, b_i, lam, w_pool, pool_scale, kbd, vbd, w_out,
              g_ffn, w_rt, b_r):
    b, s, d = x.shape
    tm = MIX_ROWS
    ns = s // tm
    const2 = lambda shape: pl.BlockSpec(shape, lambda si: (0, 0))
    const3 = lambda shape: pl.BlockSpec(shape, lambda si: (0, 0, 0))
    return pl.pallas_call(
        _mix_kernel,
        grid=(ns,),
        in_specs=[
            pl.BlockSpec((b, tm, d), lambda si: (0, si, 0)),
            const2((1, d)),
            const2((d, D_IN)),
            const2((CONV_WIDTH, D_RNN)),
            const2((1, D_RNN)),
            const3((D_RNN // MXU_DIM, MXU_DIM, 2 * MXU_DIM)),
            const2((1, D_RNN)),
            const2((1, D_RNN)),
            const2((1, D_RNN)),
            const2((D_POOL, D_POOL)),
            const2((1, D_POOL)),
            const3((b, D_XATTN, XATTN_HEADS * N_MEM)),
            const3((b, XATTN_HEADS * N_MEM, D_XATTN)),
            const2((d, d)),
            const2((1, d)),
            const2((N_EXPERTS, d)),
            const2((N_EXPERTS, 1)),
        ],
        out_specs=[
            pl.BlockSpec((b, tm, d), lambda si: (0, si, 0)),
            pl.BlockSpec((b, tm, d // 2), lambda si: (0, si, 0)),
            pl.BlockSpec((b, TOP_K, tm), lambda si: (0, 0, si)),
            pl.BlockSpec((b, TOP_K, tm), lambda si: (0, 0, si)),
        ],
        out_shape=[
            jax.ShapeDtypeStruct((b, s, d), F32),
            jax.ShapeDtypeStruct((b, s, d // 2), jnp.uint32),
            jax.ShapeDtypeStruct((b, TOP_K, s), jnp.int32),
            jax.ShapeDtypeStruct((b, TOP_K, s), F32),
        ],
        scratch_shapes=[
            pltpu.VMEM((b, HIST + tm, D_RNN), F32),
            pltpu.VMEM((b, HIST + tm, D_POOL), F32),
            pltpu.VMEM((b, 1, D_RNN), F32),
            pltpu.VMEM((b, tm, d), BF16),
        ],
        compiler_params=pltpu.CompilerParams(
            dimension_semantics=("arbitrary",), vmem_limit_bytes=VMEM_LIMIT_BYTES),
        name="mixer_router",
    )(x, g_mix, w_in, conv_w, conv_b, wg, b_a, b_i, lam, w_pool, pool_scale, kbd, vbd, w_out, g_ffn, w_rt, b_r)


def _num_blocks(t):
    bm = EXPERT_ROWS
    return (t * TOP_K + N_EXPERTS * (bm - 1) + bm - 1) // bm


def _route_kernel(idx_ref, dest_ref, meta_ref, rank_s, carry_s, start_s, tri_s):
    p = pl.program_id(0)
    c = pl.program_id(1)
    nc = pl.num_programs(1)
    ch = ROUTE_CHUNK
    bm = float(EXPERT_ROWS)
    nbl = meta_ref.shape[1]
    eid = lax.broadcasted_iota(jnp.int32, (N_EXPERTS, ch), 0)
    idxc = idx_ref[...]

    @pl.when(jnp.logical_and(p == 0, c == 0))
    def _():
        carry_s[...] = jnp.zeros(carry_s.shape, F32)
        tri_s[...] = (lax.broadcasted_iota(jnp.int32, (ch, ch), 0)
                      < lax.broadcasted_iota(jnp.int32, (ch, ch), 1)).astype(BF16)

    @pl.when(p == 0)
    def _():
        sel = jnp.zeros((N_EXPERTS, ch), F32)
        for k in range(TOP_K):
            sel = sel + (idxc[k:k + 1, :] == eid).astype(F32)
        rank_s[c] = _dot(sel.astype(BF16), tri_s[...]) + carry_s[:, 0:1]
        carry_s[...] = carry_s[...] + jnp.sum(sel, axis=1, keepdims=True)

    @pl.when(jnp.logical_and(p == 0, c == nc - 1))
    def _():
        counts = carry_s[...]

        def div_bm(v):
            q = jnp.floor(v * (1.0 / bm))
            return q + jnp.where((q + 1.0) * bm <= v, 1.0, 0.0) - jnp.where(q * bm > v, 1.0, 0.0)

        padded = div_bm(counts + (bm - 1.0)) * bm
        e128 = lax.broadcasted_iota(jnp.int32, counts.shape, 0)
        pad_end = padded
        sh = 1
        while sh < N_EXPERTS:
            pad_end = pad_end + jnp.where(e128 >= sh, pltpu.roll(pad_end, sh, axis=0), 0.0)
            sh *= 2
        pad_start = pad_end - padded
        start_s[...] = pad_start
        total = pad_end[N_EXPERTS - 1:N_EXPERTS, 0:1]
        lane = lax.broadcasted_iota(jnp.int32, (1, nbl), 1)
        lane_f = lane.astype(F32)
        n_used = div_bm(total)
        pe, ps, cn, pd = pad_end[:, 0:1], pad_start[:, 0:1], counts[:, 0:1], padded[:, 0:1]
        be = jnp.minimum(jnp.sum((pe <= lane_f * bm).astype(F32), axis=0, keepdims=True), N_EXPERTS - 1.0)
        esub = lax.broadcasted_iota(jnp.int32, (N_EXPERTS, nbl), 0)
        own = esub == lane
        fill_start = jnp.sum(jnp.where(own, ps + cn, 0.0), axis=0, keepdims=True)
        fill_n = jnp.sum(jnp.where(own, pd - cn, 0.0), axis=0, keepdims=True)
        later = jnp.logical_and(esub.astype(F32) > be, cn > 0.0)
        nxt = jnp.min(jnp.where(later, esub.astype(F32), float(N_EXPERTS)), axis=0, keepdims=True)
        nxt = jnp.where(nxt == float(N_EXPERTS), -1.0, nxt)
        zero = jnp.zeros((1, nbl), F32)
        meta_ref[...] = jnp.concatenate(
            [be, zero, zero, fill_start, fill_n, jnp.broadcast_to(n_used, (1, nbl)), nxt, zero],
            axis=0).astype(jnp.int32)

    @pl.when(p == 1)
    def _():
        base = start_s[:, 0:1] + rank_s[c]
        rows = [jnp.sum(jnp.where(idxc[k:k + 1, :] == eid, base, 0.0), axis=0, keepdims=True)
                for k in range(TOP_K)]
        dest_ref[...] = jnp.concatenate(rows, axis=0).astype(jnp.int32)


def _route_call(idx):
    k, t = idx.shape
    ch = ROUTE_CHUNK
    nc = t // ch
    nbl = -(-_num_blocks(t) // 128) * 128
    return pl.pallas_call(
        _route_kernel,
        grid=(2, nc),
        in_specs=[pl.BlockSpec((k, ch), lambda p, c: (0, c))],
        out_specs=[
            pl.BlockSpec((k, ch), lambda p, c: (0, c * p)),
            pl.BlockSpec((8, nbl), lambda p, c: (0, 0)),
        ],
        out_shape=[
            jax.ShapeDtypeStruct((k, t), jnp.int32),
            jax.ShapeDtypeStruct((8, nbl), jnp.int32),
        ],
        scratch_shapes=[
            pltpu.VMEM((nc, N_EXPERTS, ch), F32),
            pltpu.VMEM((N_EXPERTS, 128), F32),
            pltpu.VMEM((N_EXPERTS, 128), F32),
            pltpu.VMEM((ch, ch), BF16),
        ],
        compiler_params=pltpu.CompilerParams(dimension_semantics=("arbitrary", "arbitrary")),
        name="route_tables",
    )(idx)


def _sc_workers():
    info = plsc.get_sparse_core_info()
    return info.num_cores, info.num_cores * info.num_subcores


def _sc_dispatch_call(hn, dest, n_rows):
    t, w = hn.shape
    nc, nw = _sc_workers()
    per_w = t // nw
    ch = SC_ROWS
    mesh = plsc.VectorSubcoreMesh(core_axis_name="c", subcore_axis_name="s")

    @functools.partial(
        pl.kernel, mesh=mesh,
        out_type=jax.ShapeDtypeStruct((n_rows, w), hn.dtype),
        scratch_types=[pltpu.VMEM((ch,), jnp.int32), pltpu.VMEM((ch, w), hn.dtype)],
    )
    def body(hn_hbm, dest_hbm, xs_hbm, idx_v, rows_v):
        wid = lax.axis_index("s") * nc + lax.axis_index("c")

        @pl.loop(0, per_w // ch)
        def _(ci):
            base = pl.multiple_of(wid * per_w + ci * ch, ch)
            pltpu.sync_copy(hn_hbm.at[pl.ds(base, ch)], rows_v)
            for k in range(TOP_K):
                pltpu.sync_copy(dest_hbm.at[k, pl.ds(base, ch)], idx_v)
                pltpu.sync_copy(rows_v, xs_hbm.at[idx_v])

    return body(hn, dest)


def _sc_gather_call(ys, dest):
    k_, t = dest.shape
    w = ys.shape[1]
    nc, nw = _sc_workers()
    per_w = t // nw
    ch = SC_ROWS
    mesh = plsc.VectorSubcoreMesh(core_axis_name="c", subcore_axis_name="s")

    @functools.partial(
        pl.kernel, mesh=mesh,
        out_type=jax.ShapeDtypeStruct((k_, t, w), ys.dtype),
        scratch_types=[pltpu.VMEM((ch,), jnp.int32), pltpu.VMEM((ch, w), ys.dtype)],
    )
    def body(ys_hbm, dest_hbm, y4_hbm, idx_v, rows_v):
        wid = lax.axis_index("s") * nc + lax.axis_index("c")

        @pl.loop(0, per_w // ch)
        def _(ci):
            base = pl.multiple_of(wid * per_w + ci * ch, ch)
            for k in range(k_):
                pltpu.sync_copy(dest_hbm.at[k, pl.ds(base, ch)], idx_v)
                pltpu.sync_copy(ys_hbm.at[idx_v], rows_v)
                pltpu.sync_copy(rows_v, y4_hbm.at[k, pl.ds(base, ch)])

    return body(ys, dest)


def _pad_fill_kernel(fill_start_ref, fill_n_ref, xs_in_ref, xs_ref, zero_s, zsem):
    del xs_in_ref
    bm = EXPERT_ROWS
    zero_s[...] = jnp.zeros(zero_s.shape, zero_s.dtype)

    def go(cp, wait):
        if wait:
            cp.wait()
        else:
            cp.start()

    def fill(e, wait):
        n = fill_n_ref[e]
        st = fill_start_ref[e]
        head = n & (SUBLANES - 1)
        for j in range(SUBLANES - 1):
            cp = pltpu.make_async_copy(zero_s.at[pl.ds(0, 1), :], xs_ref.at[pl.ds(st + j, 1), :], zsem)
            pl.when(j < head)(functools.partial(go, cp, wait))
        st = pl.multiple_of(st + head, SUBLANES)
        bit = 1 << ((bm - 1).bit_length() - 1)
        while bit >= SUBLANES:
            cp = pltpu.make_async_copy(zero_s.at[pl.ds(0, bit), :], xs_ref.at[pl.ds(st, bit), :], zsem)
            pl.when((n & bit) != 0)(functools.partial(go, cp, wait))
            st = pl.multiple_of(st + (n & bit), SUBLANES)
            bit //= 2

    for wait in (False, True):
        lax.fori_loop(0, N_EXPERTS, lambda e, c, wait=wait: (fill(e, wait), c)[1], 0)


def _pad_fill_call(fill_start, fill_n, xs):
    bit_rows = 1 << ((EXPERT_ROWS - 1).bit_length() - 1)
    grid_spec = pltpu.PrefetchScalarGridSpec(
        num_scalar_prefetch=2,
        grid=(1,),
        in_specs=[pl.BlockSpec(memory_space=pl.ANY)],
        out_specs=pl.BlockSpec(memory_space=pl.ANY),
        scratch_shapes=[pltpu.VMEM((bit_rows, xs.shape[1]), xs.dtype), pltpu.SemaphoreType.DMA],
    )
    return pl.pallas_call(
        _pad_fill_kernel,
        grid_spec=grid_spec,
        out_shape=jax.ShapeDtypeStruct(xs.shape, xs.dtype),
        input_output_aliases={2: 0},
        compiler_params=pltpu.CompilerParams(dimension_semantics=("arbitrary",)),
        name="pad_fill",
    )(fill_start, fill_n, xs)


def _expert_kernel(be_ref, nx_ref, x_ref, wgu_hbm, bgu_ref, wd_hbm, bd_ref, y_ref,
                   wgu_st, wd_st, wgu_bf, wd_bf, wsem):
    i = pl.program_id(0)
    e = be_ref[i]
    e_prev = be_ref[jnp.maximum(i - 1, 0)]

    def weight_copies(ex):
        return (pltpu.make_async_copy(wgu_hbm.at[ex], wgu_st, wsem.at[0]),
                pltpu.make_async_copy(wd_hbm.at[ex], wd_st, wsem.at[1]))

    @pl.when(i == 0)
    def _():
        for cp in weight_copies(e):
            cp.start()

    @pl.when(jnp.logical_or(i == 0, e != e_prev))
    def _():
        for cp in weight_copies(e):
            cp.wait()
        wgu_bf[...] = wgu_st[...].astype(BF16)
        wd_bf[...] = wd_st[...].astype(BF16)
        nxt = nx_ref[i]

        @pl.when(nxt >= 0)
        def _():
            for cp in weight_copies(nxt):
                cp.start()

    x_lo, x_hi = _unpack_bf16_pairs(x_ref[...])
    xb = jnp.concatenate([x_lo.astype(BF16), x_hi.astype(BF16)], axis=1)
    hs = []
    for c in range(D_FF // (2 * MXU_DIM)):
        lo = c * 2 * MXU_DIM
        hi = lo + 2 * MXU_DIM
        g = _dot(xb, wgu_bf[:, lo:hi]) + bgu_ref[0, :, lo:hi]
        up = _dot(xb, wgu_bf[:, D_FF + lo:D_FF + hi]) + bgu_ref[0, :, D_FF + lo:D_FF + hi]
        g = jnp.minimum(g, SWIGLU_LIMIT)
        up = jnp.clip(up, -SWIGLU_LIMIT, SWIGLU_LIMIT)
        glu = g * jax.nn.sigmoid(g * SWIGLU_ALPHA)
        hs.append(((up + 1.0) * glu).astype(BF16))
    y_ref[...] = _pack_bf16_pairs(_dot(jnp.concatenate(hs, axis=1), wd_bf[...]) + bd_ref[0])


def _expert_call(n_used, block_e, block_next, xs, w_gu, b_gu, w_down, b_down):
    n_pad, dw = xs.shape
    d = 2 * dw
    bm = EXPERT_ROWS
    grid_spec = pltpu.PrefetchScalarGridSpec(
        num_scalar_prefetch=2,
        grid=(n_used,),
        in_specs=[
            pl.BlockSpec((bm, dw), lambda i, be, nx: (i, 0)),
            pl.BlockSpec(memory_space=pl.ANY),
            pl.BlockSpec((1, 1, 2 * D_FF), lambda i, be, nx: (be[i], 0, 0)),
            pl.BlockSpec(memory_space=pl.ANY),
            pl.BlockSpec((1, 1, d), lambda i, be, nx: (be[i], 0, 0)),
        ],
        out_specs=pl.BlockSpec((bm, dw), lambda i, be, nx: (i, 0)),
        scratch_shapes=[
            pltpu.VMEM((d, 2 * D_FF), F32),
            pltpu.VMEM((D_FF, d), F32),
            pltpu.VMEM((d, 2 * D_FF), BF16),
            pltpu.VMEM((D_FF, d), BF16),
            pltpu.SemaphoreType.DMA((2,)),
        ],
    )
    return pl.pallas_call(
        _expert_kernel,
        grid_spec=grid_spec,
        out_shape=jax.ShapeDtypeStruct((n_pad, dw), jnp.uint32),
        compiler_params=pltpu.CompilerParams(
            dimension_semantics=("arbitrary",), vmem_limit_bytes=VMEM_LIMIT_BYTES),
        name="expert_ffn",
    )(block_e, block_next, xs, w_gu, b_gu, w_down, b_down)


def _combine_kernel(x1_ref, gate_ref, y4_ref, g_ref, out_ref):
    x1 = x1_ref[...]
    n = x1.shape[1] // 2
    gate = gate_ref[...]
    acc_lo, acc_hi = x1[:, :n], x1[:, n:]
    for k in range(TOP_K):
        y_lo, y_hi = _unpack_bf16_pairs(y4_ref[k])
        acc_lo = acc_lo + gate[:, k:k + 1] * y_lo
        acc_hi = acc_hi + gate[:, k:k + 1] * y_hi
    out_ref[...] = _rms(jnp.concatenate([acc_lo, acc_hi], axis=1), g_ref[...])


def _combine_call(x1, gate_t, y4, g_final):
    t, d = x1.shape
    tt = COMBINE_ROWS
    return pl.pallas_call(
        _combine_kernel,
        grid=(t // tt,),
        in_specs=[
            pl.BlockSpec((tt, d), lambda i: (i, 0)),
            pl.BlockSpec((tt, TOP_K), lambda i: (i, 0)),
            pl.BlockSpec((TOP_K, tt, d // 2), lambda i: (0, i, 0)),
            pl.BlockSpec((1, d), lambda i: (0, 0)),
        ],
        out_specs=pl.BlockSpec((tt, d), lambda i: (i, 0)),
        out_shape=jax.ShapeDtypeStruct((t, d), F32),
        compiler_params=pltpu.CompilerParams(dimension_semantics=("arbitrary",)),
        name="combine_norm",
    )(x1, gate_t, y4, g_final)


def _block_diag(w):
    hh, n, _ = w.shape
    eye = jnp.eye(hh, dtype=w.dtype)
    return jnp.einsum("hij,hg->higj", w, eye).reshape(hh * n, hh * n)


def kernel(x, mem, norm_mix_g, w_in, conv_w, conv_b, w_rg_a, b_rg_a, w_rg_i, b_rg_i, lru_lambda, w_pool,
           pool_scale, mem_norm_g, w_mem_kv, w_out, norm_ffn_g, w_router, b_router, w_gu, b_gu, w_down,
           b_down, final_norm_g):
    b, s, d = x.shape
    t = b * s
    l = 0
    row = lambda v: v.reshape(1, -1)

    kv = _kv_call(mem, row(mem_norm_g[l]), w_mem_kv[l].astype(BF16))
    kh = kv[..., :D_XATTN].reshape(b, N_MEM, XATTN_HEADS, XATTN_HEAD_DIM)
    vh = kv[..., D_XATTN:].reshape(b, N_MEM, XATTN_HEADS, XATTN_HEAD_DIM)
    eye_h = jnp.eye(XATTN_HEADS, dtype=F32)
    kbd = jnp.einsum("bmhd,hg->bhdgm", kh, eye_h).reshape(b, D_XATTN, XATTN_HEADS * N_MEM).astype(BF16)
    vbd = jnp.einsum("bmhd,hg->bgmhd", vh, eye_h).reshape(b, XATTN_HEADS * N_MEM, D_XATTN).astype(BF16)

    heads_per = MXU_DIM // RNN_HEAD_DIM
    wg = jnp.stack([
        jnp.concatenate([_block_diag(w_rg_a[l, c * heads_per:(c + 1) * heads_per]),
                         _block_diag(w_rg_i[l, c * heads_per:(c + 1) * heads_per])], axis=1)
        for c in range(D_RNN // MXU_DIM)]).astype(BF16)

    x1, hn, idx, gate = _mix_call(
        x, row(norm_mix_g[l]), w_in[l].astype(BF16), conv_w[l], row(conv_b[l]), wg, row(b_rg_a[l]),
        row(b_rg_i[l]), row(lru_lambda[l]), _block_diag(w_pool[l]).astype(BF16), row(pool_scale[l]), kbd, vbd,
        w_out[l].astype(BF16), row(norm_ffn_g[l]), w_router[l].T.astype(BF16), b_router[l].reshape(-1, 1))

    hn = hn.reshape(t, d // 2)
    idx = jnp.transpose(idx, (1, 0, 2)).reshape(TOP_K, t)
    gate_t = jnp.transpose(gate, (0, 2, 1)).reshape(t, TOP_K)
    n_blocks = _num_blocks(t)
    dest, meta = _route_call(idx)
    be, nx, n_used = meta[0, :n_blocks], meta[6, :n_blocks], meta[5, 0]
    fill_start, fill_n = meta[3, :N_EXPERTS], meta[4, :N_EXPERTS]

    xs = _sc_dispatch_call(hn, dest, n_blocks * EXPERT_ROWS)
    xs = _pad_fill_call(fill_start, fill_n, xs)
    ys = _expert_call(n_used, be, nx, xs, w_gu[l], b_gu[l].reshape(N_EXPERTS, 1, -1), w_down[l],
                      b_down[l].reshape(N_EXPERTS, 1, -1))
    y4 = _sc_gather_call(ys, dest)
    out = _combine_call(x1.reshape(t, d), gate_t, y4, row(final_norm_g))
    return out.reshape(b, s, d)
```
